```python
import math
import jax, jax.numpy as jnp
from jax import lax
import numpy as np

D_MODEL = 1024
BATCH = 4
SEQ = 4096
DEPTH = 1

GRID_W = 64
D_A = D_MODEL
A_BLOCKS = 16
A_BW = D_A // A_BLOCKS
CONV_W = 4
RG_C = 8.0
N_HEADS = 16
HEAD_DIM = 64
D_B = N_HEADS * HEAD_DIM
WIN_R = 8
WIN_C = 16
IN_SPLITS = (D_A, D_A, D_B, D_B, D_B, D_B, D_MODEL, D_MODEL)
IN_COLS = sum(IN_SPLITS)
EPS = 1e-6

kernel_name = "hybrid_rglru_natten_gated_encoder"


def rms_norm(x, g):
    x32 = x.astype(jnp.float32)
    y = x32 * lax.rsqrt(jnp.mean(x32 * x32, axis=-1, keepdims=True) + EPS)
    return (y * g.astype(jnp.float32)).astype(x.dtype)


def block_diag(x, w, b):
    B, S, C = x.shape
    xr = x.reshape(B, S, A_BLOCKS, A_BW)
    return jnp.einsum('bshi,hij->bshj', xr, w).reshape(B, S, C) + b


def centred_dwconv(x, w, b):
    C = x.shape[-1]
    y = lax.conv_general_dilated(
        x, w[:, None, :].astype(x.dtype), window_strides=(1,),
        padding=[(CONV_W // 2, CONV_W - 1 - CONV_W // 2)],
        dimension_numbers=('NWC', 'WIO', 'NWC'), feature_group_count=C)
    return y + b


def _lin_combine(e1, e2):
    a1, b1 = e1
    a2, b2 = e2
    return a1 * a2, a2 * b1 + b2


def rg_lru(x, w_r, b_r, w_i, b_i, lam, reverse):
    x32 = x.astype(jnp.float32)
    r = jax.nn.sigmoid(block_diag(x, w_r, b_r).astype(jnp.float32))
    i = jax.nn.sigmoid(block_diag(x, w_i, b_i).astype(jnp.float32))
    log_a = -RG_C * r * jax.nn.softplus(-lam.astype(jnp.float32))
    a = jnp.exp(log_a)
    u = jnp.sqrt(-jnp.expm1(2.0 * log_a)) * (i * x32)
    _, h = lax.associative_scan(_lin_combine, (a, u), axis=1, reverse=reverse)
    return h


def neighbourhood_attention(q, k, v, rpb):
    B, S, H, dh = q.shape
    rows = S // GRID_W
    win_r = min(WIN_R, rows)
    qg = q.reshape(B, rows, GRID_W, H, dh)
    kg = k.reshape(B, rows, GRID_W, H, dh)
    vg = v.reshape(B, rows, GRID_W, H, dh)
    cols = jnp.arange(GRID_W)
    c0 = jnp.clip(cols - WIN_C // 2, 0, GRID_W - WIN_C)
    cidx = c0[:, None] + jnp.arange(WIN_C)[None, :]
    dcol = cidx - cols[:, None] + (WIN_C - 1)
    scale = dh ** -0.5

    def row_fn(r):
        r0 = jnp.clip(r - win_r // 2, 0, rows - win_r)
        q_r = lax.dynamic_index_in_dim(qg, r, axis=1, keepdims=False)
        k_rows = lax.dynamic_slice_in_dim(kg, r0, win_r, axis=1)
        v_rows = lax.dynamic_slice_in_dim(vg, r0, win_r, axis=1)
        k_win = k_rows[:, :, cidx]
        v_win = v_rows[:, :, cidx]
        s = jnp.einsum('bqhd,bwqkhd->bhqwk', q_r, k_win).astype(jnp.float32) * scale
        drow = r0 + jnp.arange(win_r) - r + (WIN_R - 1)
        bias = rpb[:, drow][:, :, dcol]
        s = s + jnp.transpose(bias, (0, 2, 1, 3)).astype(jnp.float32)[None]
        p = jax.nn.softmax(s.reshape(B, H, GRID_W, win_r * WIN_C), axis=-1)
        p = p.reshape(B, H, GRID_W, win_r, WIN_C).astype(v.dtype)
        return jnp.einsum('bhqwk,bwqkhd->bqhd', p, v_win)

    out = lax.map(row_fn, jnp.arange(rows))
    return jnp.transpose(out, (1, 0, 2, 3, 4)).reshape(B, S, H * dh)


def setup_inputs(seed: int = 0) -> dict:
    key = jax.random.key(seed)
    ks = jax.random.split(key, 26)
    n = lambda k, shp, s: jax.random.normal(k, shp, jnp.float32) * s

    def lam_init(k):
        a0 = jax.random.uniform(k, (DEPTH, D_A), jnp.float32, 0.9, 0.999)
        u = a0 ** (1.0 / RG_C)
        return jnp.log(u) - jnp.log1p(-u)

    return {
        "x": n(ks[0], (BATCH, SEQ, D_MODEL), 1.0),
        "c": n(ks[1], (BATCH, D_MODEL), 1.0),
        "g_pre": 1.0 + n(ks[2], (DEPTH, D_MODEL), 0.02),
        "w_c": n(ks[3], (DEPTH, D_MODEL, 3 * D_MODEL), D_MODEL ** -0.5),
        "b_c": n(ks[4], (DEPTH, 3 * D_MODEL), 0.02),
        "w_in": n(ks[5], (DEPTH, D_MODEL, IN_COLS), D_MODEL ** -0.5),
        "conv_w": n(ks[6], (DEPTH, CONV_W, D_A), CONV_W ** -0.5),
        "conv_b": n(ks[7], (DEPTH, D_A), 0.02),
        "w_r_f": n(ks[8], (DEPTH, A_BLOCKS, A_BW, A_BW), A_BW ** -0.5),
        "b_r_f": n(ks[9], (DEPTH, D_A), 0.02),
        "w_i_f": n(ks[10], (DEPTH, A_BLOCKS, A_BW, A_BW), A_BW ** -0.5),
        "b_i_f": n(ks[11], (DEPTH, D_A), 0.02),
        "lam_f": lam_init(ks[12]),
        "w_r_b": n(ks[13], (DEPTH, A_BLOCKS, A_BW, A_BW), A_BW ** -0.5),
        "b_r_b": n(ks[14], (DEPTH, D_A), 0.02),
        "w_i_b": n(ks[15], (DEPTH, A_BLOCKS, A_BW, A_BW), A_BW ** -0.5),
        "b_i_b": n(ks[16], (DEPTH, D_A), 0.02),
        "lam_b": lam_init(ks[17]),
        "rpb": n(ks[18], (DEPTH, N_HEADS, 2 * WIN_R - 1, 2 * WIN_C - 1), 0.1),
        "w_pa": n(ks[19], (DEPTH, D_A, D_MODEL), D_A ** -0.5),
        "w_pb": n(ks[20], (DEPTH, D_B, D_MODEL), D_B ** -0.5),
        "w_o": n(ks[21], (DEPTH, D_MODEL, D_MODEL), D_MODEL ** -0.5),
        "g_final": 1.0 + n(ks[22], (D_MODEL,), 0.02),
    }


def reference(x, c, g_pre, w_c, b_c, w_in, conv_w, conv_b, w_r_f, b_r_f, w_i_f, b_i_f, lam_f,
              w_r_b, b_r_b, w_i_b, b_i_b, lam_b, rpb, w_pa, w_pb, w_o, g_final):
    B, S, D = x.shape
    split_pts = list(np.cumsum(IN_SPLITS)[:-1])
    for l in range(DEPTH):
        mod = jax.nn.silu(c) @ w_c[l] + b_c[l]
        shift, scl, gate = jnp.split(mod, 3, axis=-1)
        h = rms_norm(x, g_pre[l]) * (1.0 + scl[:, None, :]) + shift[:, None, :]

        proj = h @ w_in[l]
        xa, za, q, k, v, zb, ga, gb = jnp.split(proj, split_pts, axis=-1)

        xa = centred_dwconv(xa, conv_w[l], conv_b[l])
        h_fwd = rg_lru(xa, w_r_f[l], b_r_f[l], w_i_f[l], b_i_f[l], lam_f[l], reverse=False)
        h_bwd = rg_lru(xa, w_r_b[l], b_r_b[l], w_i_b[l], b_i_b[l], lam_b[l], reverse=True)
        y_a = (h_fwd + h_bwd).astype(x.dtype) * jax.nn.silu(za)
        y_a = y_a @ w_pa[l]

        qh = q.reshape(B, S, N_HEADS, HEAD_DIM)
        kh = k.reshape(B, S, N_HEADS, HEAD_DIM)
        vh = v.reshape(B, S, N_HEADS, HEAD_DIM)
        y_b = neighbourhood_attention(qh, kh, vh, rpb[l]) * jax.nn.silu(zb)
        y_b = y_b @ w_pb[l]

        merged = jax.nn.sigmoid(ga) * y_a + jax.nn.sigmoid(gb) * y_b
        x = x + gate[:, None, :] * (merged @ w_o[l])
    return rms_norm(x, g_final)
```

```python
import functools

import jax
import jax.numpy as jnp
from jax import lax
from jax.experimental import pallas as pl
from jax.experimental.pallas import tpu as pltpu

F32 = jnp.float32
BF16 = jnp.bfloat16

SUBLANES = 8
LANES = 128
MXU_TILE = 256

GRID_W = 64
A_BLOCKS = 16
CONV_W = 4
RG_C = 8.0
N_HEADS = 16
HEAD_DIM = 64
WIN_R = 8
WIN_C = 16
EPS = 1e-6
MASK_BIAS = -1e30

PROJ_TM = 1024
NORM_RB = 64
SCAN_TC = 128
TAIL_TM = 256
MOD_ROWS = 16

VMEM_LIMIT = 56 * 1024 * 1024


def _sigmoid(x):
    return 0.5 * jnp.tanh(0.5 * x) + 0.5


def _silu(x):
    return x * _sigmoid(x)


def _mod_kernel(c_ref, w_ref, b_ref, o_ref):
    sc = _silu(c_ref[...]).astype(BF16)
    w = w_ref[...].astype(BF16)
    o_ref[...] = jnp.dot(sc, w, preferred_element_type=F32) + b_ref[...]


def _modulation(c_pad, w_c, b_c):
    d = w_c.shape[0]
    return pl.pallas_call(
        _mod_kernel,
        grid=(3,),
        in_specs=[
            pl.BlockSpec((MOD_ROWS, d), lambda j: (0, 0)),
            pl.BlockSpec((d, d), lambda j: (0, j)),
            pl.BlockSpec((1, d), lambda j: (0, j)),
        ],
        out_specs=pl.BlockSpec((MOD_ROWS, d), lambda j: (0, j)),
        out_shape=jax.ShapeDtypeStruct((MOD_ROWS, 3 * d), F32),
        compiler_params=pltpu.CompilerParams(
            dimension_semantics=("parallel",), vmem_limit_bytes=VMEM_LIMIT),
        name="mod",
    )(c_pad, w_c, b_c)


def _proj_kernel(x_ref, mod_ref, g_ref, w_ref, o_ref, h_ref):
    j = pl.program_id(1)
    d = x_ref.shape[1]

    @pl.when(j == 0)
    def _():
        shift = mod_ref[:, 0:d]
        scl1 = 1.0 + mod_ref[:, d:2 * d]
        g = g_ref[...]

        def body(k, carry):
            r0 = pl.multiple_of(k * NORM_RB, NORM_RB)
            x = x_ref[pl.ds(r0, NORM_RB), :]
            ms = jnp.mean(x * x, axis=-1, keepdims=True)
            y = x * lax.rsqrt(ms + EPS) * g
            h_ref[pl.ds(r0, NORM_RB), :] = (y * scl1 + shift).astype(BF16)
            return carry

        lax.fori_loop(0, x_ref.shape[0] // NORM_RB, body, 0)

    o_ref[...] = jnp.dot(h_ref[...], w_ref[j], preferred_element_type=F32).astype(BF16)


def _projection(x2, mod3, g_pre, w_in_g, seq):
    n, d = x2.shape
    groups = w_in_g.shape[0]
    return pl.pallas_call(
        _proj_kernel,
        grid=(n // PROJ_TM, groups),
        in_specs=[
            pl.BlockSpec((PROJ_TM, d), lambda i, j: (i, 0)),
            pl.BlockSpec((None, 1, 3 * d), lambda i, j: ((i * PROJ_TM) // seq, 0, 0)),
            pl.BlockSpec((1, d), lambda i, j: (0, 0)),
            pl.BlockSpec((groups, d, d), lambda i, j: (0, 0, 0), pipeline_mode=pl.Buffered(1)),
        ],
        out_specs=pl.BlockSpec((PROJ_TM, d), lambda i, j: (i, j)),
        out_shape=jax.ShapeDtypeStruct((n, groups * d), BF16),
        scratch_shapes=[pltpu.VMEM((PROJ_TM, d), BF16)],
        compiler_params=pltpu.CompilerParams(
            dimension_semantics=("parallel", "arbitrary"), vmem_limit_bytes=VMEM_LIMIT),
        name="proj",
    )(x2, mod3, g_pre, w_in_g)


def _rglru_kernel(xa_ref, prev_ref, next_ref, cw_ref, cb_ref, wg_ref, bg_ref, lam_ref,
                  o_ref, sa_ref, su_ref, carry_ref):
    dirn = pl.program_id(0)
    i = pl.program_id(1)
    n_chunks = pl.num_programs(1)
    nb, tc, d = xa_ref.shape
    groups = tc // SUBLANES
    n_tiles = d // MXU_TILE
    rev = dirn == 1
    ci = jnp.where(rev, n_chunks - 1 - i, i)

    @pl.when(i == 0)
    def _():
        carry_ref[...] = jnp.zeros_like(carry_ref)

    has_prev = (ci > 0).astype(F32)
    has_next = (ci < n_chunks - 1).astype(F32)
    cw = cw_ref[...]
    cb = cb_ref[...]
    lam = lam_ref[...]
    c8 = -RG_C * (jnp.maximum(-lam, 0.0) + jnp.log1p(jnp.exp(-jnp.abs(lam))))
    b_r = bg_ref[0]
    b_i = bg_ref[1]

    def gates(b, carry):
        x = xa_ref[b].astype(F32)
        pv = prev_ref[b].astype(F32) * has_prev
        nx = next_ref[b].astype(F32) * has_next
        ext = jnp.concatenate([pv, x, nx], axis=0)
        base = SUBLANES - CONV_W // 2
        xc = cb + cw[0:1] * ext[base:base + tc]
        for k in range(1, CONV_W):
            xc = xc + cw[k:k + 1] * ext[base + k:base + k + tc]
        xcb = xc.astype(BF16)
        for k in range(n_tiles):
            sl = slice(k * MXU_TILE, (k + 1) * MXU_TILE)
            xk = xcb[:, sl]
            pr = jnp.dot(xk, wg_ref[0, k], preferred_element_type=F32) + b_r[:, sl]
            pi = jnp.dot(xk, wg_ref[1, k], preferred_element_type=F32) + b_i[:, sl]
            a = jnp.exp(c8[:, sl] * _sigmoid(pr))
            u = jnp.sqrt(1.0 - a * a) * (_sigmoid(pi) * xc[:, sl])
            for c2 in range(MXU_TILE // LANES):
                cblk = k * (MXU_TILE // LANES) + c2
                rows = slice(cblk * SUBLANES, (cblk + 1) * SUBLANES)
                lanes = slice(c2 * LANES, (c2 + 1) * LANES)
                sa_ref[b, :, rows, :] = a[:, lanes].reshape(groups, SUBLANES, LANES)
                su_ref[b, :, rows, :] = u[:, lanes].reshape(groups, SUBLANES, LANES)
        return carry

    lax.fori_loop(0, nb, gates, 0)

    def scan(it, hs):
        g = jnp.where(rev, groups - 1 - it, it)
        hs = list(hs)
        for jj in range(SUBLANES):
            j = jnp.where(rev, SUBLANES - 1 - jj, jj)
            tok = pl.ds(j, SUBLANES, stride=SUBLANES)
            for b in range(nb):
                hb = sa_ref[b, g, tok, :] * hs[b] + su_ref[b, g, tok, :]
                su_ref[b, g, tok, :] = hb
                hs[b] = hb
        r0 = pl.multiple_of(g * SUBLANES, SUBLANES)
        for b in range(nb):
            for cblk in range(d // LANES):
                o_ref[b, pl.ds(r0, SUBLANES), cblk * LANES:(cblk + 1) * LANES] = (
                    su_ref[b, g, cblk * SUBLANES:(cblk + 1) * SUBLANES, :])
        return tuple(hs)

    hs = lax.fori_loop(0, groups, scan, tuple(carry_ref[b] for b in range(nb)))
    for b in range(nb):
        carry_ref[b] = hs[b]


def _rglru(proj3, conv_w, conv_b, wg, bg, lam):
    nb, seq, _ = proj3.shape
    d = conv_w.shape[1]
    n_chunks = seq // SCAN_TC
    n_groups = seq // SUBLANES
    gpc = SCAN_TC // SUBLANES
    proj4 = proj3.reshape(nb, n_groups, SUBLANES, proj3.shape[2])

    def chunk(dd, i):
        return jnp.where(dd == 1, n_chunks - 1 - i, i)

    n_tiles = d // MXU_TILE
    return pl.pallas_call(
        _rglru_kernel,
        grid=(2, n_chunks),
        in_specs=[
            pl.BlockSpec((nb, SCAN_TC, d), lambda dd, i: (0, chunk(dd, i), 0)),
            pl.BlockSpec((nb, None, SUBLANES, d),
                         lambda dd, i: (0, jnp.maximum(chunk(dd, i) * gpc - 1, 0), 0, 0)),
            pl.BlockSpec((nb, None, SUBLANES, d),
                         lambda dd, i: (0, jnp.minimum((chunk(dd, i) + 1) * gpc, n_groups - 1), 0, 0)),
            pl.BlockSpec((CONV_W, d), lambda dd, i: (0, 0)),
            pl.BlockSpec((1, d), lambda dd, i: (0, 0)),
            pl.BlockSpec((None, 2, n_tiles, MXU_TILE, MXU_TILE), lambda dd, i: (dd, 0, 0, 0, 0)),
            pl.BlockSpec((None, 2, 1, d), lambda dd, i: (dd, 0, 0, 0)),
            pl.BlockSpec((None, 1, d), lambda dd, i: (dd, 0, 0)),
        ],
        out_specs=pl.BlockSpec((None, nb, SCAN_TC, d), lambda dd, i: (dd, 0, chunk(dd, i), 0)),
        out_shape=jax.ShapeDtypeStruct((2, nb, seq, d), F32),
        scratch_shapes=[
            pltpu.VMEM((nb, gpc, d // LANES * SUBLANES, LANES), F32),
            pltpu.VMEM((nb, gpc, d // LANES * SUBLANES, LANES), F32),
            pltpu.VMEM((nb, SUBLANES, LANES), F32),
        ],
        compiler_params=pltpu.CompilerParams(
            dimension_semantics=("arbitrary", "arbitrary"), vmem_limit_bytes=VMEM_LIMIT),
        name="rglru",
    )(proj3, proj4, proj4, conv_w, conv_b, wg, bg, lam)


def _natten_kernel(q_ref, k_ref, v_ref, bias_ref, o_ref):
    r = pl.program_id(1)
    rows = pl.num_programs(1)
    gw, d = q_ref.shape
    win_r = min(WIN_R, rows)
    r0 = jnp.clip(r - win_r // 2, 0, rows - win_r)
    k0 = pl.multiple_of(r0 * gw, gw)
    kwin = k_ref[pl.ds(k0, win_r * gw), :]
    vwin = v_ref[pl.ds(k0, win_r * gw), :]
    q = q_ref[...] * (HEAD_DIM ** -0.5)
    lane = lax.broadcasted_iota(jnp.int32, (gw, LANES), 1)
    lo = lane < HEAD_DIM
    zero = jnp.zeros((gw, LANES), BF16)
    for p in range(d // LANES):
        sl = slice(p * LANES, (p + 1) * LANES)
        qp = q[:, sl]
        qs = jnp.concatenate([jnp.where(lo, qp, zero), jnp.where(lo, zero, qp)], axis=0)
        s = lax.dot_general(qs, kwin[:, sl], (((1,), (1,)), ((), ())),
                            preferred_element_type=F32)
        s = s + bias_ref[p]
        m = jnp.max(s, axis=-1, keepdims=True)
        e = jnp.exp(s - m)
        l = jnp.sum(e, axis=-1, keepdims=True)
        pv = jnp.dot(e.astype(BF16), vwin[:, sl], preferred_element_type=F32)
        o = pv / l
        o_ref[:, sl] = jnp.where(lo, o[:gw], o[gw:]).astype(BF16)


def _natten(proj3, bias_tab, q_col, k_col, v_col):
    nb, seq, _ = proj3.shape
    d = N_HEADS * HEAD_DIM
    rows = seq // GRID_W
    win_r = min(WIN_R, rows)
    n_pairs = d // LANES

    def pattern(r):
        r0 = jnp.clip(r - win_r // 2, 0, rows - win_r)
        return r0 - r + (WIN_R - 1)

    return pl.pallas_call(
        _natten_kernel,
        grid=(nb, rows),
        in_specs=[
            pl.BlockSpec((None, GRID_W, d), lambda b, r: (b, r, q_col)),
            pl.BlockSpec((None, seq, d), lambda b, r: (b, 0, k_col)),
            pl.BlockSpec((None, seq, d), lambda b, r: (b, 0, v_col)),
            pl.BlockSpec((None, n_pairs, 2 * GRID_W, win_r * GRID_W),
                         lambda b, r: (pattern(r), 0, 0, 0)),
        ],
        out_specs=pl.BlockSpec((None, GRID_W, d), lambda b, r: (b, r, 0)),
        out_shape=jax.ShapeDtypeStruct((nb, seq, d), BF16),
        compiler_params=pltpu.CompilerParams(
            dimension_semantics=("parallel", "arbitrary"), vmem_limit_bytes=VMEM_LIMIT),
        name="natten",
    )(proj3, proj3, proj3, bias_tab)


def _bias_table(rpb, rows):
    win_r = min(WIN_R, rows)
    n_pat = WIN_R
    cols = jnp.arange(GRID_W)
    c0 = jnp.clip(cols - WIN_C // 2, 0, GRID_W - WIN_C)
    valid = (cols[None, :] >= c0[:, None]) & (cols[None, :] < c0[:, None] + WIN_C)
    dcol = jnp.clip(cols[None, :] - cols[:, None] + (WIN_C - 1), 0, 2 * WIN_C - 2)
    t1 = jnp.where(valid[None, None], rpb[:, :, dcol], MASK_BIAS)
    pats = []
    for d0 in range(n_pat):
        t = t1[:, d0:d0 + win_r]
        t = jnp.transpose(t, (0, 2, 1, 3)).reshape(N_HEADS, GRID_W, win_r * GRID_W)
        pats.append(t.reshape(N_HEADS // 2, 2 * GRID_W, win_r * GRID_W))
    return jnp.stack(pats).astype(F32)


def _tail_kernel(final, x_ref, mod_ref, hf_ref, hb_ref, za_ref, zb_ref, ga_ref, gb_ref, att_ref,
                 wpa_ref, wpb_ref, wo_ref, gf_ref, o_ref):
    d = x_ref.shape[1]
    ya = (hf_ref[...] + hb_ref[...]) * _silu(za_ref[...].astype(F32))
    ya = jnp.dot(ya.astype(BF16), wpa_ref[...], preferred_element_type=F32)
    yb = att_ref[...].astype(F32) * _silu(zb_ref[...].astype(F32))
    yb = jnp.dot(yb.astype(BF16), wpb_ref[...], preferred_element_type=F32)
    merged = _sigmoid(ga_ref[...].astype(F32)) * ya + _sigmoid(gb_ref[...].astype(F32)) * yb
    res = jnp.dot(merged.astype(BF16), wo_ref[...], preferred_element_type=F32)
    xo = x_ref[...] + mod_ref[:, 2 * d:3 * d] * res
    if final:
        ms = jnp.mean(xo * xo, axis=-1, keepdims=True)
        xo = xo * lax.rsqrt(ms + EPS) * gf_ref[...]
    o_ref[...] = xo


def _tail(x2, mod3, h2, proj, att2, w_pa, w_pb, w_o, g_final, seq, cols, final):
    n, d = x2.shape
    za_col, zb_col, ga_col, gb_col = cols
    tok = lambda i: (i, 0)
    wspec = pl.BlockSpec((d, d), lambda i: (0, 0), pipeline_mode=pl.Buffered(1))
    return pl.pallas_call(
        functools.partial(_tail_kernel, final),
        grid=(n // TAIL_TM,),
        in_specs=[
            pl.BlockSpec((TAIL_TM, d), tok),
            pl.BlockSpec((None, 1, 3 * d), lambda i: ((i * TAIL_TM) // seq, 0, 0)),
            pl.BlockSpec((None, TAIL_TM, d), lambda i: (0, i, 0)),
            pl.BlockSpec((None, TAIL_TM, d), lambda i: (1, i, 0)),
            pl.BlockSpec((TAIL_TM, d), lambda i: (i, za_col)),
            pl.BlockSpec((TAIL_TM, d), lambda i: (i, zb_col)),
            pl.BlockSpec((TAIL_TM, d), lambda i: (i, ga_col)),
            pl.BlockSpec((TAIL_TM, d), lambda i: (i, gb_col)),
            pl.BlockSpec((TAIL_TM, d), tok),
            wspec, wspec, wspec,
            pl.BlockSpec((1, d), lambda i: (0, 0)),
        ],
        out_specs=pl.BlockSpec((TAIL_TM, d), tok),
        out_shape=jax.ShapeDtypeStruct((n, d), F32),
        compiler_params=pltpu.CompilerParams(
            dimension_semantics=("parallel",), vmem_limit_bytes=VMEM_LIMIT),
        name="tail",
    )(x2, mod3, h2, h2, proj, proj, proj, proj, att2, w_pa, w_pb, w_o, g_final)


def _gate_tiles(w):
    nblk, bw, _ = w.shape
    per = MXU_TILE // bw
    w4 = w.reshape(nblk // per, per, bw, bw)
    eye = jnp.eye(per, dtype=w.dtype)
    t = jnp.einsum("kmij,mn->kminj", w4, eye)
    return t.reshape(nblk // per, MXU_TILE, MXU_TILE)


def kernel(x, c, g_pre, w_c, b_c, w_in, conv_w, conv_b, w_r_f, b_r_f, w_i_f, b_i_f, lam_f,
           w_r_b, b_r_b, w_i_b, b_i_b, lam_b, rpb, w_pa, w_pb, w_o, g_final):
    nb, seq, d = x.shape
    depth = g_pre.shape[0]
    n = nb * seq
    rows = seq // GRID_W
    groups = w_in.shape[2] // d
    xa_col, za_col, q_col, k_col, v_col, zb_col, ga_col, gb_col = range(groups)
    assert xa_col == 0

    c_pad = jnp.zeros((MOD_ROWS, d), F32).at[:nb].set(c)
    x2 = x.reshape(n, d)
    for l in range(depth):
        mod = _modulation(c_pad, w_c[l], b_c[l].reshape(1, 3 * d))
        mod3 = mod[:nb].reshape(nb, 1, 3 * d)

        w_in_g = jnp.transpose(w_in[l].astype(BF16).reshape(d, groups, d), (1, 0, 2))
        proj = _projection(x2, mod3, g_pre[l].reshape(1, d), w_in_g, seq)
        proj3 = proj.reshape(nb, seq, groups * d)

        wg = jnp.stack([
            jnp.stack([_gate_tiles(w_r_f[l]), _gate_tiles(w_i_f[l])]),
            jnp.stack([_gate_tiles(w_r_b[l]), _gate_tiles(w_i_b[l])]),
        ]).astype(BF16)
        bg = jnp.stack([
            jnp.stack([b_r_f[l], b_i_f[l]]),
            jnp.stack([b_r_b[l], b_i_b[l]]),
        ]).reshape(2, 2, 1, d)
        lam = jnp.stack([lam_f[l], lam_b[l]]).reshape(2, 1, d)
        h2 = _rglru(proj3, conv_w[l], conv_b[l].reshape(1, d), wg, bg, lam)
        h2 = h2.reshape(2, n, d)

        att = _natten(proj3, _bias_table(rpb[l], rows), q_col, k_col, v_col)

        x2 = _tail(x2, mod3, h2, proj, att.reshape(n, d),
                   w_pa[l].astype(BF16), w_pb[l].astype(BF16), w_o[l].astype(BF16),
                   g_final.reshape(1, d), seq, (za_col, zb_col, ga_col, gb_col),
                   final=(l == depth - 1))
    return x2.reshape(nb, seq, d)
```

```python
import functools

import jax
import jax.numpy as jnp
import numpy as np
from jax import lax
from jax.experimental import pallas as pl
from jax.experimental.pallas import tpu as pltpu

F32 = jnp.float32
BF16 = jnp.bfloat16

SUBLANES = 8
LANES = 128
MXU_TILE = 256

GRID_W = 64
A_BLOCKS = 16
CONV_W = 4
RG_C = 8.0
N_HEADS = 16
HEAD_DIM = 64
WIN_R = 8
WIN_C = 16
EPS = 1e-6
MASK_BIAS = -1e30
LOG2_E = 1.4426950408889634
RSQRT_FLOOR = 1e-30

PROJ_TM = 512
NORM_RB = 64
SCAN_TC = 128
TAIL_TM = 256
MOD_ROWS = 16

VMEM_LIMIT = 56 * 1024 * 1024


def _sigmoid(x):
    return 0.5 * jnp.tanh(0.5 * x) + 0.5


def _silu(x):
    return x * _sigmoid(x)


def _mod_kernel(c_ref, w_ref, b_ref, o_ref):
    sc = _silu(c_ref[...]).astype(BF16)
    w = w_ref[...].astype(BF16)
    o_ref[...] = jnp.dot(sc, w, preferred_element_type=F32) + b_ref[...]


def _modulation(c_pad, w_c, b_c):
    d = w_c.shape[0]
    return pl.pallas_call(
        _mod_kernel,
        grid=(3,),
        in_specs=[
            pl.BlockSpec((MOD_ROWS, d), lambda j: (0, 0)),
            pl.BlockSpec((d, d), lambda j: (0, j)),
            pl.BlockSpec((1, d), lambda j: (0, j)),
        ],
        out_specs=pl.BlockSpec((MOD_ROWS, d), lambda j: (0, j)),
        out_shape=jax.ShapeDtypeStruct((MOD_ROWS, 3 * d), F32),
        compiler_params=pltpu.CompilerParams(
            dimension_semantics=("parallel",), vmem_limit_bytes=VMEM_LIMIT),
        name="mod",
    )(c_pad, w_c, b_c)


def _norm_rows(x_ref, mod_ref, g, h_ref, slot, r0):
    d = x_ref.shape[1]
    x = x_ref[pl.ds(r0, NORM_RB), :]
    ms = jnp.mean(x * x, axis=-1, keepdims=True)
    y = x * lax.rsqrt(ms + EPS) * g
    h = y * (1.0 + mod_ref[:, d:2 * d]) + mod_ref[:, 0:d]
    h_ref[slot, pl.ds(r0, NORM_RB), :] = h.astype(BF16)


def _proj_kernel(x0_ref, mod0_ref, xn_ref, modn_ref, g_ref, w_ref, o_ref, h_ref):
    i = pl.program_id(0)
    tm, d = xn_ref.shape
    g = g_ref[...]

    @pl.when(i == 0)
    def _():
        def body(k, carry):
            _norm_rows(x0_ref, mod0_ref, g, h_ref, 0, pl.multiple_of(k * NORM_RB, NORM_RB))
            return carry
        lax.fori_loop(0, tm // NORM_RB, body, 0)

    cur = i % 2
    for k in range(tm // NORM_RB):
        _norm_rows(xn_ref, modn_ref, g, h_ref, 1 - cur, k * NORM_RB)
    for j in range(w_ref.shape[1] // d):
        cols = slice(j * d, (j + 1) * d)
        o_ref[:, cols] = jnp.dot(h_ref[cur], w_ref[:, cols],
                                 preferred_element_type=F32).astype(BF16)


def _projection(x2, mod3, g_pre, w_in, seq):
    n, d = x2.shape
    n_tiles = n // PROJ_TM
    nxt = lambda i: jnp.minimum(i + 1, n_tiles - 1)
    return pl.pallas_call(
        _proj_kernel,
        grid=(n_tiles,),
        in_specs=[
            pl.BlockSpec((PROJ_TM, d), lambda i: (0, 0), pipeline_mode=pl.Buffered(1)),
            pl.BlockSpec((None, 1, 3 * d), lambda i: (0, 0, 0)),
            pl.BlockSpec((PROJ_TM, d), lambda i: (nxt(i), 0)),
            pl.BlockSpec((None, 1, 3 * d), lambda i: ((nxt(i) * PROJ_TM) // seq, 0, 0)),
            pl.BlockSpec((1, d), lambda i: (0, 0)),
            pl.BlockSpec(w_in.shape, lambda i: (0, 0), pipeline_mode=pl.Buffered(1)),
        ],
        out_specs=pl.BlockSpec((PROJ_TM, w_in.shape[1]), lambda i: (i, 0)),
        out_shape=jax.ShapeDtypeStruct((n, w_in.shape[1]), BF16),
        scratch_shapes=[pltpu.VMEM((2, PROJ_TM, d), BF16)],
        compiler_params=pltpu.CompilerParams(
            dimension_semantics=("arbitrary",), vmem_limit_bytes=VMEM_LIMIT),
        name="proj",
    )(x2, mod3, x2, mod3, g_pre, w_in)


def _rglru_kernel(xa_ref, prev_ref, next_ref, shift_ref, cw_ref, cb_ref, wg_ref, bg_ref, lam_ref,
                  o_ref, sa_ref, su_ref, carry_ref):
    dirn = pl.program_id(0)
    i = pl.program_id(1)
    n_chunks = pl.num_programs(1)
    nb, tc, d = xa_ref.shape
    groups = tc // SUBLANES
    n_tiles = d // MXU_TILE
    rev = dirn == 1
    ci = jnp.where(rev, n_chunks - 1 - i, i)

    @pl.when(i == 0)
    def _():
        carry_ref[...] = jnp.zeros_like(carry_ref)

    has_prev = ci > 0
    has_next = ci < n_chunks - 1
    cwh = 0.5 * cw_ref[...]
    cbh = 0.5 * cb_ref[...]
    lam = lam_ref[...]
    softplus_neg_lam = jnp.maximum(-lam, 0.0) + jnp.log1p(jnp.exp(-jnp.abs(lam)))
    c8l = (-0.5 * RG_C * LOG2_E) * softplus_neg_lam
    brh = 0.5 * bg_ref[0]
    bih = 0.5 * bg_ref[1]
    shift_m = shift_ref[...]
    taps = [k for k in range(CONV_W) if k != CONV_W // 2]

    def gates(b, carry):
        xb = xa_ref[b]
        halo = jnp.concatenate([
            jnp.where(has_prev, prev_ref[b], jnp.zeros_like(prev_ref[b])),
            jnp.where(has_next, next_ref[b], jnp.zeros_like(next_ref[b]))], axis=0)
        sh = jnp.dot(shift_m, jnp.concatenate([xb, halo], axis=0), preferred_element_type=F32)
        xh = cbh + cwh[CONV_W // 2:CONV_W // 2 + 1] * xb.astype(F32)
        for n, k in enumerate(taps):
            xh = xh + cwh[k:k + 1] * sh[n * tc:(n + 1) * tc]
        xhb = xh.astype(BF16)
        for k in range(n_tiles):
            sl = slice(k * MXU_TILE, (k + 1) * MXU_TILE)
            xk = xhb[:, sl]
            tr = jnp.tanh(jnp.dot(xk, wg_ref[0, k], preferred_element_type=F32) + brh[:, sl])
            ti = jnp.tanh(jnp.dot(xk, wg_ref[1, k], preferred_element_type=F32) + bih[:, sl])
            a = jnp.exp2(c8l[:, sl] * (tr + 1.0))
            z = 1.0 - a * a
            u = (z * lax.rsqrt(jnp.maximum(z, RSQRT_FLOOR))) * ((ti + 1.0) * xh[:, sl])
            for c2 in range(MXU_TILE // LANES):
                cblk = k * (MXU_TILE // LANES) + c2
                rows = slice(cblk * SUBLANES, (cblk + 1) * SUBLANES)
                lanes = slice(c2 * LANES, (c2 + 1) * LANES)
                sa_ref[b, :, rows, :] = a[:, lanes].reshape(groups, SUBLANES, LANES)
                su_ref[b, :, rows, :] = u[:, lanes].reshape(groups, SUBLANES, LANES)
        return carry

    lax.fori_loop(0, nb, gates, 0)

    def scan(it, hs):
        g = jnp.where(rev, groups - 1 - it, it)
        hs = list(hs)
        for jj in range(SUBLANES):
            j = jnp.where(rev, SUBLANES - 1 - jj, jj)
            tok = pl.ds(j, SUBLANES, stride=SUBLANES)
            for b in range(nb):
                hb = sa_ref[b, g, tok, :] * hs[b] + su_ref[b, g, tok, :]
                su_ref[b, g, tok, :] = hb
                hs[b] = hb
        r0 = pl.multiple_of(g * SUBLANES, SUBLANES)
        for b in range(nb):
            for cblk in range(d // LANES):
                o_ref[b, pl.ds(r0, SUBLANES), cblk * LANES:(cblk + 1) * LANES] = (
                    su_ref[b, g, cblk * SUBLANES:(cblk + 1) * SUBLANES, :])
        return tuple(hs)

    hs = lax.fori_loop(0, groups, scan, tuple(carry_ref[b] for b in range(nb)))
    for b in range(nb):
        carry_ref[b] = hs[b]


def _shift_matrix(tc):
    taps = [k for k in range(CONV_W) if k != CONV_W // 2]
    m = np.zeros((len(taps) * tc, tc + 2 * SUBLANES), np.float32)
    for n, k in enumerate(taps):
        for t in range(tc):
            s = t - CONV_W // 2 + k
            if s < 0:
                col = tc + SUBLANES + s
            elif s >= tc:
                col = tc + SUBLANES + (s - tc)
            else:
                col = s
            m[n * tc + t, col] = 1.0
    return m


def _rglru(proj3, conv_w, conv_b, wg, bg, lam):
    nb, seq, _ = proj3.shape
    d = conv_w.shape[1]
    n_chunks = seq // SCAN_TC
    n_groups = seq // SUBLANES
    gpc = SCAN_TC // SUBLANES
    proj4 = proj3.reshape(nb, n_groups, SUBLANES, proj3.shape[2])

    def chunk(dd, i):
        return jnp.where(dd == 1, n_chunks - 1 - i, i)

    n_tiles = d // MXU_TILE
    shift_m = jnp.asarray(_shift_matrix(SCAN_TC), BF16)
    return pl.pallas_call(
        _rglru_kernel,
        grid=(2, n_chunks),
        in_specs=[
            pl.BlockSpec((nb, SCAN_TC, d), lambda dd, i: (0, chunk(dd, i), 0)),
            pl.BlockSpec((nb, None, SUBLANES, d),
                         lambda dd, i: (0, jnp.maximum(chunk(dd, i) * gpc - 1, 0), 0, 0)),
            pl.BlockSpec((nb, None, SUBLANES, d),
                         lambda dd, i: (0, jnp.minimum((chunk(dd, i) + 1) * gpc, n_groups - 1), 0, 0)),
            pl.BlockSpec(shift_m.shape, lambda dd, i: (0, 0)),
            pl.BlockSpec((CONV_W, d), lambda dd, i: (0, 0)),
            pl.BlockSpec((1, d), lambda dd, i: (0, 0)),
            pl.BlockSpec((None, 2, n_tiles, MXU_TILE, MXU_TILE), lambda dd, i: (dd, 0, 0, 0, 0)),
            pl.BlockSpec((None, 2, 1, d), lambda dd, i: (dd, 0, 0, 0)),
            pl.BlockSpec((None, 1, d), lambda dd, i: (dd, 0, 0)),
        ],
        out_specs=pl.BlockSpec((None, nb, SCAN_TC, d), lambda dd, i: (dd, 0, chunk(dd, i), 0)),
        out_shape=jax.ShapeDtypeStruct((2, nb, seq, d), F32),
        scratch_shapes=[
            pltpu.VMEM((nb, gpc, d // LANES * SUBLANES, LANES), F32),
            pltpu.VMEM((nb, gpc, d // LANES * SUBLANES, LANES), F32),
            pltpu.VMEM((nb, SUBLANES, LANES), F32),
        ],
        compiler_params=pltpu.CompilerParams(
            dimension_semantics=("arbitrary", "arbitrary"), vmem_limit_bytes=VMEM_LIMIT),
        name="rglru",
    )(proj3, proj4, proj4, shift_m, conv_w, conv_b, wg, bg, lam)


def _natten_kernel(q_ref, k_ref, v_ref, bias_ref, o_ref):
    r = pl.program_id(1)
    rows = pl.num_programs(1)
    gw, d = q_ref.shape
    win_r = min(WIN_R, rows)
    r0 = jnp.clip(r - win_r // 2, 0, rows - win_r)
    k0 = pl.multiple_of(r0 * gw, gw)
    kwin = k_ref[pl.ds(k0, win_r * gw), :]
    vwin = v_ref[pl.ds(k0, win_r * gw), :]
    q = q_ref[...] * (HEAD_DIM ** -0.5)
    lane = lax.broadcasted_iota(jnp.int32, (gw, LANES), 1)
    lo = lane < HEAD_DIM
    zero = jnp.zeros((gw, LANES), BF16)
    n_pairs = d // LANES
    scores = []
    for p in range(n_pairs):
        sl = slice(p * LANES, (p + 1) * LANES)
        qp = q[:, sl]
        qs = jnp.concatenate([jnp.where(lo, qp, zero), jnp.where(lo, zero, qp)], axis=0)
        scores.append(lax.dot_general(qs, kwin[:, sl], (((1,), (1,)), ((), ())),
                                      preferred_element_type=F32))
    probs = []
    for p in range(n_pairs):
        s = scores[p] + bias_ref[p]
        m = jnp.max(s, axis=-1, keepdims=True)
        e = jnp.exp(s - m)
        probs.append((e.astype(BF16), jnp.sum(e, axis=-1, keepdims=True)))
    for p in range(n_pairs):
        sl = slice(p * LANES, (p + 1) * LANES)
        e, l = probs[p]
        o = jnp.dot(e, vwin[:, sl], preferred_element_type=F32) / l
        o_ref[:, sl] = jnp.where(lo, o[:gw], o[gw:]).astype(BF16)


def _natten(proj3, bias_tab, q_col, k_col, v_col):
    nb, seq, _ = proj3.shape
    d = N_HEADS * HEAD_DIM
    rows = seq // GRID_W
    win_r = min(WIN_R, rows)
    n_pairs = d // LANES

    def pattern(r):
        r0 = jnp.clip(r - win_r // 2, 0, rows - win_r)
        return r0 - r + (WIN_R - 1)

    return pl.pallas_call(
        _natten_kernel,
        grid=(nb, rows),
        in_specs=[
            pl.BlockSpec((None, GRID_W, d), lambda b, r: (b, r, q_col)),
            pl.BlockSpec((None, seq, d), lambda b, r: (b, 0, k_col)),
            pl.BlockSpec((None, seq, d), lambda b, r: (b, 0, v_col)),
            pl.BlockSpec((None, n_pairs, 2 * GRID_W, win_r * GRID_W),
                         lambda b, r: (pattern(r), 0, 0, 0)),
        ],
        out_specs=pl.BlockSpec((None, GRID_W, d), lambda b, r: (b, r, 0)),
        out_shape=jax.ShapeDtypeStruct((nb, seq, d), BF16),
        compiler_params=pltpu.CompilerParams(
            dimension_semantics=("parallel", "arbitrary"), vmem_limit_bytes=VMEM_LIMIT),
        name="natten",
    )(proj3, proj3, proj3, bias_tab)


def _bias_table(rpb, rows):
    win_r = min(WIN_R, rows)
    n_pat = WIN_R
    cols = np.arange(GRID_W)
    c0 = np.clip(cols - WIN_C // 2, 0, GRID_W - WIN_C)
    valid = (cols[None, :] >= c0[:, None]) & (cols[None, :] < c0[:, None] + WIN_C)
    period = 2 * GRID_W
    nh, nd, _ = rpb.shape
    rp = jnp.concatenate([rpb[:, :, WIN_C - 1:],
                          jnp.zeros((nh, nd, period - (2 * WIN_C - 1)), rpb.dtype),
                          rpb[:, :, :WIN_C - 1]], axis=-1)
    toe = jnp.tile(rp, (1, 1, GRID_W))[:, :, :GRID_W * (period - 1)]
    toe = toe.reshape(nh, nd, GRID_W, period - 1)[:, :, :, :GRID_W]
    t1 = jnp.where(jnp.asarray(valid)[None, None], toe, MASK_BIAS)
    pats = []
    for d0 in range(n_pat):
        t = t1[:, d0:d0 + win_r]
        t = jnp.transpose(t, (0, 2, 1, 3)).reshape(N_HEADS, GRID_W, win_r * GRID_W)
        pats.append(t.reshape(N_HEADS // 2, 2 * GRID_W, win_r * GRID_W))
    return jnp.stack(pats).astype(F32)


def _tail_kernel(final, x_ref, mod_ref, hf_ref, hb_ref, za_ref, zb_ref, ga_ref, gb_ref, att_ref,
                 wpa_ref, wpb_ref, wo_ref, gf_ref, o_ref):
    d = x_ref.shape[1]
    ya = (hf_ref[...] + hb_ref[...]) * _silu(za_ref[...].astype(F32))
    ya = jnp.dot(ya.astype(BF16), wpa_ref[...], preferred_element_type=F32)
    yb = att_ref[...].astype(F32) * _silu(zb_ref[...].astype(F32))
    yb = jnp.dot(yb.astype(BF16), wpb_ref[...], preferred_element_type=F32)
    merged = _sigmoid(ga_ref[...].astype(F32)) * ya + _sigmoid(gb_ref[...].astype(F32)) * yb
    res = jnp.dot(merged.astype(BF16), wo_ref[...], preferred_element_type=F32)
    xo = x_ref[...] + mod_ref[:, 2 * d:3 * d] * res
    if final:
        ms = jnp.mean(xo * xo, axis=-1, keepdims=True)
        xo = xo * lax.rsqrt(ms + EPS) * gf_ref[...]
    o_ref[...] = xo


def _tail(x2, mod3, h2, proj, att2, w_pa, w_pb, w_o, g_final, seq, cols, final):
    n, d = x2.shape
    za_col, zb_col, ga_col, gb_col = cols
    tok = lambda i: (i, 0)
    wspec = pl.BlockSpec((d, d), lambda i: (0, 0), pipeline_mode=pl.Buffered(1))
    return pl.pallas_call(
        functools.partial(_tail_kernel, final),
        grid=(n // TAIL_TM,),
        in_specs=[
            pl.BlockSpec((TAIL_TM, d), tok),
            pl.BlockSpec((None, 1, 3 * d), lambda i: ((i * TAIL_TM) // seq, 0, 0)),
            pl.BlockSpec((None, TAIL_TM, d), lambda i: (0, i, 0)),
            pl.BlockSpec((None, TAIL_TM, d), lambda i: (1, i, 0)),
            pl.BlockSpec((TAIL_TM, d), lambda i: (i, za_col)),
            pl.BlockSpec((TAIL_TM, d), lambda i: (i, zb_col)),
            pl.BlockSpec((TAIL_TM, d), lambda i: (i, ga_col)),
            pl.BlockSpec((TAIL_TM, d), lambda i: (i, gb_col)),
            pl.BlockSpec((TAIL_TM, d), tok),
            wspec, wspec, wspec,
            pl.BlockSpec((1, d), lambda i: (0, 0)),
        ],
        out_specs=pl.BlockSpec((TAIL_TM, d), tok),
        out_shape=jax.ShapeDtypeStruct((n, d), F32),
        compiler_params=pltpu.CompilerParams(
            dimension_semantics=("parallel",), vmem_limit_bytes=VMEM_LIMIT),
        name="tail",
    )(x2, mod3, h2, h2, proj, proj, proj, proj, att2, w_pa, w_pb, w_o, g_final)


def _gate_tiles(ws):
    ng, nblk, bw, _ = ws.shape
    per = MXU_TILE // bw
    w6 = ws.astype(BF16).reshape(ng, nblk // per, per, bw, 1, bw)
    eye = jnp.eye(per, dtype=BF16).reshape(1, 1, per, 1, per, 1)
    return (w6 * eye).reshape(ng, nblk // per, MXU_TILE, MXU_TILE)


def kernel(x, c, g_pre, w_c, b_c, w_in, conv_w, conv_b, w_r_f, b_r_f, w_i_f, b_i_f, lam_f,
           w_r_b, b_r_b, w_i_b, b_i_b, lam_b, rpb, w_pa, w_pb, w_o, g_final):
    nb, seq, d = x.shape
    depth = g_pre.shape[0]
    n = nb * seq
    rows = seq // GRID_W
    groups = w_in.shape[2] // d
    xa_col, za_col, q_col, k_col, v_col, zb_col, ga_col, gb_col = range(groups)
    assert xa_col == 0

    c_pad = jnp.zeros((MOD_ROWS, d), F32).at[:nb].set(c)
    x2 = x.reshape(n, d)
    for l in range(depth):
        mod = _modulation(c_pad, w_c[l], b_c[l].reshape(1, 3 * d))
        mod3 = mod[:nb].reshape(nb, 1, 3 * d)

        proj = _projection(x2, mod3, g_pre[l].reshape(1, d), w_in[l].astype(BF16), seq)
        proj3 = proj.reshape(nb, seq, groups * d)

        wg = _gate_tiles(jnp.stack([w_r_f[l], w_i_f[l], w_r_b[l], w_i_b[l]]))
        wg = wg.reshape((2, 2) + wg.shape[1:])
        bg = jnp.stack([
            jnp.stack([b_r_f[l], b_i_f[l]]),
            jnp.stack([b_r_b[l], b_i_b[l]]),
        ]).reshape(2, 2, 1, d)
        lam = jnp.stack([lam_f[l], lam_b[l]]).reshape(2, 1, d)
        h2 = _rglru(proj3, conv_w[l], conv_b[l].reshape(1, d), wg, bg, lam)
        h2 = h2.reshape(2, n, d)

        att = _natten(proj3, _bias_table(rpb[l], rows), q_col, k_col, v_col)

        x2 = _tail(x2, mod3, h2, proj, att.reshape(n, d),
                   w_pa[l].astype(BF16), w_pb[l].astype(BF16), w_o[l].astype(BF16),
                   g_final.reshape(1, d), seq, (za_col, zb_col, ga_col, gb_col),
                   final=(l == depth - 1))
    return x2.reshape(nb, seq, d)
```

```python
import functools

import jax
import jax.numpy as jnp
import numpy as np
from jax import lax
from jax.experimental import pallas as pl
from jax.experimental.pallas import tpu as pltpu

F32 = jnp.float32
BF16 = jnp.bfloat16

SUBLANES = 8
LANES = 128
MXU_TILE = 256

GRID_W = 64
A_BLOCKS = 16
CONV_W = 4
RG_C = 8.0
N_HEADS = 16
HEAD_DIM = 64
WIN_R = 8
WIN_C = 16
EPS = 1e-6
MASK_BIAS = -1e30
LOG2_E = 1.4426950408889634
RSQRT_FLOOR = 1e-30

PROJ_TM = 512
NORM_RB = 64
SCAN_TC = 128
TAIL_TM = 256
NAT_ROWS = 2
MOD_ROWS = 16

VMEM_LIMIT = 56 * 1024 * 1024


def _sigmoid(x):
    return 0.5 * jnp.tanh(0.5 * x) + 0.5


def _silu(x):
    return x * _sigmoid(x)


def _mod_kernel(c_ref, w_ref, b_ref, o_ref):
    sc = _silu(c_ref[...]).astype(BF16)
    w = w_ref[...].astype(BF16)
    o_ref[...] = jnp.dot(sc, w, preferred_element_type=F32) + b_ref[...]


def _modulation(c_pad, w_c, b_c):
    d = w_c.shape[0]
    return pl.pallas_call(
        _mod_kernel,
        grid=(3,),
        in_specs=[
            pl.BlockSpec((MOD_ROWS, d), lambda j: (0, 0)),
            pl.BlockSpec((d, d), lambda j: (0, j)),
            pl.BlockSpec((1, d), lambda j: (0, j)),
        ],
        out_specs=pl.BlockSpec((MOD_ROWS, d), lambda j: (0, j)),
        out_shape=jax.ShapeDtypeStruct((MOD_ROWS, 3 * d), F32),
        compiler_params=pltpu.CompilerParams(
            dimension_semantics=("parallel",), vmem_limit_bytes=VMEM_LIMIT),
        name="mod",
    )(c_pad, w_c, b_c)


def _norm_rows(x_ref, mod_ref, g, h_ref, slot, r0):
    d = x_ref.shape[1]
    x = x_ref[pl.ds(r0, NORM_RB), :]
    ms = jnp.mean(x * x, axis=-1, keepdims=True)
    y = x * lax.rsqrt(ms + EPS) * g
    h = y * (1.0 + mod_ref[:, d:2 * d]) + mod_ref[:, 0:d]
    h_ref[slot, pl.ds(r0, NORM_RB), :] = h.astype(BF16)


def _proj_kernel(x0_ref, mod0_ref, xn_ref, modn_ref, g_ref, w_ref, o_ref, h_ref):
    i = pl.program_id(0)
    tm, d = xn_ref.shape
    g = g_ref[...]

    @pl.when(i == 0)
    def _():
        def body(k, carry):
            _norm_rows(x0_ref, mod0_ref, g, h_ref, 0, pl.multiple_of(k * NORM_RB, NORM_RB))
            return carry
        lax.fori_loop(0, tm // NORM_RB, body, 0)

    cur = i % 2
    for k in range(tm // NORM_RB):
        _norm_rows(xn_ref, modn_ref, g, h_ref, 1 - cur, k * NORM_RB)
    for j in range(w_ref.shape[1] // d):
        cols = slice(j * d, (j + 1) * d)
        o_ref[:, cols] = jnp.dot(h_ref[cur], w_ref[:, cols],
                                 preferred_element_type=F32).astype(BF16)


def _projection(x2, mod3, g_pre, w_in, seq):
    n, d = x2.shape
    n_tiles = n // PROJ_TM
    nxt = lambda i: jnp.minimum(i + 1, n_tiles - 1)
    return pl.pallas_call(
        _proj_kernel,
        grid=(n_tiles,),
        in_specs=[
            pl.BlockSpec((PROJ_TM, d), lambda i: (0, 0), pipeline_mode=pl.Buffered(1)),
            pl.BlockSpec((None, 1, 3 * d), lambda i: (0, 0, 0)),
            pl.BlockSpec((PROJ_TM, d), lambda i: (nxt(i), 0)),
            pl.BlockSpec((None, 1, 3 * d), lambda i: ((nxt(i) * PROJ_TM) // seq, 0, 0)),
            pl.BlockSpec((1, d), lambda i: (0, 0)),
            pl.BlockSpec(w_in.shape, lambda i: (0, 0), pipeline_mode=pl.Buffered(1)),
        ],
        out_specs=pl.BlockSpec((PROJ_TM, w_in.shape[1]), lambda i: (i, 0)),
        out_shape=jax.ShapeDtypeStruct((n, w_in.shape[1]), BF16),
        scratch_shapes=[pltpu.VMEM((2, PROJ_TM, d), BF16)],
        compiler_params=pltpu.CompilerParams(
            dimension_semantics=("arbitrary",), vmem_limit_bytes=VMEM_LIMIT),
        name="proj",
    )(x2, mod3, x2, mod3, g_pre, w_in)


def _rglru_kernel(xa_ref, prev_ref, next_ref, shift_ref, cw_ref, cb_ref, wg_ref, bg_ref, lam_ref,
                  o_ref, sa_ref, su_ref, carry_ref):
    dirn = pl.program_id(0)
    i = pl.program_id(1)
    n_chunks = pl.num_programs(1)
    nb, tc, d = xa_ref.shape
    groups = tc // SUBLANES
    n_tiles = d // MXU_TILE
    rev = dirn == 1
    ci = jnp.where(rev, n_chunks - 1 - i, i)

    @pl.when(i == 0)
    def _():
        carry_ref[...] = jnp.zeros_like(carry_ref)

    has_prev = ci > 0
    has_next = ci < n_chunks - 1
    cwh = 0.5 * cw_ref[...]
    cbh = 0.5 * cb_ref[...]
    lam = lam_ref[...]
    softplus_neg_lam = jnp.maximum(-lam, 0.0) + jnp.log1p(jnp.exp(-jnp.abs(lam)))
    c8l = (-0.5 * RG_C * LOG2_E) * softplus_neg_lam
    brh = 0.5 * bg_ref[0]
    bih = 0.5 * bg_ref[1]
    shift_m = shift_ref[...]
    taps = [k for k in range(CONV_W) if k != CONV_W // 2]

    def gates(b, carry):
        xb = xa_ref[b]
        halo = jnp.concatenate([
            jnp.where(has_prev, prev_ref[b], jnp.zeros_like(prev_ref[b])),
            jnp.where(has_next, next_ref[b], jnp.zeros_like(next_ref[b]))], axis=0)
        sh = jnp.dot(shift_m, jnp.concatenate([xb, halo], axis=0), preferred_element_type=F32)
        xh = cbh + cwh[CONV_W // 2:CONV_W // 2 + 1] * xb.astype(F32)
        for n, k in enumerate(taps):
            xh = xh + cwh[k:k + 1] * sh[n * tc:(n + 1) * tc]
        xhb = xh.astype(BF16)
        for k in range(n_tiles):
            sl = slice(k * MXU_TILE, (k + 1) * MXU_TILE)
            xk = xhb[:, sl]
            tr = jnp.tanh(jnp.dot(xk, wg_ref[0, k], preferred_element_type=F32) + brh[:, sl])
            ti = jnp.tanh(jnp.dot(xk, wg_ref[1, k], preferred_element_type=F32) + bih[:, sl])
            a = jnp.exp2(c8l[:, sl] * (tr + 1.0))
            z = 1.0 - a * a
            u = (z * lax.rsqrt(jnp.maximum(z, RSQRT_FLOOR))) * ((ti + 1.0) * xh[:, sl])
            for c2 in range(MXU_TILE // LANES):
                cblk = k * (MXU_TILE // LANES) + c2
                rows = slice(cblk * SUBLANES, (cblk + 1) * SUBLANES)
                lanes = slice(c2 * LANES, (c2 + 1) * LANES)
                sa_ref[b, :, rows, :] = a[:, lanes].reshape(groups, SUBLANES, LANES)
                su_ref[b, :, rows, :] = u[:, lanes].reshape(groups, SUBLANES, LANES)
        return carry

    lax.fori_loop(0, nb, gates, 0, unroll=2)

    def scan(it, hs):
        gp = jnp.where(rev, groups // 2 - 1 - it, it)
        hs = list(hs)
        for gg in range(2):
            g = 2 * gp + jnp.where(rev, 1 - gg, gg)
            for jj in range(SUBLANES):
                j = jnp.where(rev, SUBLANES - 1 - jj, jj)
                tok = pl.ds(j, SUBLANES, stride=SUBLANES)
                for b in range(nb):
                    hb = sa_ref[b, g, tok, :] * hs[b] + su_ref[b, g, tok, :]
                    su_ref[b, g, tok, :] = hb
                    hs[b] = hb
        r0 = pl.multiple_of(gp * 2 * SUBLANES, 2 * SUBLANES)
        for b in range(nb):
            for cblk in range(d // LANES):
                blk = slice(cblk * SUBLANES, (cblk + 1) * SUBLANES)
                rows16 = jnp.concatenate([su_ref[b, 2 * gp, blk, :], su_ref[b, 2 * gp + 1, blk, :]],
                                         axis=0)
                o_ref[b, pl.ds(r0, 2 * SUBLANES), cblk * LANES:(cblk + 1) * LANES] = (
                    rows16.astype(o_ref.dtype))
        return tuple(hs)

    hs = lax.fori_loop(0, groups // 2, scan, tuple(carry_ref[b] for b in range(nb)))
    for b in range(nb):
        carry_ref[b] = hs[b]


def _shift_matrix(tc):
    taps = [k for k in range(CONV_W) if k != CONV_W // 2]
    m = np.zeros((len(taps) * tc, tc + 2 * SUBLANES), np.float32)
    for n, k in enumerate(taps):
        for t in range(tc):
            s = t - CONV_W // 2 + k
            if s < 0:
                col = tc + SUBLANES + s
            elif s >= tc:
                col = tc + SUBLANES + (s - tc)
            else:
                col = s
            m[n * tc + t, col] = 1.0
    return m


def _rglru(proj3, conv_w, conv_b, wg, bg, lam):
    nb, seq, _ = proj3.shape
    d = conv_w.shape[1]
    n_chunks = seq // SCAN_TC
    n_groups = seq // SUBLANES
    gpc = SCAN_TC // SUBLANES
    proj4 = proj3.reshape(nb, n_groups, SUBLANES, proj3.shape[2])

    def chunk(dd, i):
        return jnp.where(dd == 1, n_chunks - 1 - i, i)

    n_tiles = d // MXU_TILE
    shift_m = jnp.asarray(_shift_matrix(SCAN_TC), BF16)
    return pl.pallas_call(
        _rglru_kernel,
        grid=(2, n_chunks),
        in_specs=[
            pl.BlockSpec((nb, SCAN_TC, d), lambda dd, i: (0, chunk(dd, i), 0)),
            pl.BlockSpec((nb, None, SUBLANES, d),
                         lambda dd, i: (0, jnp.maximum(chunk(dd, i) * gpc - 1, 0), 0, 0)),
            pl.BlockSpec((nb, None, SUBLANES, d),
                         lambda dd, i: (0, jnp.minimum((chunk(dd, i) + 1) * gpc, n_groups - 1), 0, 0)),
            pl.BlockSpec(shift_m.shape, lambda dd, i: (0, 0)),
            pl.BlockSpec((CONV_W, d), lambda dd, i: (0, 0)),
            pl.BlockSpec((1, d), lambda dd, i: (0, 0)),
            pl.BlockSpec((None, 2, n_tiles, MXU_TILE, MXU_TILE), lambda dd, i: (dd, 0, 0, 0, 0)),
            pl.BlockSpec((None, 2, 1, d), lambda dd, i: (dd, 0, 0, 0)),
            pl.BlockSpec((None, 1, d), lambda dd, i: (dd, 0, 0)),
        ],
        out_specs=pl.BlockSpec((None, nb, SCAN_TC, d), lambda dd, i: (dd, 0, chunk(dd, i), 0)),
        out_shape=jax.ShapeDtypeStruct((2, nb, seq, d), BF16),
        scratch_shapes=[
            pltpu.VMEM((nb, gpc, d // LANES * SUBLANES, LANES), F32),
            pltpu.VMEM((nb, gpc, d // LANES * SUBLANES, LANES), F32),
            pltpu.VMEM((nb, SUBLANES, LANES), F32),
        ],
        compiler_params=pltpu.CompilerParams(
            dimension_semantics=("arbitrary", "arbitrary"), vmem_limit_bytes=VMEM_LIMIT),
        name="rglru",
    )(proj3, proj4, proj4, shift_m, conv_w, conv_b, wg, bg, lam)


def _natten_kernel(q_ref, k_ref, v_ref, *rest):
    bias_refs, o_ref = rest[:-1], rest[-1]
    step = pl.program_id(1)
    rows = pl.num_programs(1) * NAT_ROWS
    gw = GRID_W
    d = q_ref.shape[1]
    win_r = min(WIN_R, rows)
    lane = lax.broadcasted_iota(jnp.int32, (gw, LANES), 1)
    lo = lane < HEAD_DIM
    zero = jnp.zeros((gw, LANES), BF16)
    n_pairs = d // LANES
    wins = []
    scores = []
    for rr in range(NAT_ROWS):
        r = step * NAT_ROWS + rr
        r0 = jnp.clip(r - win_r // 2, 0, rows - win_r)
        k0 = pl.multiple_of(r0 * gw, gw)
        wins.append(k0)
        q = q_ref[rr * gw:(rr + 1) * gw, :] * (HEAD_DIM ** -0.5)
        for p in range(n_pairs):
            sl = slice(p * LANES, (p + 1) * LANES)
            qp = q[:, sl]
            qs = jnp.concatenate([jnp.where(lo, qp, zero), jnp.where(lo, zero, qp)], axis=0)
            scores.append(lax.dot_general(qs, k_ref[pl.ds(k0, win_r * gw), sl],
                                          (((1,), (1,)), ((), ())), preferred_element_type=F32))
    probs = []
    for rr in range(NAT_ROWS):
        for p in range(n_pairs):
            s = scores[rr * n_pairs + p] + bias_refs[rr][p]
            m = jnp.max(s, axis=-1, keepdims=True)
            e = jnp.exp(s - m)
            probs.append((e.astype(BF16), jnp.sum(e, axis=-1, keepdims=True)))
    for rr in range(NAT_ROWS):
        for p in range(n_pairs):
            sl = slice(p * LANES, (p + 1) * LANES)
            e, l = probs[rr * n_pairs + p]
            o = jnp.dot(e, v_ref[pl.ds(wins[rr], win_r * gw), sl], preferred_element_type=F32) / l
            o_ref[rr * gw:(rr + 1) * gw, sl] = jnp.where(lo, o[:gw], o[gw:]).astype(BF16)


def _natten(proj3, bias_tab, q_col, k_col, v_col):
    nb, seq, _ = proj3.shape
    d = N_HEADS * HEAD_DIM
    rows = seq // GRID_W
    win_r = min(WIN_R, rows)
    n_pairs = d // LANES

    def pattern(r):
        r0 = jnp.clip(r - win_r // 2, 0, rows - win_r)
        return r0 - r + (WIN_R - 1)

    def bias_spec(rr):
        return pl.BlockSpec((None, n_pairs, 2 * GRID_W, win_r * GRID_W),
                            lambda b, i: (pattern(i * NAT_ROWS + rr), 0, 0, 0))

    return pl.pallas_call(
        _natten_kernel,
        grid=(nb, rows // NAT_ROWS),
        in_specs=[
            pl.BlockSpec((None, NAT_ROWS * GRID_W, d), lambda b, i: (b, i, q_col)),
            pl.BlockSpec((None, seq, d), lambda b, i: (b, 0, k_col)),
            pl.BlockSpec((None, seq, d), lambda b, i: (b, 0, v_col)),
        ] + [bias_spec(rr) for rr in range(NAT_ROWS)],
        out_specs=pl.BlockSpec((None, NAT_ROWS * GRID_W, d), lambda b, i: (b, i, 0)),
        out_shape=jax.ShapeDtypeStruct((nb, seq, d), BF16),
        compiler_params=pltpu.CompilerParams(
            dimension_semantics=("parallel", "arbitrary"), vmem_limit_bytes=VMEM_LIMIT),
        name="natten",
    )(proj3, proj3, proj3, *([bias_tab] * NAT_ROWS))


def _bias_kernel(rp_ref, o_ref):
    d0 = pl.program_id(0)
    n_pairs, _, keys = o_ref.shape
    gw = GRID_W
    q_idx = lax.broadcasted_iota(jnp.int32, (gw, LANES), 0)
    lane = lax.broadcasted_iota(jnp.int32, (gw, LANES), 1)
    col = jnp.where(lane < gw, lane, lane - gw)
    c0 = jnp.clip(q_idx - WIN_C // 2, 0, gw - WIN_C)
    valid = (col >= c0) & (col < c0 + WIN_C)
    first = lane < gw
    for pair in range(n_pairs):
        for hh in range(2):
            h = 2 * pair + hh
            for t in range(keys // LANES):
                va = jnp.broadcast_to(rp_ref[h, pl.ds(d0 + 2 * t, 1), :], (gw, LANES))
                vb = jnp.broadcast_to(rp_ref[h, pl.ds(d0 + 2 * t + 1, 1), :], (gw, LANES))
                ta = pltpu.roll(va, 0, 1, stride=1, stride_axis=0)
                tb = pltpu.roll(vb, gw, 1, stride=1, stride_axis=0)
                tile = jnp.where(valid, jnp.where(first, ta, tb), MASK_BIAS)
                o_ref[pair, hh * gw:(hh + 1) * gw, t * LANES:(t + 1) * LANES] = tile


def _bias_table(rpb, rows):
    win_r = min(WIN_R, rows)
    assert 2 * GRID_W == LANES and win_r % 2 == 0
    nh, nd, _ = rpb.shape
    rp = jnp.concatenate([rpb[:, :, WIN_C - 1:],
                          jnp.zeros((nh, nd, LANES - (2 * WIN_C - 1)), rpb.dtype),
                          rpb[:, :, :WIN_C - 1]], axis=-1).astype(F32)
    n_pat = WIN_R
    return pl.pallas_call(
        _bias_kernel,
        grid=(n_pat,),
        in_specs=[pl.BlockSpec(rp.shape, lambda p: (0, 0, 0))],
        out_specs=pl.BlockSpec((None, nh // 2, 2 * GRID_W, win_r * GRID_W), lambda p: (p, 0, 0, 0)),
        out_shape=jax.ShapeDtypeStruct((n_pat, nh // 2, 2 * GRID_W, win_r * GRID_W), F32),
        compiler_params=pltpu.CompilerParams(
            dimension_semantics=("parallel",), vmem_limit_bytes=VMEM_LIMIT),
        name="bias_table",
    )(rp)


def _tail_kernel(final, x_ref, mod_ref, hf_ref, hb_ref, za_ref, zb_ref, ga_ref, gb_ref, att_ref,
                 wpa_ref, wpb_ref, wo_ref, gf_ref, o_ref, w_scr):
    d = x_ref.shape[1]

    @pl.when(pl.program_id(0) == 0)
    def _():
        for n, w_ref in enumerate((wpa_ref, wpb_ref, wo_ref)):
            w_scr[n] = w_ref[...].astype(BF16)

    h = hf_ref[...].astype(F32) + hb_ref[...].astype(F32)
    ya = h * _silu(za_ref[...].astype(F32))
    ya = jnp.dot(ya.astype(BF16), w_scr[0], preferred_element_type=F32)
    yb = att_ref[...].astype(F32) * _silu(zb_ref[...].astype(F32))
    yb = jnp.dot(yb.astype(BF16), w_scr[1], preferred_element_type=F32)
    merged = _sigmoid(ga_ref[...].astype(F32)) * ya + _sigmoid(gb_ref[...].astype(F32)) * yb
    res = jnp.dot(merged.astype(BF16), w_scr[2], preferred_element_type=F32)
    xo = x_ref[...] + mod_ref[:, 2 * d:3 * d] * res
    if final:
        ms = jnp.mean(xo * xo, axis=-1, keepdims=True)
        xo = xo * lax.rsqrt(ms + EPS) * gf_ref[...]
    o_ref[...] = xo


def _tail(x2, mod3, h2, proj, att2, w_pa, w_pb, w_o, g_final, seq, cols, final):
    n, d = x2.shape
    za_col, zb_col, ga_col, gb_col = cols
    tok = lambda i: (i, 0)
    wspec = pl.BlockSpec((d, d), lambda i: (0, 0), pipeline_mode=pl.Buffered(1))
    return pl.pallas_call(
        functools.partial(_tail_kernel, final),
        grid=(n // TAIL_TM,),
        in_specs=[
            pl.BlockSpec((TAIL_TM, d), tok),
            pl.BlockSpec((None, 1, 3 * d), lambda i: ((i * TAIL_TM) // seq, 0, 0)),
            pl.BlockSpec((None, TAIL_TM, d), lambda i: (0, i, 0)),
            pl.BlockSpec((None, TAIL_TM, d), lambda i: (1, i, 0)),
            pl.BlockSpec((TAIL_TM, d), lambda i: (i, za_col)),
            pl.BlockSpec((TAIL_TM, d), lambda i: (i, zb_col)),
            pl.BlockSpec((TAIL_TM, d), lambda i: (i, ga_col)),
            pl.BlockSpec((TAIL_TM, d), lambda i: (i, gb_col)),
            pl.BlockSpec((TAIL_TM, d), tok),
            wspec, wspec, wspec,
            pl.BlockSpec((1, d), lambda i: (0, 0)),
        ],
        out_specs=pl.BlockSpec((TAIL_TM, d), tok),
        out_shape=jax.ShapeDtypeStruct((n, d), F32),
        scratch_shapes=[pltpu.VMEM((3, d, d), BF16)],
        compiler_params=pltpu.CompilerParams(
            dimension_semantics=("arbitrary",), vmem_limit_bytes=VMEM_LIMIT),
        name="tail",
    )(x2, mod3, h2, h2, proj, proj, proj, proj, att2, w_pa, w_pb, w_o, g_final)


def _gate_tiles(ws):
    ng, nblk, bw, _ = ws.shape
    per = MXU_TILE // bw
    w6 = ws.astype(BF16).reshape(ng, nblk // per, per, bw, 1, bw)
    eye = jnp.eye(per, dtype=BF16).reshape(1, 1, per, 1, per, 1)
    return (w6 * eye).reshape(ng, nblk // per, MXU_TILE, MXU_TILE)


def kernel(x, c, g_pre, w_c, b_c, w_in, conv_w, conv_b, w_r_f, b_r_f, w_i_f, b_i_f, lam_f,
           w_r_b, b_r_b, w_i_b, b_i_b, lam_b, rpb, w_pa, w_pb, w_o, g_final):
    nb, seq, d = x.shape
    depth = g_pre.shape[0]
    n = nb * seq
    rows = seq // GRID_W
    groups = w_in.shape[2] // d
    xa_col, za_col, q_col, k_col, v_col, zb_col, ga_col, gb_col = range(groups)
    assert xa_col == 0

    c_pad = jnp.zeros((MOD_ROWS, d), F32).at[:nb].set(c)
    x2 = x.reshape(n, d)
    for l in range(depth):
        mod = _modulation(c_pad, w_c[l], b_c[l].reshape(1, 3 * d))
        mod3 = mod[:nb].reshape(nb, 1, 3 * d)

        proj = _projection(x2, mod3, g_pre[l].reshape(1, d), w_in[l].astype(BF16), seq)
        proj3 = proj.reshape(nb, seq, groups * d)

        wg = _gate_tiles(jnp.stack([w_r_f[l], w_i_f[l], w_r_b[l], w_i_b[l]]))
        wg = wg.reshape((2, 2) + wg.shape[1:])
        bg = jnp.stack([
            jnp.stack([b_r_f[l], b_i_f[l]]),
            jnp.stack([b_r_b[l], b_i_b[l]]),
        ]).reshape(2, 2, 1, d)
        lam = jnp.stack([lam_f[l], lam_b[l]]).reshape(2, 1, d)
        h2 = _rglru(proj3, conv_w[l], conv_b[l].reshape(1, d), wg, bg, lam)
        h2 = h2.reshape(2, n, d)

        att = _natten(proj3, _bias_table(rpb[l], rows), q_col, k_col, v_col)

        x2 = _tail(x2, mod3, h2, proj, att.reshape(n, d),
                   w_pa[l], w_pb[l], w_o[l],
                   g_final.reshape(1, d), seq, (za_col, zb_col, ga_col, gb_col),
                   final=(l == depth - 1))
    return x2.reshape(nb, seq, d)
```

```python
import functools

import jax
import jax.numpy as jnp
import numpy as np
from jax import lax
from jax.experimental import pallas as pl
from jax.experimental.pallas import tpu as pltpu

F32 = jnp.float32
BF16 = jnp.bfloat16

SUBLANES = 8
LANES = 128
MXU_TILE = 256

GRID_W = 64
A_BLOCKS = 16
CONV_W = 4
RG_C = 8.0
N_HEADS = 16
HEAD_DIM = 64
WIN_R = 8
WIN_C = 16
EPS = 1e-6
MASK_BIAS = -1e30
LOG2_E = 1.4426950408889634
RSQRT_FLOOR = 1e-30

PROJ_TM = 512
NORM_RB = 64
SCAN_TC = 128
TAIL_TM = 512
TAIL_SUB = 256
NAT_ROWS = 2
MOD_ROWS = 16

VMEM_LIMIT = 56 * 1024 * 1024


def _sigmoid(x):
    return 0.5 * jnp.tanh(0.5 * x) + 0.5


def _silu(x):
    return x * _sigmoid(x)


_ACTIVATIONS = {"id": lambda v: v, "silu": _silu, "sigmoid": _sigmoid}


def _mod_kernel(c_ref, w_ref, b_ref, o_ref):
    sc = _silu(c_ref[...]).astype(BF16)
    w = w_ref[...].astype(BF16)
    o_ref[...] = jnp.dot(sc, w, preferred_element_type=F32) + b_ref[...]


def _modulation(c_pad, w_c, b_c):
    d = w_c.shape[0]
    return pl.pallas_call(
        _mod_kernel,
        grid=(3,),
        in_specs=[
            pl.BlockSpec((MOD_ROWS, d), lambda j: (0, 0)),
            pl.BlockSpec((d, d), lambda j: (0, j)),
            pl.BlockSpec((1, d), lambda j: (0, j)),
        ],
        out_specs=pl.BlockSpec((MOD_ROWS, d), lambda j: (0, j)),
        out_shape=jax.ShapeDtypeStruct((MOD_ROWS, 3 * d), F32),
        compiler_params=pltpu.CompilerParams(
            dimension_semantics=("parallel",), vmem_limit_bytes=VMEM_LIMIT),
        name="mod",
    )(c_pad, w_c, b_c)


def _norm_rows(x_ref, mod_ref, g, h_ref, slot, r0):
    d = x_ref.shape[1]
    x = x_ref[pl.ds(r0, NORM_RB), :]
    ms = jnp.mean(x * x, axis=-1, keepdims=True)
    y = x * lax.rsqrt(ms + EPS) * g
    h = y * (1.0 + mod_ref[:, d:2 * d]) + mod_ref[:, 0:d]
    h_ref[slot, pl.ds(r0, NORM_RB), :] = h.astype(BF16)


def _proj_kernel(acts, x0_ref, mod0_ref, xn_ref, modn_ref, g_ref, w_ref, o_ref, h_ref):
    i = pl.program_id(0)
    tm, d = xn_ref.shape
    g = g_ref[...]

    @pl.when(i == 0)
    def _():
        def body(k, carry):
            _norm_rows(x0_ref, mod0_ref, g, h_ref, 0, pl.multiple_of(k * NORM_RB, NORM_RB))
            return carry
        lax.fori_loop(0, tm // NORM_RB, body, 0)

    cur = i % 2
    for k in range(tm // NORM_RB):
        _norm_rows(xn_ref, modn_ref, g, h_ref, 1 - cur, k * NORM_RB)
    for j, act in enumerate(acts):
        cols = slice(j * d, (j + 1) * d)
        acc = jnp.dot(h_ref[cur], w_ref[:, cols], preferred_element_type=F32)
        o_ref[:, cols] = _ACTIVATIONS[act](acc).astype(BF16)


def _projection(x2, mod3, g_pre, w_in, seq, acts):
    n, d = x2.shape
    n_tiles = n // PROJ_TM
    nxt = lambda i: jnp.minimum(i + 1, n_tiles - 1)
    return pl.pallas_call(
        functools.partial(_proj_kernel, acts),
        grid=(n_tiles,),
        in_specs=[
            pl.BlockSpec((PROJ_TM, d), lambda i: (0, 0), pipeline_mode=pl.Buffered(1)),
            pl.BlockSpec((None, 1, 3 * d), lambda i: (0, 0, 0)),
            pl.BlockSpec((PROJ_TM, d), lambda i: (nxt(i), 0)),
            pl.BlockSpec((None, 1, 3 * d), lambda i: ((nxt(i) * PROJ_TM) // seq, 0, 0)),
            pl.BlockSpec((1, d), lambda i: (0, 0)),
            pl.BlockSpec(w_in.shape, lambda i: (0, 0), pipeline_mode=pl.Buffered(1)),
        ],
        out_specs=pl.BlockSpec((PROJ_TM, w_in.shape[1]), lambda i: (i, 0)),
        out_shape=jax.ShapeDtypeStruct((n, w_in.shape[1]), BF16),
        scratch_shapes=[pltpu.VMEM((2, PROJ_TM, d), BF16)],
        compiler_params=pltpu.CompilerParams(
            dimension_semantics=("arbitrary",), vmem_limit_bytes=VMEM_LIMIT),
        name="proj",
    )(x2, mod3, x2, mod3, g_pre, w_in)


def _rglru_kernel(xa_ref, prev_ref, next_ref, shift_ref, cw_ref, cb_ref, wblk_ref, bg_ref, lam_ref,
                  o_ref, wg_ref, sa_ref, su_ref, carry_ref):
    dirn = pl.program_id(0)
    i = pl.program_id(1)
    n_chunks = pl.num_programs(1)
    nb, tc, d = xa_ref.shape
    groups = tc // SUBLANES
    n_tiles = d // MXU_TILE
    n_gates, n_blk, bw, _ = wblk_ref.shape
    per_tile = MXU_TILE // bw
    rev = dirn == 1
    ci = jnp.where(rev, n_chunks - 1 - i, i)

    @pl.when(i == 0)
    def _():
        carry_ref[...] = jnp.zeros_like(carry_ref)
        wg_ref[...] = jnp.zeros_like(wg_ref)
        for gate in range(n_gates):
            for blk in range(n_blk):
                off = (blk % per_tile) * bw
                wg_ref[gate, blk // per_tile, off:off + bw, off:off + bw] = (
                    wblk_ref[gate, blk].astype(BF16))

    has_prev = ci > 0
    has_next = ci < n_chunks - 1
    cwh = 0.5 * cw_ref[...]
    cbh = 0.5 * cb_ref[...]
    lam = lam_ref[...]
    softplus_neg_lam = jnp.maximum(-lam, 0.0) + jnp.log1p(jnp.exp(-jnp.abs(lam)))
    c8l = (-0.5 * RG_C * LOG2_E) * softplus_neg_lam
    brh = 0.5 * bg_ref[0]
    bih = 0.5 * bg_ref[1]
    shift_m = shift_ref[...]
    taps = [k for k in range(CONV_W) if k != CONV_W // 2]

    def gates(b, carry):
        xb = xa_ref[b]
        halo = jnp.concatenate([
            jnp.where(has_prev, prev_ref[b], jnp.zeros_like(prev_ref[b])),
            jnp.where(has_next, next_ref[b], jnp.zeros_like(next_ref[b]))], axis=0)
        sh = jnp.dot(shift_m, jnp.concatenate([xb, halo], axis=0), preferred_element_type=F32)
        xh = cbh + cwh[CONV_W // 2:CONV_W // 2 + 1] * xb.astype(F32)
        for n, k in enumerate(taps):
            xh = xh + cwh[k:k + 1] * sh[n * tc:(n + 1) * tc]
        xhb = xh.astype(BF16)
        for k in range(n_tiles):
            sl = slice(k * MXU_TILE, (k + 1) * MXU_TILE)
            xk = xhb[:, sl]
            tr = jnp.tanh(jnp.dot(xk, wg_ref[0, k], preferred_element_type=F32) + brh[:, sl])
            ti = jnp.tanh(jnp.dot(xk, wg_ref[1, k], preferred_element_type=F32) + bih[:, sl])
            a = jnp.exp2(c8l[:, sl] * (tr + 1.0))
            z = 1.0 - a * a
            u = (z * lax.rsqrt(jnp.maximum(z, RSQRT_FLOOR))) * ((ti + 1.0) * xh[:, sl])
            for c2 in range(MXU_TILE // LANES):
                cblk = k * (MXU_TILE // LANES) + c2
                rows = slice(cblk * SUBLANES, (cblk + 1) * SUBLANES)
                lanes = slice(c2 * LANES, (c2 + 1) * LANES)
                sa_ref[b, :, rows, :] = a[:, lanes].reshape(groups, SUBLANES, LANES)
                su_ref[b, :, rows, :] = u[:, lanes].reshape(groups, SUBLANES, LANES)
        return carry

    lax.fori_loop(0, nb, gates, 0, unroll=2)

    def scan(it, hs):
        gp = jnp.where(rev, groups // 2 - 1 - it, it)
        hs = list(hs)
        for gg in range(2):
            g = 2 * gp + jnp.where(rev, 1 - gg, gg)
            for jj in range(SUBLANES):
                j = jnp.where(rev, SUBLANES - 1 - jj, jj)
                tok = pl.ds(j, SUBLANES, stride=SUBLANES)
                for b in range(nb):
                    hb = sa_ref[b, g, tok, :] * hs[b] + su_ref[b, g, tok, :]
                    su_ref[b, g, tok, :] = hb
                    hs[b] = hb
        r0 = pl.multiple_of(gp * 2 * SUBLANES, 2 * SUBLANES)
        for b in range(nb):
            for cblk in range(d // LANES):
                blk = slice(cblk * SUBLANES, (cblk + 1) * SUBLANES)
                rows16 = jnp.concatenate([su_ref[b, 2 * gp, blk, :], su_ref[b, 2 * gp + 1, blk, :]],
                                         axis=0)
                o_ref[b, pl.ds(r0, 2 * SUBLANES), cblk * LANES:(cblk + 1) * LANES] = (
                    rows16.astype(o_ref.dtype))
        return tuple(hs)

    hs = lax.fori_loop(0, groups // 2, scan, tuple(carry_ref[b] for b in range(nb)))
    for b in range(nb):
        carry_ref[b] = hs[b]


def _shift_matrix(tc):
    taps = [k for k in range(CONV_W) if k != CONV_W // 2]
    m = np.zeros((len(taps) * tc, tc + 2 * SUBLANES), np.float32)
    for n, k in enumerate(taps):
        for t in range(tc):
            s = t - CONV_W // 2 + k
            if s < 0:
                col = tc + SUBLANES + s
            elif s >= tc:
                col = tc + SUBLANES + (s - tc)
            else:
                col = s
            m[n * tc + t, col] = 1.0
    return m


def _rglru(proj3, conv_w, conv_b, wblk, bg, lam):
    nb, seq, _ = proj3.shape
    d = conv_w.shape[1]
    n_chunks = seq // SCAN_TC
    n_groups = seq // SUBLANES
    gpc = SCAN_TC // SUBLANES
    proj4 = proj3.reshape(nb, n_groups, SUBLANES, proj3.shape[2])

    def chunk(dd, i):
        return jnp.where(dd == 1, n_chunks - 1 - i, i)

    n_tiles = d // MXU_TILE
    n_gates = wblk.shape[1]
    shift_m = jnp.asarray(_shift_matrix(SCAN_TC), BF16)
    return pl.pallas_call(
        _rglru_kernel,
        grid=(2, n_chunks),
        in_specs=[
            pl.BlockSpec((nb, SCAN_TC, d), lambda dd, i: (0, chunk(dd, i), 0)),
            pl.BlockSpec((nb, None, SUBLANES, d),
                         lambda dd, i: (0, jnp.maximum(chunk(dd, i) * gpc - 1, 0), 0, 0)),
            pl.BlockSpec((nb, None, SUBLANES, d),
                         lambda dd, i: (0, jnp.minimum((chunk(dd, i) + 1) * gpc, n_groups - 1), 0, 0)),
            pl.BlockSpec(shift_m.shape, lambda dd, i: (0, 0)),
            pl.BlockSpec((CONV_W, d), lambda dd, i: (0, 0)),
            pl.BlockSpec((1, d), lambda dd, i: (0, 0)),
            pl.BlockSpec((None,) + wblk.shape[1:], lambda dd, i: (dd, 0, 0, 0, 0)),
            pl.BlockSpec((None, n_gates, 1, d), lambda dd, i: (dd, 0, 0, 0)),
            pl.BlockSpec((None, 1, d), lambda dd, i: (dd, 0, 0)),
        ],
        out_specs=pl.BlockSpec((None, nb, SCAN_TC, d), lambda dd, i: (dd, 0, chunk(dd, i), 0)),
        out_shape=jax.ShapeDtypeStruct((2, nb, seq, d), BF16),
        scratch_shapes=[
            pltpu.VMEM((n_gates, n_tiles, MXU_TILE, MXU_TILE), BF16),
            pltpu.VMEM((nb, gpc, d // LANES * SUBLANES, LANES), F32),
            pltpu.VMEM((nb, gpc, d // LANES * SUBLANES, LANES), F32),
            pltpu.VMEM((nb, SUBLANES, LANES), F32),
        ],
        compiler_params=pltpu.CompilerParams(
            dimension_semantics=("arbitrary", "arbitrary"), vmem_limit_bytes=VMEM_LIMIT),
        name="rglru",
    )(proj3, proj4, proj4, shift_m, conv_w, conv_b, wblk, bg, lam)


def _natten_kernel(q_ref, k_ref, v_ref, *rest):
    bias_refs, o_ref = rest[:-1], rest[-1]
    step = pl.program_id(1)
    rows = pl.num_programs(1) * NAT_ROWS
    gw = GRID_W
    d = q_ref.shape[1]
    win_r = min(WIN_R, rows)
    lane = lax.broadcasted_iota(jnp.int32, (gw, LANES), 1)
    lo = lane < HEAD_DIM
    zero = jnp.zeros((gw, LANES), BF16)
    n_pairs = d // LANES
    wins = []
    scores = []
    for rr in range(NAT_ROWS):
        r = step * NAT_ROWS + rr
        r0 = jnp.clip(r - win_r // 2, 0, rows - win_r)
        k0 = pl.multiple_of(r0 * gw, gw)
        wins.append(k0)
        q = q_ref[rr * gw:(rr + 1) * gw, :] * (HEAD_DIM ** -0.5)
        for p in range(n_pairs):
            sl = slice(p * LANES, (p + 1) * LANES)
            qp = q[:, sl]
            qs = jnp.concatenate([jnp.where(lo, qp, zero), jnp.where(lo, zero, qp)], axis=0)
            scores.append(lax.dot_general(qs, k_ref[pl.ds(k0, win_r * gw), sl],
                                          (((1,), (1,)), ((), ())), preferred_element_type=F32))
    probs = []
    for rr in range(NAT_ROWS):
        for p in range(n_pairs):
            s = scores[rr * n_pairs + p] + bias_refs[rr][p]
            m = jnp.max(s, axis=-1, keepdims=True)
            e = jnp.exp(s - m)
            probs.append((e.astype(BF16), jnp.sum(e, axis=-1, keepdims=True)))
    for rr in range(NAT_ROWS):
        for p in range(n_pairs):
            sl = slice(p * LANES, (p + 1) * LANES)
            e, l = probs[rr * n_pairs + p]
            o = jnp.dot(e, v_ref[pl.ds(wins[rr], win_r * gw), sl], preferred_element_type=F32) / l
            o_ref[rr * gw:(rr + 1) * gw, sl] = jnp.where(lo, o[:gw], o[gw:]).astype(BF16)


def _natten(proj3, bias_tab, q_col, k_col, v_col):
    nb, seq, _ = proj3.shape
    d = N_HEADS * HEAD_DIM
    rows = seq // GRID_W
    win_r = min(WIN_R, rows)
    n_pairs = d // LANES

    def pattern(r):
        r0 = jnp.clip(r - win_r // 2, 0, rows - win_r)
        return r0 - r + (WIN_R - 1)

    def bias_spec(rr):
        return pl.BlockSpec((None, n_pairs, 2 * GRID_W, win_r * GRID_W),
                            lambda b, i: (pattern(i * NAT_ROWS + rr), 0, 0, 0))

    return pl.pallas_call(
        _natten_kernel,
        grid=(nb, rows // NAT_ROWS),
        in_specs=[
            pl.BlockSpec((None, NAT_ROWS * GRID_W, d), lambda b, i: (b, i, q_col)),
            pl.BlockSpec((None, seq, d), lambda b, i: (b, 0, k_col)),
            pl.BlockSpec((None, seq, d), lambda b, i: (b, 0, v_col)),
        ] + [bias_spec(rr) for rr in range(NAT_ROWS)],
        out_specs=pl.BlockSpec((None, NAT_ROWS * GRID_W, d), lambda b, i: (b, i, 0)),
        out_shape=jax.ShapeDtypeStruct((nb, seq, d), BF16),
        compiler_params=pltpu.CompilerParams(
            dimension_semantics=("parallel", "arbitrary"), vmem_limit_bytes=VMEM_LIMIT),
        name="natten",
    )(proj3, proj3, proj3, *([bias_tab] * NAT_ROWS))


def _bias_kernel(rp_ref, o_ref):
    d0 = pl.program_id(0)
    n_pairs, _, keys = o_ref.shape
    gw = GRID_W
    q_idx = lax.broadcasted_iota(jnp.int32, (gw, LANES), 0)
    lane = lax.broadcasted_iota(jnp.int32, (gw, LANES), 1)
    col = jnp.where(lane < gw, lane, lane - gw)
    c0 = jnp.clip(q_idx - WIN_C // 2, 0, gw - WIN_C)
    valid = (col >= c0) & (col < c0 + WIN_C)
    first = lane < gw
    for pair in range(n_pairs):
        for hh in range(2):
            h = 2 * pair + hh
            for t in range(keys // LANES):
                va = jnp.broadcast_to(rp_ref[h, pl.ds(d0 + 2 * t, 1), :], (gw, LANES))
                vb = jnp.broadcast_to(rp_ref[h, pl.ds(d0 + 2 * t + 1, 1), :], (gw, LANES))
                ta = pltpu.roll(va, 0, 1, stride=1, stride_axis=0)
                tb = pltpu.roll(vb, gw, 1, stride=1, stride_axis=0)
                tile = jnp.where(valid, jnp.where(first, ta, tb), MASK_BIAS)
                o_ref[pair, hh * gw:(hh + 1) * gw, t * LANES:(t + 1) * LANES] = tile


def _bias_table(rpb, rows):
    win_r = min(WIN_R, rows)
    assert 2 * GRID_W == LANES and win_r % 2 == 0
    nh, nd, _ = rpb.shape
    rp = jnp.concatenate([rpb[:, :, WIN_C - 1:],
                          jnp.zeros((nh, nd, LANES - (2 * WIN_C - 1)), rpb.dtype),
                          rpb[:, :, :WIN_C - 1]], axis=-1).astype(F32)
    n_pat = WIN_R
    return pl.pallas_call(
        _bias_kernel,
        grid=(n_pat,),
        in_specs=[pl.BlockSpec(rp.shape, lambda p: (0, 0, 0))],
        out_specs=pl.BlockSpec((None, nh // 2, 2 * GRID_W, win_r * GRID_W), lambda p: (p, 0, 0, 0)),
        out_shape=jax.ShapeDtypeStruct((n_pat, nh // 2, 2 * GRID_W, win_r * GRID_W), F32),
        compiler_params=pltpu.CompilerParams(
            dimension_semantics=("parallel",), vmem_limit_bytes=VMEM_LIMIT),
        name="bias_table",
    )(rp)


def _tail_kernel(final, x_ref, mod_ref, hf_ref, hb_ref, za_ref, zb_ref, ga_ref, gb_ref, att_ref,
                 wpa_ref, wpb_ref, wo_ref, gf_ref, o_ref, w_scr):
    d = x_ref.shape[1]

    @pl.when(pl.program_id(0) == 0)
    def _():
        for n, w_ref in enumerate((wpa_ref, wpb_ref, wo_ref)):
            w_scr[n] = w_ref[...].astype(BF16)

    for s in range(x_ref.shape[0] // TAIL_SUB):
        rows = slice(s * TAIL_SUB, (s + 1) * TAIL_SUB)
        h = hf_ref[rows, :].astype(F32) + hb_ref[rows, :].astype(F32)
        ya = (h * za_ref[rows, :].astype(F32)).astype(BF16)
        ya = jnp.dot(ya, w_scr[0], preferred_element_type=F32)
        yb = (att_ref[rows, :].astype(F32) * zb_ref[rows, :].astype(F32)).astype(BF16)
        yb = jnp.dot(yb, w_scr[1], preferred_element_type=F32)
        merged = ga_ref[rows, :].astype(F32) * ya + gb_ref[rows, :].astype(F32) * yb
        res = jnp.dot(merged.astype(BF16), w_scr[2], preferred_element_type=F32)
        xo = x_ref[rows, :] + mod_ref[:, 2 * d:3 * d] * res
        if final:
            ms = jnp.mean(xo * xo, axis=-1, keepdims=True)
            xo = xo * lax.rsqrt(ms + EPS) * gf_ref[...]
        o_ref[rows, :] = xo


def _tail(x2, mod3, h2, proj, att2, w_pa, w_pb, w_o, g_final, seq, cols, final):
    n, d = x2.shape
    za_col, zb_col, ga_col, gb_col = cols
    tok = lambda i: (i, 0)
    wspec = pl.BlockSpec((d, d), lambda i: (0, 0), pipeline_mode=pl.Buffered(1))
    return pl.pallas_call(
        functools.partial(_tail_kernel, final),
        grid=(n // TAIL_TM,),
        in_specs=[
            pl.BlockSpec((TAIL_TM, d), tok),
            pl.BlockSpec((None, 1, 3 * d), lambda i: ((i * TAIL_TM) // seq, 0, 0)),
            pl.BlockSpec((None, TAIL_TM, d), lambda i: (0, i, 0)),
            pl.BlockSpec((None, TAIL_TM, d), lambda i: (1, i, 0)),
            pl.BlockSpec((TAIL_TM, d), lambda i: (i, za_col)),
            pl.BlockSpec((TAIL_TM, d), lambda i: (i, zb_col)),
            pl.BlockSpec((TAIL_TM, d), lambda i: (i, ga_col)),
            pl.BlockSpec((TAIL_TM, d), lambda i: (i, gb_col)),
            pl.BlockSpec((TAIL_TM, d), tok),
            wspec, wspec, wspec,
            pl.BlockSpec((1, d), lambda i: (0, 0)),
        ],
        out_specs=pl.BlockSpec((TAIL_TM, d), tok),
        out_shape=jax.ShapeDtypeStruct((n, d), F32),
        scratch_shapes=[pltpu.VMEM((3, d, d), BF16)],
        compiler_params=pltpu.CompilerParams(
            dimension_semantics=("arbitrary",), vmem_limit_bytes=VMEM_LIMIT),
        name="tail",
    )(x2, mod3, h2, h2, proj, proj, proj, proj, att2, w_pa, w_pb, w_o, g_final)


def kernel(x, c, g_pre, w_c, b_c, w_in, conv_w, conv_b, w_r_f, b_r_f, w_i_f, b_i_f, lam_f,
           w_r_b, b_r_b, w_i_b, b_i_b, lam_b, rpb, w_pa, w_pb, w_o, g_final):
    nb, seq, d = x.shape
    depth = g_pre.shape[0]
    n = nb * seq
    rows = seq // GRID_W
    groups = w_in.shape[2] // d
    xa_col, za_col, q_col, k_col, v_col, zb_col, ga_col, gb_col = range(groups)
    assert xa_col == 0
    acts = ["id"] * groups
    acts[za_col] = acts[zb_col] = "silu"
    acts[ga_col] = acts[gb_col] = "sigmoid"

    c_pad = jnp.zeros((MOD_ROWS, d), F32).at[:nb].set(c)
    x2 = x.reshape(n, d)
    for l in range(depth):
        mod = _modulation(c_pad, w_c[l], b_c[l].reshape(1, 3 * d))
        mod3 = mod[:nb].reshape(nb, 1, 3 * d)

        proj = _projection(x2, mod3, g_pre[l].reshape(1, d), w_in[l].astype(BF16), seq,
                           tuple(acts))
        proj3 = proj.reshape(nb, seq, groups * d)

        wblk = jnp.stack([jnp.stack([w_r_f[l], w_i_f[l]]), jnp.stack([w_r_b[l], w_i_b[l]])])
        bg = jnp.stack([
            jnp.stack([b_r_f[l], b_i_f[l]]),
            jnp.stack([b_r_b[l], b_i_b[l]]),
        ]).reshape(2, 2, 1, d)
        lam = jnp.stack([lam_f[l], lam_b[l]]).reshape(2, 1, d)
        h2 = _rglru(proj3, conv_w[l], conv_b[l].reshape(1, d), wblk, bg, lam)
        h2 = h2.reshape(2, n, d)

        att = _natten(proj3, _bias_table(rpb[l], rows), q_col, k_col, v_col)

        x2 = _tail(x2, mod3, h2, proj, att.reshape(n, d),
                   w_pa[l], w_pb[l], w_o[l],
                   g_final.reshape(1, d), seq, (za_col, zb_col, ga_col, gb_col),
                   final=(l == depth - 1))
    return x2.reshape(nb, seq, d)
```

```python
import functools

import jax
import jax.numpy as jnp
import numpy as np
from jax import lax
from jax.experimental import pallas as pl
from jax.experimental.pallas import tpu as pltpu

F32 = jnp.float32
BF16 = jnp.bfloat16

SUBLANES = 8
LANES = 128
MXU_TILE = 256

GRID_W = 64
A_BLOCKS = 16
CONV_W = 4
RG_C = 8.0
N_HEADS = 16
HEAD_DIM = 64
WIN_R = 8
WIN_C = 16
EPS = 1e-6
MASK_BIAS = -1e30
LOG2_E = 1.4426950408889634
RSQRT_FLOOR = 1e-30

PROJ_TM = 512
NORM_RB = 64
SCAN_TC = 128
TAIL_TM = 512
TAIL_SUB = 256
NAT_ROWS = 2
MOD_ROWS = 16

VMEM_LIMIT = 56 * 1024 * 1024


def _sigmoid(x):
    return 0.5 * jnp.tanh(0.5 * x) + 0.5


def _silu(x):
    return x * _sigmoid(x)


_ACTIVATIONS = {"id": lambda v: v, "silu": _silu, "sigmoid": _sigmoid}


def _mod_kernel(c_ref, w_ref, b_ref, o_ref):
    sc = _silu(c_ref[...]).astype(BF16)
    w = w_ref[...].astype(BF16)
    o_ref[...] = jnp.dot(sc, w, preferred_element_type=F32) + b_ref[...]


def _modulation(c_pad, w_c, b_c):
    d = w_c.shape[0]
    return pl.pallas_call(
        _mod_kernel,
        grid=(3,),
        in_specs=[
            pl.BlockSpec((MOD_ROWS, d), lambda j: (0, 0)),
            pl.BlockSpec((d, d), lambda j: (0, j)),
            pl.BlockSpec((1, d), lambda j: (0, j)),
        ],
        out_specs=pl.BlockSpec((MOD_ROWS, d), lambda j: (0, j)),
        out_shape=jax.ShapeDtypeStruct((MOD_ROWS, 3 * d), F32),
        compiler_params=pltpu.CompilerParams(
            dimension_semantics=("parallel",), vmem_limit_bytes=VMEM_LIMIT),
        name="mod",
    )(c_pad, w_c, b_c)


def _norm_rows(x_ref, mod_ref, g, h_ref, slot, r0):
    d = x_ref.shape[1]
    x = x_ref[pl.ds(r0, NORM_RB), :]
    ms = jnp.mean(x * x, axis=-1, keepdims=True)
    y = x * lax.rsqrt(ms + EPS) * g
    h = y * (1.0 + mod_ref[:, d:2 * d]) + mod_ref[:, 0:d]
    h_ref[slot, pl.ds(r0, NORM_RB), :] = h.astype(BF16)


def _proj_kernel(acts, x0_ref, mod0_ref, xn_ref, modn_ref, g_ref, w_ref, o_ref, h_ref):
    i = pl.program_id(0)
    tm, d = xn_ref.shape
    g = g_ref[...]

    @pl.when(i == 0)
    def _():
        def body(k, carry):
            _norm_rows(x0_ref, mod0_ref, g, h_ref, 0, pl.multiple_of(k * NORM_RB, NORM_RB))
            return carry
        lax.fori_loop(0, tm // NORM_RB, body, 0)

    cur = i % 2
    for k in range(tm // NORM_RB):
        _norm_rows(xn_ref, modn_ref, g, h_ref, 1 - cur, k * NORM_RB)
    for j, act in enumerate(acts):
        cols = slice(j * d, (j + 1) * d)
        acc = jnp.dot(h_ref[cur], w_ref[:, cols], preferred_element_type=F32)
        o_ref[:, cols] = _ACTIVATIONS[act](acc).astype(BF16)


def _projection(x2, mod3, g_pre, w_in, seq, acts):
    n, d = x2.shape
    n_tiles = n // PROJ_TM
    nxt = lambda i: jnp.minimum(i + 1, n_tiles - 1)
    return pl.pallas_call(
        functools.partial(_proj_kernel, acts),
        grid=(n_tiles,),
        in_specs=[
            pl.BlockSpec((PROJ_TM, d), lambda i: (0, 0), pipeline_mode=pl.Buffered(1)),
            pl.BlockSpec((None, 1, 3 * d), lambda i: (0, 0, 0)),
            pl.BlockSpec((PROJ_TM, d), lambda i: (nxt(i), 0)),
            pl.BlockSpec((None, 1, 3 * d), lambda i: ((nxt(i) * PROJ_TM) // seq, 0, 0)),
            pl.BlockSpec((1, d), lambda i: (0, 0)),
            pl.BlockSpec(w_in.shape, lambda i: (0, 0), pipeline_mode=pl.Buffered(1)),
        ],
        out_specs=pl.BlockSpec((PROJ_TM, w_in.shape[1]), lambda i: (i, 0)),
        out_shape=jax.ShapeDtypeStruct((n, w_in.shape[1]), BF16),
        scratch_shapes=[pltpu.VMEM((2, PROJ_TM, d), BF16)],
        compiler_params=pltpu.CompilerParams(
            dimension_semantics=("arbitrary",), vmem_limit_bytes=VMEM_LIMIT),
        name="proj",
    )(x2, mod3, x2, mod3, g_pre, w_in)


def _rglru_kernel(rev, xa_ref, prev_ref, next_ref, shift_ref, cw_ref, cb_ref, wblk_ref, bg_ref,
                  lam_ref, o_ref, wg_ref, sa_ref, su_ref, sh_ref, carry_ref):
    i = pl.program_id(0)
    n_chunks = pl.num_programs(0) - 1
    nb, tc, d = xa_ref.shape
    groups = tc // SUBLANES
    n_tiles = d // MXU_TILE
    n_gates, n_blk, bw, _ = wblk_ref.shape
    per_tile = MXU_TILE // bw
    ig = jnp.minimum(i, n_chunks - 1)
    ci = n_chunks - 1 - ig if rev else ig
    cur = i % 2
    old = 1 - cur

    @pl.when(i == 0)
    def _():
        carry_ref[...] = jnp.zeros_like(carry_ref)
        sa_ref[1] = jnp.zeros(sa_ref.shape[1:], F32)
        su_ref[1] = jnp.zeros(su_ref.shape[1:], F32)
        wg_ref[...] = jnp.zeros_like(wg_ref)
        for gate in range(n_gates):
            for blk in range(n_blk):
                off = (blk % per_tile) * bw
                wg_ref[gate, blk // per_tile, off:off + bw, off:off + bw] = (
                    wblk_ref[gate, blk].astype(BF16))

    has_prev = ci > 0
    has_next = ci < n_chunks - 1
    cwh = 0.5 * cw_ref[...]
    cbh = 0.5 * cb_ref[...]
    lam = lam_ref[...]
    softplus_neg_lam = jnp.maximum(-lam, 0.0) + jnp.log1p(jnp.exp(-jnp.abs(lam)))
    c8l = (-0.5 * RG_C * LOG2_E) * softplus_neg_lam
    brh = 0.5 * bg_ref[0]
    bih = 0.5 * bg_ref[1]
    shift_m = shift_ref[...]
    taps = [k for k in range(CONV_W) if k != CONV_W // 2]

    def scan_row(b):
        h = carry_ref[b]
        for gg in range(groups):
            g = groups - 1 - gg if rev else gg
            for jj in range(SUBLANES):
                j = SUBLANES - 1 - jj if rev else jj
                tok = pl.ds(j, SUBLANES, stride=SUBLANES)
                h = sa_ref[old, b, g, tok, :] * h + su_ref[old, b, g, tok, :]
                sh_ref[b, g, tok, :] = h
            if gg % 2 == 1:
                gp = g // 2
                r0 = gp * 2 * SUBLANES
                for cblk in range(d // LANES):
                    blk = slice(cblk * SUBLANES, (cblk + 1) * SUBLANES)
                    rows16 = jnp.concatenate(
                        [sh_ref[b, 2 * gp, blk, :], sh_ref[b, 2 * gp + 1, blk, :]], axis=0)
                    o_ref[b, pl.ds(r0, 2 * SUBLANES), cblk * LANES:(cblk + 1) * LANES] = (
                        rows16.astype(o_ref.dtype))
        carry_ref[b] = h

    def gates(b):
        xb = xa_ref[b]
        halo = jnp.concatenate([
            jnp.where(has_prev, prev_ref[b], jnp.zeros_like(prev_ref[b])),
            jnp.where(has_next, next_ref[b], jnp.zeros_like(next_ref[b]))], axis=0)
        sh = jnp.dot(shift_m, jnp.concatenate([xb, halo], axis=0), preferred_element_type=F32)
        xh = cbh + cwh[CONV_W // 2:CONV_W // 2 + 1] * xb.astype(F32)
        for n, k in enumerate(taps):
            xh = xh + cwh[k:k + 1] * sh[n * tc:(n + 1) * tc]
        xhb = xh.astype(BF16)
        for k in range(n_tiles):
            sl = slice(k * MXU_TILE, (k + 1) * MXU_TILE)
            xk = xhb[:, sl]
            tr = jnp.tanh(jnp.dot(xk, wg_ref[0, k], preferred_element_type=F32) + brh[:, sl])
            ti = jnp.tanh(jnp.dot(xk, wg_ref[1, k], preferred_element_type=F32) + bih[:, sl])
            a = jnp.exp2(c8l[:, sl] * (tr + 1.0))
            z = 1.0 - a * a
            u = (z * lax.rsqrt(jnp.maximum(z, RSQRT_FLOOR))) * ((ti + 1.0) * xh[:, sl])
            for c2 in range(MXU_TILE // LANES):
                cblk = k * (MXU_TILE // LANES) + c2
                rows = slice(cblk * SUBLANES, (cblk + 1) * SUBLANES)
                lanes = slice(c2 * LANES, (c2 + 1) * LANES)
                sa_ref[cur, b, :, rows, :] = a[:, lanes].reshape(groups, SUBLANES, LANES)
                su_ref[cur, b, :, rows, :] = u[:, lanes].reshape(groups, SUBLANES, LANES)

    def body(b, carry):
        scan_row(b)
        gates(b)
        return carry

    lax.fori_loop(0, nb, body, 0)


def _shift_matrix(tc):
    taps = [k for k in range(CONV_W) if k != CONV_W // 2]
    m = np.zeros((len(taps) * tc, tc + 2 * SUBLANES), np.float32)
    for n, k in enumerate(taps):
        for t in range(tc):
            s = t - CONV_W // 2 + k
            if s < 0:
                col = tc + SUBLANES + s
            elif s >= tc:
                col = tc + SUBLANES + (s - tc)
            else:
                col = s
            m[n * tc + t, col] = 1.0
    return m


def _rglru(proj3, conv_w, conv_b, wblk, bg, lam, rev):
    nb, seq, _ = proj3.shape
    d = conv_w.shape[1]
    n_chunks = seq // SCAN_TC
    n_groups = seq // SUBLANES
    gpc = SCAN_TC // SUBLANES
    proj4 = proj3.reshape(nb, n_groups, SUBLANES, proj3.shape[2])

    def chunk(i):
        return n_chunks - 1 - i if rev else i

    def gated(i):
        return chunk(jnp.minimum(i, n_chunks - 1))

    def scanned(i):
        return chunk(jnp.maximum(i - 1, 0))

    n_tiles = d // MXU_TILE
    n_gates = wblk.shape[0]
    shift_m = jnp.asarray(_shift_matrix(SCAN_TC), BF16)
    whole = lambda a: pl.BlockSpec(a.shape, lambda i: (0,) * a.ndim)
    return pl.pallas_call(
        functools.partial(_rglru_kernel, rev),
        grid=(n_chunks + 1,),
        in_specs=[
            pl.BlockSpec((nb, SCAN_TC, d), lambda i: (0, gated(i), 0)),
            pl.BlockSpec((nb, None, SUBLANES, d),
                         lambda i: (0, jnp.maximum(gated(i) * gpc - 1, 0), 0, 0)),
            pl.BlockSpec((nb, None, SUBLANES, d),
                         lambda i: (0, jnp.minimum((gated(i) + 1) * gpc, n_groups - 1), 0, 0)),
            whole(shift_m), whole(conv_w), whole(conv_b), whole(wblk), whole(bg), whole(lam),
        ],
        out_specs=pl.BlockSpec((nb, SCAN_TC, d), lambda i: (0, scanned(i), 0)),
        out_shape=jax.ShapeDtypeStruct((nb, seq, d), BF16),
        scratch_shapes=[
            pltpu.VMEM((n_gates, n_tiles, MXU_TILE, MXU_TILE), BF16),
            pltpu.VMEM((2, nb, gpc, d // LANES * SUBLANES, LANES), F32),
            pltpu.VMEM((2, nb, gpc, d // LANES * SUBLANES, LANES), F32),
            pltpu.VMEM((nb, gpc, d // LANES * SUBLANES, LANES), F32),
            pltpu.VMEM((nb, SUBLANES, LANES), F32),
        ],
        compiler_params=pltpu.CompilerParams(
            dimension_semantics=("arbitrary",), vmem_limit_bytes=VMEM_LIMIT),
        name="rglru_bwd" if rev else "rglru_fwd",
    )(proj3, proj4, proj4, shift_m, conv_w, conv_b, wblk, bg, lam)


def _natten_kernel(q_ref, k_ref, v_ref, *rest):
    bias_refs, o_ref = rest[:-1], rest[-1]
    step = pl.program_id(1)
    rows = pl.num_programs(1) * NAT_ROWS
    gw = GRID_W
    d = q_ref.shape[1]
    win_r = min(WIN_R, rows)
    lane = lax.broadcasted_iota(jnp.int32, (gw, LANES), 1)
    lo = lane < HEAD_DIM
    zero = jnp.zeros((gw, LANES), BF16)
    n_pairs = d // LANES
    wins = []
    scores = []
    for rr in range(NAT_ROWS):
        r = step * NAT_ROWS + rr
        r0 = jnp.clip(r - win_r // 2, 0, rows - win_r)
        k0 = pl.multiple_of(r0 * gw, gw)
        wins.append(k0)
        q = q_ref[rr * gw:(rr + 1) * gw, :] * (HEAD_DIM ** -0.5)
        for p in range(n_pairs):
            sl = slice(p * LANES, (p + 1) * LANES)
            qp = q[:, sl]
            qs = jnp.concatenate([jnp.where(lo, qp, zero), jnp.where(lo, zero, qp)], axis=0)
            scores.append(lax.dot_general(qs, k_ref[pl.ds(k0, win_r * gw), sl],
                                          (((1,), (1,)), ((), ())), preferred_element_type=F32))
    probs = []
    for rr in range(NAT_ROWS):
        for p in range(n_pairs):
            s = scores[rr * n_pairs + p] + bias_refs[rr][p]
            m = jnp.max(s, axis=-1, keepdims=True)
            e = jnp.exp(s - m)
            probs.append((e.astype(BF16), jnp.sum(e, axis=-1, keepdims=True)))
    for rr in range(NAT_ROWS):
        for p in range(n_pairs):
            sl = slice(p * LANES, (p + 1) * LANES)
            e, l = probs[rr * n_pairs + p]
            o = jnp.dot(e, v_ref[pl.ds(wins[rr], win_r * gw), sl], preferred_element_type=F32) / l
            o_ref[rr * gw:(rr + 1) * gw, sl] = jnp.where(lo, o[:gw], o[gw:]).astype(BF16)


def _natten(proj3, bias_tab, q_col, k_col, v_col):
    nb, seq, _ = proj3.shape
    d = N_HEADS * HEAD_DIM
    rows = seq // GRID_W
    win_r = min(WIN_R, rows)
    n_pairs = d // LANES

    def pattern(r):
        r0 = jnp.clip(r - win_r // 2, 0, rows - win_r)
        return r0 - r + (WIN_R - 1)

    def bias_spec(rr):
        return pl.BlockSpec((None, n_pairs, 2 * GRID_W, win_r * GRID_W),
                            lambda b, i: (pattern(i * NAT_ROWS + rr), 0, 0, 0))

    return pl.pallas_call(
        _natten_kernel,
        grid=(nb, rows // NAT_ROWS),
        in_specs=[
            pl.BlockSpec((None, NAT_ROWS * GRID_W, d), lambda b, i: (b, i, q_col)),
            pl.BlockSpec((None, seq, d), lambda b, i: (b, 0, k_col)),
            pl.BlockSpec((None, seq, d), lambda b, i: (b, 0, v_col)),
        ] + [bias_spec(rr) for rr in range(NAT_ROWS)],
        out_specs=pl.BlockSpec((None, NAT_ROWS * GRID_W, d), lambda b, i: (b, i, 0)),
        out_shape=jax.ShapeDtypeStruct((nb, seq, d), BF16),
        compiler_params=pltpu.CompilerParams(
            dimension_semantics=("parallel", "arbitrary"), vmem_limit_bytes=VMEM_LIMIT),
        name="natten",
    )(proj3, proj3, proj3, *([bias_tab] * NAT_ROWS))


def _bias_kernel(rp_ref, o_ref):
    d0 = pl.program_id(0)
    n_pairs, _, keys = o_ref.shape
    gw = GRID_W
    q_idx = lax.broadcasted_iota(jnp.int32, (gw, LANES), 0)
    lane = lax.broadcasted_iota(jnp.int32, (gw, LANES), 1)
    col = jnp.where(lane < gw, lane, lane - gw)
    c0 = jnp.clip(q_idx - WIN_C // 2, 0, gw - WIN_C)
    valid = (col >= c0) & (col < c0 + WIN_C)
    first = lane < gw
    for pair in range(n_pairs):
        for hh in range(2):
            h = 2 * pair + hh
            for t in range(keys // LANES):
                va = jnp.broadcast_to(rp_ref[h, pl.ds(d0 + 2 * t, 1), :], (gw, LANES))
                vb = jnp.broadcast_to(rp_ref[h, pl.ds(d0 + 2 * t + 1, 1), :], (gw, LANES))
                ta = pltpu.roll(va, 0, 1, stride=1, stride_axis=0)
                tb = pltpu.roll(vb, gw, 1, stride=1, stride_axis=0)
                tile = jnp.where(valid, jnp.where(first, ta, tb), MASK_BIAS)
                o_ref[pair, hh * gw:(hh + 1) * gw, t * LANES:(t + 1) * LANES] = tile


def _bias_table(rpb, rows):
    win_r = min(WIN_R, rows)
    assert 2 * GRID_W == LANES and win_r % 2 == 0
    nh, nd, _ = rpb.shape
    rp = jnp.concatenate([rpb[:, :, WIN_C - 1:],
                          jnp.zeros((nh, nd, LANES - (2 * WIN_C - 1)), rpb.dtype),
                          rpb[:, :, :WIN_C - 1]], axis=-1).astype(F32)
    n_pat = WIN_R
    return pl.pallas_call(
        _bias_kernel,
        grid=(n_pat,),
        in_specs=[pl.BlockSpec(rp.shape, lambda p: (0, 0, 0))],
        out_specs=pl.BlockSpec((None, nh // 2, 2 * GRID_W, win_r * GRID_W), lambda p: (p, 0, 0, 0)),
        out_shape=jax.ShapeDtypeStruct((n_pat, nh // 2, 2 * GRID_W, win_r * GRID_W), F32),
        compiler_params=pltpu.CompilerParams(
            dimension_semantics=("parallel",), vmem_limit_bytes=VMEM_LIMIT),
        name="bias_table",
    )(rp)


def _tail_kernel(final, x_ref, mod_ref, hf_ref, hb_ref, za_ref, zb_ref, ga_ref, gb_ref, att_ref,
                 wpa_ref, wpb_ref, wo_ref, gf_ref, o_ref, w_scr):
    d = x_ref.shape[1]

    @pl.when(pl.program_id(0) == 0)
    def _():
        for n, w_ref in enumerate((wpa_ref, wpb_ref, wo_ref)):
            w_scr[n] = w_ref[...].astype(BF16)

    for s in range(x_ref.shape[0] // TAIL_SUB):
        rows = slice(s * TAIL_SUB, (s + 1) * TAIL_SUB)
        h = hf_ref[rows, :].astype(F32) + hb_ref[rows, :].astype(F32)
        ya = (h * za_ref[rows, :].astype(F32)).astype(BF16)
        ya = jnp.dot(ya, w_scr[0], preferred_element_type=F32)
        yb = (att_ref[rows, :].astype(F32) * zb_ref[rows, :].astype(F32)).astype(BF16)
        yb = jnp.dot(yb, w_scr[1], preferred_element_type=F32)
        merged = ga_ref[rows, :].astype(F32) * ya + gb_ref[rows, :].astype(F32) * yb
        res = jnp.dot(merged.astype(BF16), w_scr[2], preferred_element_type=F32)
        xo = x_ref[rows, :] + mod_ref[:, 2 * d:3 * d] * res
        if final:
            ms = jnp.mean(xo * xo, axis=-1, keepdims=True)
            xo = xo * lax.rsqrt(ms + EPS) * gf_ref[...]
        o_ref[rows, :] = xo


def _tail(x2, mod3, hf, hb, proj, att2, w_pa, w_pb, w_o, g_final, seq, cols, final):
    n, d = x2.shape
    za_col, zb_col, ga_col, gb_col = cols
    tok = lambda i: (i, 0)
    wspec = pl.BlockSpec((d, d), lambda i: (0, 0), pipeline_mode=pl.Buffered(1))
    return pl.pallas_call(
        functools.partial(_tail_kernel, final),
        grid=(n // TAIL_TM,),
        in_specs=[
            pl.BlockSpec((TAIL_TM, d), tok),
            pl.BlockSpec((None, 1, 3 * d), lambda i: ((i * TAIL_TM) // seq, 0, 0)),
            pl.BlockSpec((TAIL_TM, d), tok),
            pl.BlockSpec((TAIL_TM, d), tok),
            pl.BlockSpec((TAIL_TM, d), lambda i: (i, za_col)),
            pl.BlockSpec((TAIL_TM, d), lambda i: (i, zb_col)),
            pl.BlockSpec((TAIL_TM, d), lambda i: (i, ga_col)),
            pl.BlockSpec((TAIL_TM, d), lambda i: (i, gb_col)),
            pl.BlockSpec((TAIL_TM, d), tok),
            wspec, wspec, wspec,
            pl.BlockSpec((1, d), lambda i: (0, 0)),
        ],
        out_specs=pl.BlockSpec((TAIL_TM, d), tok),
        out_shape=jax.ShapeDtypeStruct((n, d), F32),
        scratch_shapes=[pltpu.VMEM((3, d, d), BF16)],
        compiler_params=pltpu.CompilerParams(
            dimension_semantics=("arbitrary",), vmem_limit_bytes=VMEM_LIMIT),
        name="tail",
    )(x2, mod3, hf, hb, proj, proj, proj, proj, att2, w_pa, w_pb, w_o, g_final)


def kernel(x, c, g_pre, w_c, b_c, w_in, conv_w, conv_b, w_r_f, b_r_f, w_i_f, b_i_f, lam_f,
           w_r_b, b_r_b, w_i_b, b_i_b, lam_b, rpb, w_pa, w_pb, w_o, g_final):
    nb, seq, d = x.shape
    depth = g_pre.shape[0]
    n = nb * seq
    rows = seq // GRID_W
    groups = w_in.shape[2] // d
    xa_col, za_col, q_col, k_col, v_col, zb_col, ga_col, gb_col = range(groups)
    assert xa_col == 0
    acts = ["id"] * groups
    acts[za_col] = acts[zb_col] = "silu"
    acts[ga_col] = acts[gb_col] = "sigmoid"

    c_pad = jnp.zeros((MOD_ROWS, d), F32).at[:nb].set(c)
    x2 = x.reshape(n, d)
    for l in range(depth):
        mod = _modulation(c_pad, w_c[l], b_c[l].reshape(1, 3 * d))
        mod3 = mod[:nb].reshape(nb, 1, 3 * d)

        proj = _projection(x2, mod3, g_pre[l].reshape(1, d), w_in[l].astype(BF16), seq,
                           tuple(acts))
        proj3 = proj.reshape(nb, seq, groups * d)

        cw, cb = conv_w[l], conv_b[l].reshape(1, d)
        hf = _rglru(proj3, cw, cb, jnp.stack([w_r_f[l], w_i_f[l]]),
                    jnp.stack([b_r_f[l], b_i_f[l]]).reshape(2, 1, d), lam_f[l].reshape(1, d), False)
        hb = _rglru(proj3, cw, cb, jnp.stack([w_r_b[l], w_i_b[l]]),
                    jnp.stack([b_r_b[l], b_i_b[l]]).reshape(2, 1, d), lam_b[l].reshape(1, d), True)

        att = _natten(proj3, _bias_table(rpb[l], rows), q_col, k_col, v_col)

        x2 = _tail(x2, mod3, hf.reshape(n, d), hb.reshape(n, d), proj, att.reshape(n, d),
                   w_pa[l], w_pb[l], w_o[l],
                   g_final.reshape(1, d), seq, (za_col, zb_col, ga_col, gb_col),
                   final=(l == depth - 1))
    return x2.reshape(nb, seq, d)
```

```python
import functools

import jax
import jax.numpy as jnp
import numpy as np
from jax import lax
from jax.experimental import pallas as pl
from jax.experimental.pallas import tpu as pltpu

F32 = jnp.float32
BF16 = jnp.bfloat16

SUBLANES = 8
LANES = 128
MXU_TILE = 256

GRID_W = 64
A_BLOCKS = 16
CONV_W = 4
RG_C = 8.0
N_HEADS = 16
HEAD_DIM = 64
WIN_R = 8
WIN_C = 16
EPS = 1e-6
MASK_BIAS = -1e30
LOG2_E = 1.4426950408889634
RSQRT_FLOOR = 1e-30

PROJ_TM = 512
NORM_RB = 64
SCAN_TC = 128
TAIL_TM = 512
TAIL_SUB = 256
NAT_ROWS = 2
MOD_ROWS = 16

VMEM_LIMIT = 56 * 1024 * 1024


def _sigmoid(x):
    return 0.5 * jnp.tanh(0.5 * x) + 0.5


def _silu(x):
    return x * _sigmoid(x)


_ACTIVATIONS = {
    "id": lambda v: v,
    "silu": _silu,
    "sigmoid": _sigmoid,
    "qscale": lambda v: v * (HEAD_DIM ** -0.5 * LOG2_E),
}


def _mod_kernel(c_ref, w_ref, b_ref, o_ref):
    sc = _silu(c_ref[...]).astype(BF16)
    w = w_ref[...].astype(BF16)
    o_ref[...] = jnp.dot(sc, w, preferred_element_type=F32) + b_ref[...]


def _modulation(c_pad, w_c, b_c):
    d = w_c.shape[0]
    return pl.pallas_call(
        _mod_kernel,
        grid=(3,),
        in_specs=[
            pl.BlockSpec((MOD_ROWS, d), lambda j: (0, 0)),
            pl.BlockSpec((d, d), lambda j: (0, j)),
            pl.BlockSpec((1, d), lambda j: (0, j)),
        ],
        out_specs=pl.BlockSpec((MOD_ROWS, d), lambda j: (0, j)),
        out_shape=jax.ShapeDtypeStruct((MOD_ROWS, 3 * d), F32),
        compiler_params=pltpu.CompilerParams(
            dimension_semantics=("parallel",), vmem_limit_bytes=VMEM_LIMIT),
        name="mod",
    )(c_pad, w_c, b_c)


def _norm_rows(x_ref, mod_ref, g, h_ref, slot, r0):
    d = x_ref.shape[1]
    x = x_ref[pl.ds(r0, NORM_RB), :]
    ms = jnp.mean(x * x, axis=-1, keepdims=True)
    y = x * lax.rsqrt(ms + EPS) * g
    h = y * (1.0 + mod_ref[:, d:2 * d]) + mod_ref[:, 0:d]
    h_ref[slot, pl.ds(r0, NORM_RB), :] = h.astype(BF16)


def _proj_kernel(acts, x0_ref, mod0_ref, xn_ref, modn_ref, g_ref, w_ref, o_ref, h_ref):
    i = pl.program_id(0)
    tm, d = xn_ref.shape
    g = g_ref[...]

    @pl.when(i == 0)
    def _():
        def body(k, carry):
            _norm_rows(x0_ref, mod0_ref, g, h_ref, 0, pl.multiple_of(k * NORM_RB, NORM_RB))
            return carry
        lax.fori_loop(0, tm // NORM_RB, body, 0)

    cur = i % 2
    for k in range(tm // NORM_RB):
        _norm_rows(xn_ref, modn_ref, g, h_ref, 1 - cur, k * NORM_RB)
    for j, act in enumerate(acts):
        cols = slice(j * d, (j + 1) * d)
        acc = jnp.dot(h_ref[cur], w_ref[:, cols], preferred_element_type=F32)
        o_ref[:, cols] = _ACTIVATIONS[act](acc).astype(BF16)


def _projection(x2, mod3, g_pre, w_in, seq, acts):
    n, d = x2.shape
    n_tiles = n // PROJ_TM
    nxt = lambda i: jnp.minimum(i + 1, n_tiles - 1)
    return pl.pallas_call(
        functools.partial(_proj_kernel, acts),
        grid=(n_tiles,),
        in_specs=[
            pl.BlockSpec((PROJ_TM, d), lambda i: (0, 0), pipeline_mode=pl.Buffered(1)),
            pl.BlockSpec((None, 1, 3 * d), lambda i: (0, 0, 0)),
            pl.BlockSpec((PROJ_TM, d), lambda i: (nxt(i), 0)),
            pl.BlockSpec((None, 1, 3 * d), lambda i: ((nxt(i) * PROJ_TM) // seq, 0, 0)),
            pl.BlockSpec((1, d), lambda i: (0, 0)),
            pl.BlockSpec(w_in.shape, lambda i: (0, 0), pipeline_mode=pl.Buffered(1)),
        ],
        out_specs=pl.BlockSpec((PROJ_TM, w_in.shape[1]), lambda i: (i, 0)),
        out_shape=jax.ShapeDtypeStruct((n, w_in.shape[1]), BF16),
        scratch_shapes=[pltpu.VMEM((2, PROJ_TM, d), BF16)],
        compiler_params=pltpu.CompilerParams(
            dimension_semantics=("arbitrary",), vmem_limit_bytes=VMEM_LIMIT),
        name="proj",
    )(x2, mod3, x2, mod3, g_pre, w_in)


def _rglru_kernel(rev, xa_ref, prev_ref, next_ref, shift_ref, cw_ref, cb_ref, wblk_ref, bg_ref,
                  lam_ref, o_ref, wg_ref, sa_ref, su_ref, sh_ref, carry_ref):
    i = pl.program_id(0)
    n_chunks = pl.num_programs(0) - 1
    nb, tc, d = xa_ref.shape
    groups = tc // SUBLANES
    n_tiles = d // MXU_TILE
    n_gates, n_blk, bw, _ = wblk_ref.shape
    per_tile = MXU_TILE // bw
    ig = jnp.minimum(i, n_chunks - 1)
    ci = n_chunks - 1 - ig if rev else ig
    cur = i % 2
    old = 1 - cur

    @pl.when(i == 0)
    def _():
        carry_ref[...] = jnp.zeros_like(carry_ref)
        sa_ref[1] = jnp.zeros(sa_ref.shape[1:], F32)
        su_ref[1] = jnp.zeros(su_ref.shape[1:], F32)
        wg_ref[...] = jnp.zeros_like(wg_ref)
        for gate in range(n_gates):
            for blk in range(n_blk):
                off = (blk % per_tile) * bw
                wg_ref[gate, blk // per_tile, off:off + bw, off:off + bw] = (
                    wblk_ref[gate, blk].astype(BF16))

    has_prev = ci > 0
    has_next = ci < n_chunks - 1
    cwh = 0.5 * cw_ref[...]
    cbh = 0.5 * cb_ref[...]
    lam = lam_ref[...]
    softplus_neg_lam = jnp.maximum(-lam, 0.0) + jnp.log1p(jnp.exp(-jnp.abs(lam)))
    c8l = (-0.5 * RG_C * LOG2_E) * softplus_neg_lam
    brh = 0.5 * bg_ref[0]
    bih = 0.5 * bg_ref[1]
    shift_m = shift_ref[...]
    taps = [k for k in range(CONV_W) if k != CONV_W // 2]

    def scan_row(b):
        h = carry_ref[b]
        for gg in range(groups):
            g = groups - 1 - gg if rev else gg
            for jj in range(SUBLANES):
                j = SUBLANES - 1 - jj if rev else jj
                tok = pl.ds(j, SUBLANES, stride=SUBLANES)
                h = sa_ref[old, b, g, tok, :] * h + su_ref[old, b, g, tok, :]
                sh_ref[b, g, tok, :] = h
            if gg % 2 == 1:
                gp = g // 2
                r0 = gp * 2 * SUBLANES
                for cblk in range(d // LANES):
                    blk = slice(cblk * SUBLANES, (cblk + 1) * SUBLANES)
                    rows16 = jnp.concatenate(
                        [sh_ref[b, 2 * gp, blk, :], sh_ref[b, 2 * gp + 1, blk, :]], axis=0)
                    o_ref[b, pl.ds(r0, 2 * SUBLANES), cblk * LANES:(cblk + 1) * LANES] = (
                        rows16.astype(o_ref.dtype))
        carry_ref[b] = h

    def gates(b):
        xb = xa_ref[b]
        halo = jnp.concatenate([
            jnp.where(has_prev, prev_ref[b], jnp.zeros_like(prev_ref[b])),
            jnp.where(has_next, next_ref[b], jnp.zeros_like(next_ref[b]))], axis=0)
        sh = jnp.dot(shift_m, jnp.concatenate([xb, halo], axis=0), preferred_element_type=F32)
        xh = cbh + cwh[CONV_W // 2:CONV_W // 2 + 1] * xb.astype(F32)
        for n, k in enumerate(taps):
            xh = xh + cwh[k:k + 1] * sh[n * tc:(n + 1) * tc]
        xhb = xh.astype(BF16)
        for k in range(n_tiles):
            sl = slice(k * MXU_TILE, (k + 1) * MXU_TILE)
            xk = xhb[:, sl]
            tr = jnp.tanh(jnp.dot(xk, wg_ref[0, k], preferred_element_type=F32) + brh[:, sl])
            ti = jnp.tanh(jnp.dot(xk, wg_ref[1, k], preferred_element_type=F32) + bih[:, sl])
            a = jnp.exp2(c8l[:, sl] * (tr + 1.0))
            z = 1.0 - a * a
            u = (z * lax.rsqrt(jnp.maximum(z, RSQRT_FLOOR))) * ((ti + 1.0) * xh[:, sl])
            for c2 in range(MXU_TILE // LANES):
                cblk = k * (MXU_TILE // LANES) + c2
                rows = slice(cblk * SUBLANES, (cblk + 1) * SUBLANES)
                lanes = slice(c2 * LANES, (c2 + 1) * LANES)
                sa_ref[cur, b, :, rows, :] = a[:, lanes].reshape(groups, SUBLANES, LANES)
                su_ref[cur, b, :, rows, :] = u[:, lanes].reshape(groups, SUBLANES, LANES)

    def body(b, carry):
        scan_row(b)
        gates(b)
        return carry

    lax.fori_loop(0, nb, body, 0)


def _shift_matrix(tc):
    taps = [k for k in range(CONV_W) if k != CONV_W // 2]
    m = np.zeros((len(taps) * tc, tc + 2 * SUBLANES), np.float32)
    for n, k in enumerate(taps):
        for t in range(tc):
            s = t - CONV_W // 2 + k
            if s < 0:
                col = tc + SUBLANES + s
            elif s >= tc:
                col = tc + SUBLANES + (s - tc)
            else:
                col = s
            m[n * tc + t, col] = 1.0
    return m


def _rglru(proj3, conv_w, conv_b, wblk, bg, lam, rev):
    nb, seq, _ = proj3.shape
    d = conv_w.shape[1]
    n_chunks = seq // SCAN_TC
    n_groups = seq // SUBLANES
    gpc = SCAN_TC // SUBLANES
    proj4 = proj3.reshape(nb, n_groups, SUBLANES, proj3.shape[2])

    def chunk(i):
        return n_chunks - 1 - i if rev else i

    def gated(i):
        return chunk(jnp.minimum(i, n_chunks - 1))

    def scanned(i):
        return chunk(jnp.maximum(i - 1, 0))

    n_tiles = d // MXU_TILE
    n_gates = wblk.shape[0]
    shift_m = jnp.asarray(_shift_matrix(SCAN_TC), BF16)
    whole = lambda a: pl.BlockSpec(a.shape, lambda i: (0,) * a.ndim)
    return pl.pallas_call(
        functools.partial(_rglru_kernel, rev),
        grid=(n_chunks + 1,),
        in_specs=[
            pl.BlockSpec((nb, SCAN_TC, d), lambda i: (0, gated(i), 0)),
            pl.BlockSpec((nb, None, SUBLANES, d),
                         lambda i: (0, jnp.maximum(gated(i) * gpc - 1, 0), 0, 0)),
            pl.BlockSpec((nb, None, SUBLANES, d),
                         lambda i: (0, jnp.minimum((gated(i) + 1) * gpc, n_groups - 1), 0, 0)),
            whole(shift_m), whole(conv_w), whole(conv_b), whole(wblk), whole(bg), whole(lam),
        ],
        out_specs=pl.BlockSpec((nb, SCAN_TC, d), lambda i: (0, scanned(i), 0)),
        out_shape=jax.ShapeDtypeStruct((nb, seq, d), BF16),
        scratch_shapes=[
            pltpu.VMEM((n_gates, n_tiles, MXU_TILE, MXU_TILE), BF16),
            pltpu.VMEM((2, nb, gpc, d // LANES * SUBLANES, LANES), F32),
            pltpu.VMEM((2, nb, gpc, d // LANES * SUBLANES, LANES), F32),
            pltpu.VMEM((nb, gpc, d // LANES * SUBLANES, LANES), F32),
            pltpu.VMEM((nb, SUBLANES, LANES), F32),
        ],
        compiler_params=pltpu.CompilerParams(
            dimension_semantics=("arbitrary",), vmem_limit_bytes=VMEM_LIMIT),
        name="rglru_bwd" if rev else "rglru_fwd",
    )(proj3, proj4, proj4, shift_m, conv_w, conv_b, wblk, bg, lam)


def _natten_kernel(q_ref, k_ref, v_ref, *rest):
    bias_refs, o_ref = rest[:-1], rest[-1]
    step = pl.program_id(1)
    rows = pl.num_programs(1) * NAT_ROWS
    gw = GRID_W
    d = q_ref.shape[1]
    win_r = min(WIN_R, rows)
    lane = lax.broadcasted_iota(jnp.int32, (gw, LANES), 1)
    lo = lane < HEAD_DIM
    zero = jnp.zeros((gw, LANES), BF16)
    n_pairs = d // LANES
    wins = []
    scores = []
    for rr in range(NAT_ROWS):
        r = step * NAT_ROWS + rr
        r0 = jnp.clip(r - win_r // 2, 0, rows - win_r)
        k0 = pl.multiple_of(r0 * gw, gw)
        wins.append(k0)
        for p in range(n_pairs):
            sl = slice(p * LANES, (p + 1) * LANES)
            qp = q_ref[rr * gw:(rr + 1) * gw, sl]
            qs = jnp.concatenate([jnp.where(lo, qp, zero), jnp.where(lo, zero, qp)], axis=0)
            scores.append(lax.dot_general(qs, k_ref[pl.ds(k0, win_r * gw), sl],
                                          (((1,), (1,)), ((), ())), preferred_element_type=F32))
    probs = []
    for rr in range(NAT_ROWS):
        for p in range(n_pairs):
            s = scores[rr * n_pairs + p] + bias_refs[rr][p]
            m = jnp.max(s, axis=-1, keepdims=True)
            probs.append(jnp.exp2(s - m).astype(BF16))
    ones = jnp.ones((win_r * gw, LANES), BF16)
    for rr in range(NAT_ROWS):
        for p in range(n_pairs):
            sl = slice(p * LANES, (p + 1) * LANES)
            v_ext = jnp.concatenate([v_ref[pl.ds(wins[rr], win_r * gw), sl], ones], axis=1)
            pv = jnp.dot(probs[rr * n_pairs + p], v_ext, preferred_element_type=F32)
            o = pv[:, :LANES] / pv[:, LANES:]
            o_ref[rr * gw:(rr + 1) * gw, sl] = jnp.where(lo, o[:gw], o[gw:]).astype(BF16)


def _natten(proj3, bias_tab, q_col, k_col, v_col):
    nb, seq, _ = proj3.shape
    d = N_HEADS * HEAD_DIM
    rows = seq // GRID_W
    win_r = min(WIN_R, rows)
    n_pairs = d // LANES

    def pattern(r):
        r0 = jnp.clip(r - win_r // 2, 0, rows - win_r)
        return r0 - r + (WIN_R - 1)

    def bias_spec(rr):
        return pl.BlockSpec((None, n_pairs, 2 * GRID_W, win_r * GRID_W),
                            lambda b, i: (pattern(i * NAT_ROWS + rr), 0, 0, 0))

    return pl.pallas_call(
        _natten_kernel,
        grid=(nb, rows // NAT_ROWS),
        in_specs=[
            pl.BlockSpec((None, NAT_ROWS * GRID_W, d), lambda b, i: (b, i, q_col)),
            pl.BlockSpec((None, seq, d), lambda b, i: (b, 0, k_col)),
            pl.BlockSpec((None, seq, d), lambda b, i: (b, 0, v_col)),
        ] + [bias_spec(rr) for rr in range(NAT_ROWS)],
        out_specs=pl.BlockSpec((None, NAT_ROWS * GRID_W, d), lambda b, i: (b, i, 0)),
        out_shape=jax.ShapeDtypeStruct((nb, seq, d), BF16),
        compiler_params=pltpu.CompilerParams(
            dimension_semantics=("parallel", "arbitrary"), vmem_limit_bytes=VMEM_LIMIT),
        name="natten",
    )(proj3, proj3, proj3, *([bias_tab] * NAT_ROWS))


def _bias_kernel(rp_ref, o_ref):
    d0 = pl.program_id(0)
    n_pairs, _, keys = o_ref.shape
    gw = GRID_W
    q_idx = lax.broadcasted_iota(jnp.int32, (gw, LANES), 0)
    lane = lax.broadcasted_iota(jnp.int32, (gw, LANES), 1)
    col = jnp.where(lane < gw, lane, lane - gw)
    c0 = jnp.clip(q_idx - WIN_C // 2, 0, gw - WIN_C)
    valid = (col >= c0) & (col < c0 + WIN_C)
    first = lane < gw
    for pair in range(n_pairs):
        for hh in range(2):
            h = 2 * pair + hh
            for t in range(keys // LANES):
                va = jnp.broadcast_to(rp_ref[h, pl.ds(d0 + 2 * t, 1), :], (gw, LANES))
                vb = jnp.broadcast_to(rp_ref[h, pl.ds(d0 + 2 * t + 1, 1), :], (gw, LANES))
                ta = pltpu.roll(va, 0, 1, stride=1, stride_axis=0)
                tb = pltpu.roll(vb, gw, 1, stride=1, stride_axis=0)
                tile = jnp.where(valid, LOG2_E * jnp.where(first, ta, tb), MASK_BIAS)
                o_ref[pair, hh * gw:(hh + 1) * gw, t * LANES:(t + 1) * LANES] = tile


def _bias_table(rpb, rows):
    win_r = min(WIN_R, rows)
    assert 2 * GRID_W == LANES and win_r % 2 == 0
    nh, nd, _ = rpb.shape
    rp = jnp.concatenate([rpb[:, :, WIN_C - 1:],
                          jnp.zeros((nh, nd, LANES - (2 * WIN_C - 1)), rpb.dtype),
                          rpb[:, :, :WIN_C - 1]], axis=-1).astype(F32)
    n_pat = WIN_R
    return pl.pallas_call(
        _bias_kernel,
        grid=(n_pat,),
        in_specs=[pl.BlockSpec(rp.shape, lambda p: (0, 0, 0))],
        out_specs=pl.BlockSpec((None, nh // 2, 2 * GRID_W, win_r * GRID_W), lambda p: (p, 0, 0, 0)),
        out_shape=jax.ShapeDtypeStruct((n_pat, nh // 2, 2 * GRID_W, win_r * GRID_W), F32),
        compiler_params=pltpu.CompilerParams(
            dimension_semantics=("parallel",), vmem_limit_bytes=VMEM_LIMIT),
        name="bias_table",
    )(rp)


def _tail_kernel(final, x_ref, mod_ref, hf_ref, hb_ref, za_ref, zb_ref, ga_ref, gb_ref, att_ref,
                 wpa_ref, wpb_ref, wo_ref, gf_ref, o_ref, w_scr):
    d = x_ref.shape[1]

    @pl.when(pl.program_id(0) == 0)
    def _():
        for n, w_ref in enumerate((wpa_ref, wpb_ref, wo_ref)):
            w_scr[n] = w_ref[...].astype(BF16)

    for s in range(x_ref.shape[0] // TAIL_SUB):
        rows = slice(s * TAIL_SUB, (s + 1) * TAIL_SUB)
        h = hf_ref[rows, :].astype(F32) + hb_ref[rows, :].astype(F32)
        ya = (h * za_ref[rows, :].astype(F32)).astype(BF16)
        ya = jnp.dot(ya, w_scr[0], preferred_element_type=F32)
        yb = (att_ref[rows, :].astype(F32) * zb_ref[rows, :].astype(F32)).astype(BF16)
        yb = jnp.dot(yb, w_scr[1], preferred_element_type=F32)
        merged = ga_ref[rows, :].astype(F32) * ya + gb_ref[rows, :].astype(F32) * yb
        res = jnp.dot(merged.astype(BF16), w_scr[2], preferred_element_type=F32)
        xo = x_ref[rows, :] + mod_ref[:, 2 * d:3 * d] * res
        if final:
            ms = jnp.mean(xo * xo, axis=-1, keepdims=True)
            xo = xo * lax.rsqrt(ms + EPS) * gf_ref[...]
        o_ref[rows, :] = xo


def _tail(x2, mod3, hf, hb, proj, att2, w_pa, w_pb, w_o, g_final, seq, cols, final):
    n, d = x2.shape
    za_col, zb_col, ga_col, gb_col = cols
    tok = lambda i: (i, 0)
    wspec = pl.BlockSpec((d, d), lambda i: (0, 0), pipeline_mode=pl.Buffered(1))
    return pl.pallas_call(
        functools.partial(_tail_kernel, final),
        grid=(n // TAIL_TM,),
        in_specs=[
            pl.BlockSpec((TAIL_TM, d), tok),
            pl.BlockSpec((None, 1, 3 * d), lambda i: ((i * TAIL_TM) // seq, 0, 0)),
            pl.BlockSpec((TAIL_TM, d), tok),
            pl.BlockSpec((TAIL_TM, d), tok),
            pl.BlockSpec((TAIL_TM, d), lambda i: (i, za_col)),
            pl.BlockSpec((TAIL_TM, d), lambda i: (i, zb_col)),
            pl.BlockSpec((TAIL_TM, d), lambda i: (i, ga_col)),
            pl.BlockSpec((TAIL_TM, d), lambda i: (i, gb_col)),
            pl.BlockSpec((TAIL_TM, d), tok),
            wspec, wspec, wspec,
            pl.BlockSpec((1, d), lambda i: (0, 0)),
        ],
        out_specs=pl.BlockSpec((TAIL_TM, d), tok),
        out_shape=jax.ShapeDtypeStruct((n, d), F32),
        scratch_shapes=[pltpu.VMEM((3, d, d), BF16)],
        compiler_params=pltpu.CompilerParams(
            dimension_semantics=("arbitrary",), vmem_limit_bytes=VMEM_LIMIT),
        name="tail",
    )(x2, mod3, hf, hb, proj, proj, proj, proj, att2, w_pa, w_pb, w_o, g_final)


def kernel(x, c, g_pre, w_c, b_c, w_in, conv_w, conv_b, w_r_f, b_r_f, w_i_f, b_i_f, lam_f,
           w_r_b, b_r_b, w_i_b, b_i_b, lam_b, rpb, w_pa, w_pb, w_o, g_final):
    nb, seq, d = x.shape
    depth = g_pre.shape[0]
    n = nb * seq
    rows = seq // GRID_W
    groups = w_in.shape[2] // d
    xa_col, za_col, q_col, k_col, v_col, zb_col, ga_col, gb_col = range(groups)
    assert xa_col == 0
    acts = ["id"] * groups
    acts[za_col] = acts[zb_col] = "silu"
    acts[ga_col] = acts[gb_col] = "sigmoid"
    acts[q_col] = "qscale"

    c_pad = jnp.zeros((MOD_ROWS, d), F32).at[:nb].set(c)
    x2 = x.reshape(n, d)
    for l in range(depth):
        mod = _modulation(c_pad, w_c[l], b_c[l].reshape(1, 3 * d))
        mod3 = mod[:nb].reshape(nb, 1, 3 * d)

        proj = _projection(x2, mod3, g_pre[l].reshape(1, d), w_in[l].astype(BF16), seq,
                           tuple(acts))
        proj3 = proj.reshape(nb, seq, groups * d)

        cw, cb = conv_w[l], conv_b[l].reshape(1, d)
        hf = _rglru(proj3, cw, cb, jnp.stack([w_r_f[l], w_i_f[l]]),
                    jnp.stack([b_r_f[l], b_i_f[l]]).reshape(2, 1, d), lam_f[l].reshape(1, d), False)
        hb = _rglru(proj3, cw, cb, jnp.stack([w_r_b[l], w_i_b[l]]),
                    jnp.stack([b_r_b[l], b_i_b[l]]).reshape(2, 1, d), lam_b[l].reshape(1, d), True)

        att = _natten(proj3, _bias_table(rpb[l], rows), q_col, k_col, v_col)

        x2 = _tail(x2, mod3, hf.reshape(n, d), hb.reshape(n, d), proj, att.reshape(n, d),
                   w_pa[l], w_pb[l], w_o[l],
                   g_final.reshape(1, d), seq, (za_col, zb_col, ga_col, gb_col),
                   final=(l == depth - 1))
    return x2.reshape(nb, seq, d)
```

```python
import functools

import jax
import jax.numpy as jnp
import numpy as np
from jax import lax
from jax.experimental import pallas as pl
from jax.experimental.pallas import tpu as pltpu

F32 = jnp.float32
BF16 = jnp.bfloat16

SUBLANES = 8
LANES = 128
MXU_TILE = 256

GRID_W = 64
A_BLOCKS = 16
CONV_W = 4
RG_C = 8.0
N_HEADS = 16
HEAD_DIM = 64
WIN_R = 8
WIN_C = 16
EPS = 1e-6
MASK_BIAS = -1e30
LOG2_E = 1.4426950408889634
RSQRT_FLOOR = 1e-30

PROJ_TM = 512
NORM_RB = 64
CAST_RB = 128
SCAN_TC = 128
TAIL_TM = 512
TAIL_SUB = 256
NAT_ROWS = 2
MOD_ROWS = 16

VMEM_LIMIT = 56 * 1024 * 1024


def _sigmoid(x):
    return 0.5 * jnp.tanh(0.5 * x) + 0.5


def _silu(x):
    return x * _sigmoid(x)


_ACTIVATIONS = {
    "id": lambda v: v,
    "silu": _silu,
    "sigmoid": _sigmoid,
    "qscale": lambda v: v * (HEAD_DIM ** -0.5 * LOG2_E),
}


def _mod_kernel(c_ref, w_ref, b_ref, o_ref):
    sc = _silu(c_ref[...]).astype(BF16)
    w = w_ref[...].astype(BF16)
    o_ref[...] = jnp.dot(sc, w, preferred_element_type=F32) + b_ref[...]


def _modulation(c_pad, w_c, b_c):
    d = w_c.shape[0]
    return pl.pallas_call(
        _mod_kernel,
        grid=(3,),
        in_specs=[
            pl.BlockSpec((MOD_ROWS, d), lambda j: (0, 0)),
            pl.BlockSpec((d, d), lambda j: (0, j)),
            pl.BlockSpec((1, d), lambda j: (0, j)),
        ],
        out_specs=pl.BlockSpec((MOD_ROWS, d), lambda j: (0, j)),
        out_shape=jax.ShapeDtypeStruct((MOD_ROWS, 3 * d), F32),
        compiler_params=pltpu.CompilerParams(
            dimension_semantics=("parallel",), vmem_limit_bytes=VMEM_LIMIT),
        name="mod",
    )(c_pad, w_c, b_c)


def _norm_rows(x_ref, mod_ref, g, h_ref, slot, r0):
    d = x_ref.shape[1]
    x = x_ref[pl.ds(r0, NORM_RB), :]
    ms = jnp.mean(x * x, axis=-1, keepdims=True)
    y = x * lax.rsqrt(ms + EPS) * g
    h = y * (1.0 + mod_ref[:, d:2 * d]) + mod_ref[:, 0:d]
    h_ref[slot, pl.ds(r0, NORM_RB), :] = h.astype(BF16)


def _proj_kernel(acts, x0_ref, mod0_ref, xn_ref, modn_ref, g_ref, w_hbm, o_ref,
                 h_ref, w_ref, stage_ref, sem):
    i = pl.program_id(0)
    tm, d = xn_ref.shape
    g = g_ref[...]

    def weight_copy(j):
        return pltpu.make_async_copy(w_hbm.at[:, pl.ds(j * d, d)], stage_ref.at[j % 2],
                                     sem.at[j % 2])

    @pl.when(i == 0)
    def _():
        weight_copy(0).start()
        for j in range(len(acts)):
            if j + 1 < len(acts):
                weight_copy(j + 1).start()
            weight_copy(j).wait()

            def cast(k, carry, j=j):
                r0 = pl.multiple_of(k * CAST_RB, CAST_RB)
                w_ref[pl.ds(r0, CAST_RB), j * d:(j + 1) * d] = (
                    stage_ref[j % 2, pl.ds(r0, CAST_RB), :].astype(BF16))
                return carry
            lax.fori_loop(0, d // CAST_RB, cast, 0)

        def body(k, carry):
            _norm_rows(x0_ref, mod0_ref, g, h_ref, 0, pl.multiple_of(k * NORM_RB, NORM_RB))
            return carry
        lax.fori_loop(0, tm // NORM_RB, body, 0)

    cur = i % 2
    for k in range(tm // NORM_RB):
        _norm_rows(xn_ref, modn_ref, g, h_ref, 1 - cur, k * NORM_RB)
    for j, act in enumerate(acts):
        cols = slice(j * d, (j + 1) * d)
        acc = jnp.dot(h_ref[cur], w_ref[:, cols], preferred_element_type=F32)
        o_ref[:, cols] = _ACTIVATIONS[act](acc).astype(BF16)


def _projection(x2, mod3, g_pre, w_in, seq, acts):
    n, d = x2.shape
    n_tiles = n // PROJ_TM
    nxt = lambda i: jnp.minimum(i + 1, n_tiles - 1)
    return pl.pallas_call(
        functools.partial(_proj_kernel, acts),
        grid=(n_tiles,),
        in_specs=[
            pl.BlockSpec((PROJ_TM, d), lambda i: (0, 0), pipeline_mode=pl.Buffered(1)),
            pl.BlockSpec((None, 1, 3 * d), lambda i: (0, 0, 0)),
            pl.BlockSpec((PROJ_TM, d), lambda i: (nxt(i), 0)),
            pl.BlockSpec((None, 1, 3 * d), lambda i: ((nxt(i) * PROJ_TM) // seq, 0, 0)),
            pl.BlockSpec((1, d), lambda i: (0, 0)),
            pl.BlockSpec(memory_space=pl.ANY),
        ],
        out_specs=pl.BlockSpec((PROJ_TM, w_in.shape[1]), lambda i: (i, 0)),
        out_shape=jax.ShapeDtypeStruct((n, w_in.shape[1]), BF16),
        scratch_shapes=[
            pltpu.VMEM((2, PROJ_TM, d), BF16),
            pltpu.VMEM(w_in.shape, BF16),
            pltpu.VMEM((2, d, d), w_in.dtype),
            pltpu.SemaphoreType.DMA((2,)),
        ],
        compiler_params=pltpu.CompilerParams(
            dimension_semantics=("arbitrary",), vmem_limit_bytes=VMEM_LIMIT),
        name="proj",
    )(x2, mod3, x2, mod3, g_pre, w_in)


def _rglru_kernel(rev, xa_ref, prev_ref, next_ref, shift_ref, cw_ref, cb_ref, wblk_ref, bg_ref,
                  lam_ref, o_ref, wg_ref, sa_ref, su_ref, sh_ref, carry_ref):
    i = pl.program_id(0)
    n_chunks = pl.num_programs(0) - 1
    nb, tc, d = xa_ref.shape
    groups = tc // SUBLANES
    n_tiles = d // MXU_TILE
    n_gates, n_blk, bw, _ = wblk_ref.shape
    per_tile = MXU_TILE // bw
    ig = jnp.minimum(i, n_chunks - 1)
    ci = n_chunks - 1 - ig if rev else ig
    cur = i % 2
    old = 1 - cur

    @pl.when(i == 0)
    def _():
        carry_ref[...] = jnp.zeros_like(carry_ref)
        sa_ref[1] = jnp.zeros(sa_ref.shape[1:], F32)
        su_ref[1] = jnp.zeros(su_ref.shape[1:], F32)
        wg_ref[...] = jnp.zeros_like(wg_ref)
        for gate in range(n_gates):
            for blk in range(n_blk):
                off = (blk % per_tile) * bw
                wg_ref[gate, blk // per_tile, off:off + bw, off:off + bw] = (
                    wblk_ref[gate, blk].astype(BF16))

    has_prev = ci > 0
    has_next = ci < n_chunks - 1
    cwh = 0.5 * cw_ref[...]
    cbh = 0.5 * cb_ref[...]
    lam = lam_ref[...]
    softplus_neg_lam = jnp.maximum(-lam, 0.0) + jnp.log1p(jnp.exp(-jnp.abs(lam)))
    c8l = (-0.5 * RG_C * LOG2_E) * softplus_neg_lam
    brh = 0.5 * bg_ref[0]
    bih = 0.5 * bg_ref[1]
    shift_m = shift_ref[...]
    taps = [k for k in range(CONV_W) if k != CONV_W // 2]

    def scan_row(b):
        h = carry_ref[b]
        for gg in range(groups):
            g = groups - 1 - gg if rev else gg
            for jj in range(SUBLANES):
                j = SUBLANES - 1 - jj if rev else jj
                tok = pl.ds(j, SUBLANES, stride=SUBLANES)
                h = sa_ref[old, b, g, tok, :] * h + su_ref[old, b, g, tok, :]
                sh_ref[b, g, tok, :] = h
            if gg % 2 == 1:
                gp = g // 2
                r0 = gp * 2 * SUBLANES
                for cblk in range(d // LANES):
                    blk = slice(cblk * SUBLANES, (cblk + 1) * SUBLANES)
                    rows16 = jnp.concatenate(
                        [sh_ref[b, 2 * gp, blk, :], sh_ref[b, 2 * gp + 1, blk, :]], axis=0)
                    o_ref[b, pl.ds(r0, 2 * SUBLANES), cblk * LANES:(cblk + 1) * LANES] = (
                        rows16.astype(o_ref.dtype))
        carry_ref[b] = h

    def gates(b):
        xb = xa_ref[b]
        halo = jnp.concatenate([
            jnp.where(has_prev, prev_ref[b], jnp.zeros_like(prev_ref[b])),
            jnp.where(has_next, next_ref[b], jnp.zeros_like(next_ref[b]))], axis=0)
        sh = jnp.dot(shift_m, jnp.concatenate([xb, halo], axis=0), preferred_element_type=F32)
        xh = cbh + cwh[CONV_W // 2:CONV_W // 2 + 1] * xb.astype(F32)
        for n, k in enumerate(taps):
            xh = xh + cwh[k:k + 1] * sh[n * tc:(n + 1) * tc]
        xhb = xh.astype(BF16)
        for k in range(n_tiles):
            sl = slice(k * MXU_TILE, (k + 1) * MXU_TILE)
            xk = xhb[:, sl]
            tr = jnp.tanh(jnp.dot(xk, wg_ref[0, k], preferred_element_type=F32) + brh[:, sl])
            ti = jnp.tanh(jnp.dot(xk, wg_ref[1, k], preferred_element_type=F32) + bih[:, sl])
            a = jnp.exp2(c8l[:, sl] * (tr + 1.0))
            z = 1.0 - a * a
            u = (z * lax.rsqrt(jnp.maximum(z, RSQRT_FLOOR))) * ((ti + 1.0) * xh[:, sl])
            for c2 in range(MXU_TILE // LANES):
                cblk = k * (MXU_TILE // LANES) + c2
                rows = slice(cblk * SUBLANES, (cblk + 1) * SUBLANES)
                lanes = slice(c2 * LANES, (c2 + 1) * LANES)
                sa_ref[cur, b, :, rows, :] = a[:, lanes].reshape(groups, SUBLANES, LANES)
                su_ref[cur, b, :, rows, :] = u[:, lanes].reshape(groups, SUBLANES, LANES)

    def body(b, carry):
        scan_row(b)
        gates(b)
        return carry

    lax.fori_loop(0, nb, body, 0)


def _shift_matrix(tc):
    taps = [k for k in range(CONV_W) if k != CONV_W // 2]
    m = np.zeros((len(taps) * tc, tc + 2 * SUBLANES), np.float32)
    for n, k in enumerate(taps):
        for t in range(tc):
            s = t - CONV_W // 2 + k
            if s < 0:
                col = tc + SUBLANES + s
            elif s >= tc:
                col = tc + SUBLANES + (s - tc)
            else:
                col = s
            m[n * tc + t, col] = 1.0
    return m


def _rglru(proj3, conv_w, conv_b, wblk, bg, lam, rev):
    nb, seq, _ = proj3.shape
    d = conv_w.shape[1]
    n_chunks = seq // SCAN_TC
    n_groups = seq // SUBLANES
    gpc = SCAN_TC // SUBLANES
    proj4 = proj3.reshape(nb, n_groups, SUBLANES, proj3.shape[2])

    def chunk(i):
        return n_chunks - 1 - i if rev else i

    def gated(i):
        return chunk(jnp.minimum(i, n_chunks - 1))

    def scanned(i):
        return chunk(jnp.maximum(i - 1, 0))

    n_tiles = d // MXU_TILE
    n_gates = wblk.shape[0]
    shift_m = jnp.asarray(_shift_matrix(SCAN_TC), BF16)
    whole = lambda a: pl.BlockSpec(a.shape, lambda i: (0,) * a.ndim)
    return pl.pallas_call(
        functools.partial(_rglru_kernel, rev),
        grid=(n_chunks + 1,),
        in_specs=[
            pl.BlockSpec((nb, SCAN_TC, d), lambda i: (0, gated(i), 0)),
            pl.BlockSpec((nb, None, SUBLANES, d),
                         lambda i: (0, jnp.maximum(gated(i) * gpc - 1, 0), 0, 0)),
            pl.BlockSpec((nb, None, SUBLANES, d),
                         lambda i: (0, jnp.minimum((gated(i) + 1) * gpc, n_groups - 1), 0, 0)),
            whole(shift_m), whole(conv_w), whole(conv_b), whole(wblk), whole(bg), whole(lam),
        ],
        out_specs=pl.BlockSpec((nb, SCAN_TC, d), lambda i: (0, scanned(i), 0)),
        out_shape=jax.ShapeDtypeStruct((nb, seq, d), BF16),
        scratch_shapes=[
            pltpu.VMEM((n_gates, n_tiles, MXU_TILE, MXU_TILE), BF16),
            pltpu.VMEM((2, nb, gpc, d // LANES * SUBLANES, LANES), F32),
            pltpu.VMEM((2, nb, gpc, d // LANES * SUBLANES, LANES), F32),
            pltpu.VMEM((nb, gpc, d // LANES * SUBLANES, LANES), F32),
            pltpu.VMEM((nb, SUBLANES, LANES), F32),
        ],
        compiler_params=pltpu.CompilerParams(
            dimension_semantics=("arbitrary",), vmem_limit_bytes=VMEM_LIMIT),
        name="rglru_bwd" if rev else "rglru_fwd",
    )(proj3, proj4, proj4, shift_m, conv_w, conv_b, wblk, bg, lam)


def _natten_kernel(q_ref, k_ref, v_ref, *rest):
    bias_refs, o_ref = rest[:-1], rest[-1]
    step = pl.program_id(1)
    rows = pl.num_programs(1) * NAT_ROWS
    gw = GRID_W
    d = q_ref.shape[1]
    win_r = min(WIN_R, rows)
    lane = lax.broadcasted_iota(jnp.int32, (gw, LANES), 1)
    lo = lane < HEAD_DIM
    zero = jnp.zeros((gw, LANES), BF16)
    n_pairs = d // LANES
    wins = []
    scores = []
    for rr in range(NAT_ROWS):
        r = step * NAT_ROWS + rr
        r0 = jnp.clip(r - win_r // 2, 0, rows - win_r)
        k0 = pl.multiple_of(r0 * gw, gw)
        wins.append(k0)
        for p in range(n_pairs):
            sl = slice(p * LANES, (p + 1) * LANES)
            qp = q_ref[rr * gw:(rr + 1) * gw, sl]
            qs = jnp.concatenate([jnp.where(lo, qp, zero), jnp.where(lo, zero, qp)], axis=0)
            scores.append(lax.dot_general(qs, k_ref[pl.ds(k0, win_r * gw), sl],
                                          (((1,), (1,)), ((), ())), preferred_element_type=F32))
    probs = []
    for rr in range(NAT_ROWS):
        for p in range(n_pairs):
            s = scores[rr * n_pairs + p] + bias_refs[rr][p]
            m = jnp.max(s, axis=-1, keepdims=True)
            probs.append(jnp.exp2(s - m).astype(BF16))
    ones = jnp.ones((win_r * gw, LANES), BF16)
    for rr in range(NAT_ROWS):
        for p in range(n_pairs):
            sl = slice(p * LANES, (p + 1) * LANES)
            v_ext = jnp.concatenate([v_ref[pl.ds(wins[rr], win_r * gw), sl], ones], axis=1)
            pv = jnp.dot(probs[rr * n_pairs + p], v_ext, preferred_element_type=F32)
            o = pv[:, :LANES] / pv[:, LANES:]
            o_ref[rr * gw:(rr + 1) * gw, sl] = jnp.where(lo, o[:gw], o[gw:]).astype(BF16)


def _natten(proj3, bias_tab, q_col, k_col, v_col):
    nb, seq, _ = proj3.shape
    d = N_HEADS * HEAD_DIM
    rows = seq // GRID_W
    win_r = min(WIN_R, rows)
    n_pairs = d // LANES

    def pattern(r):
        r0 = jnp.clip(r - win_r // 2, 0, rows - win_r)
        return r0 - r + (WIN_R - 1)

    def bias_spec(rr):
        return pl.BlockSpec((None, n_pairs, 2 * GRID_W, win_r * GRID_W),
                            lambda b, i: (pattern(i * NAT_ROWS + rr), 0, 0, 0))

    return pl.pallas_call(
        _natten_kernel,
        grid=(nb, rows // NAT_ROWS),
        in_specs=[
            pl.BlockSpec((None, NAT_ROWS * GRID_W, d), lambda b, i: (b, i, q_col)),
            pl.BlockSpec((None, seq, d), lambda b, i: (b, 0, k_col)),
            pl.BlockSpec((None, seq, d), lambda b, i: (b, 0, v_col)),
        ] + [bias_spec(rr) for rr in range(NAT_ROWS)],
        out_specs=pl.BlockSpec((None, NAT_ROWS * GRID_W, d), lambda b, i: (b, i, 0)),
        out_shape=jax.ShapeDtypeStruct((nb, seq, d), BF16),
        compiler_params=pltpu.CompilerParams(
            dimension_semantics=("parallel", "arbitrary"), vmem_limit_bytes=VMEM_LIMIT),
        name="natten",
    )(proj3, proj3, proj3, *([bias_tab] * NAT_ROWS))


def _bias_kernel(rp_ref, o_ref):
    d0 = pl.program_id(0)
    n_pairs, _, keys = o_ref.shape
    gw = GRID_W
    q_idx = lax.broadcasted_iota(jnp.int32, (gw, LANES), 0)
    lane = lax.broadcasted_iota(jnp.int32, (gw, LANES), 1)
    col = jnp.where(lane < gw, lane, lane - gw)
    c0 = jnp.clip(q_idx - WIN_C // 2, 0, gw - WIN_C)
    valid = (col >= c0) & (col < c0 + WIN_C)
    first = lane < gw
    for pair in range(n_pairs):
        for hh in range(2):
            h = 2 * pair + hh
            for t in range(keys // LANES):
                va = jnp.broadcast_to(rp_ref[h, pl.ds(d0 + 2 * t, 1), :], (gw, LANES))
                vb = jnp.broadcast_to(rp_ref[h, pl.ds(d0 + 2 * t + 1, 1), :], (gw, LANES))
                ta = pltpu.roll(va, 0, 1, stride=1, stride_axis=0)
                tb = pltpu.roll(vb, gw, 1, stride=1, stride_axis=0)
                tile = jnp.where(valid, LOG2_E * jnp.where(first, ta, tb), MASK_BIAS)
                o_ref[pair, hh * gw:(hh + 1) * gw, t * LANES:(t + 1) * LANES] = tile


def _bias_table(rpb, rows):
    win_r = min(WIN_R, rows)
    assert 2 * GRID_W == LANES and win_r % 2 == 0
    nh, nd, _ = rpb.shape
    rp = jnp.concatenate([rpb[:, :, WIN_C - 1:],
                          jnp.zeros((nh, nd, LANES - (2 * WIN_C - 1)), rpb.dtype),
                          rpb[:, :, :WIN_C - 1]], axis=-1).astype(F32)
    n_pat = WIN_R
    return pl.pallas_call(
        _bias_kernel,
        grid=(n_pat,),
        in_specs=[pl.BlockSpec(rp.shape, lambda p: (0, 0, 0))],
        out_specs=pl.BlockSpec((None, nh // 2, 2 * GRID_W, win_r * GRID_W), lambda p: (p, 0, 0, 0)),
        out_shape=jax.ShapeDtypeStruct((n_pat, nh // 2, 2 * GRID_W, win_r * GRID_W), F32),
        compiler_params=pltpu.CompilerParams(
            dimension_semantics=("parallel",), vmem_limit_bytes=VMEM_LIMIT),
        name="bias_table",
    )(rp)


def _tail_kernel(final, x_ref, mod_ref, hf_ref, hb_ref, za_ref, zb_ref, ga_ref, gb_ref, att_ref,
                 wpa_ref, wpb_ref, wo_ref, gf_ref, o_ref, w_scr):
    d = x_ref.shape[1]

    @pl.when(pl.program_id(0) == 0)
    def _():
        for n, w_ref in enumerate((wpa_ref, wpb_ref, wo_ref)):
            w_scr[n] = w_ref[...].astype(BF16)

    for s in range(x_ref.shape[0] // TAIL_SUB):
        rows = slice(s * TAIL_SUB, (s + 1) * TAIL_SUB)
        h = hf_ref[rows, :].astype(F32) + hb_ref[rows, :].astype(F32)
        ya = (h * za_ref[rows, :].astype(F32)).astype(BF16)
        ya = jnp.dot(ya, w_scr[0], preferred_element_type=F32)
        yb = (att_ref[rows, :].astype(F32) * zb_ref[rows, :].astype(F32)).astype(BF16)
        yb = jnp.dot(yb, w_scr[1], preferred_element_type=F32)
        merged = ga_ref[rows, :].astype(F32) * ya + gb_ref[rows, :].astype(F32) * yb
        res = jnp.dot(merged.astype(BF16), w_scr[2], preferred_element_type=F32)
        xo = x_ref[rows, :] + mod_ref[:, 2 * d:3 * d] * res
        if final:
            ms = jnp.mean(xo * xo, axis=-1, keepdims=True)
            xo = xo * lax.rsqrt(ms + EPS) * gf_ref[...]
        o_ref[rows, :] = xo


def _tail(x2, mod3, hf, hb, proj, att2, w_pa, w_pb, w_o, g_final, seq, cols, final):
    n, d = x2.shape
    za_col, zb_col, ga_col, gb_col = cols
    tok = lambda i: (i, 0)
    wspec = pl.BlockSpec((d, d), lambda i: (0, 0), pipeline_mode=pl.Buffered(1))
    return pl.pallas_call(
        functools.partial(_tail_kernel, final),
        grid=(n // TAIL_TM,),
        in_specs=[
            pl.BlockSpec((TAIL_TM, d), tok),
            pl.BlockSpec((None, 1, 3 * d), lambda i: ((i * TAIL_TM) // seq, 0, 0)),
            pl.BlockSpec((TAIL_TM, d), tok),
            pl.BlockSpec((TAIL_TM, d), tok),
            pl.BlockSpec((TAIL_TM, d), lambda i: (i, za_col)),
            pl.BlockSpec((TAIL_TM, d), lambda i: (i, zb_col)),
            pl.BlockSpec((TAIL_TM, d), lambda i: (i, ga_col)),
            pl.BlockSpec((TAIL_TM, d), lambda i: (i, gb_col)),
            pl.BlockSpec((TAIL_TM, d), tok),
            wspec, wspec, wspec,
            pl.BlockSpec((1, d), lambda i: (0, 0)),
        ],
        out_specs=pl.BlockSpec((TAIL_TM, d), tok),
        out_shape=jax.ShapeDtypeStruct((n, d), F32),
        scratch_shapes=[pltpu.VMEM((3, d, d), BF16)],
        compiler_params=pltpu.CompilerParams(
            dimension_semantics=("arbitrary",), vmem_limit_bytes=VMEM_LIMIT),
        name="tail",
    )(x2, mod3, hf, hb, proj, proj, proj, proj, att2, w_pa, w_pb, w_o, g_final)


def kernel(x, c, g_pre, w_c, b_c, w_in, conv_w, conv_b, w_r_f, b_r_f, w_i_f, b_i_f, lam_f,
           w_r_b, b_r_b, w_i_b, b_i_b, lam_b, rpb, w_pa, w_pb, w_o, g_final):
    nb, seq, d = x.shape
    depth = g_pre.shape[0]
    n = nb * seq
    rows = seq // GRID_W
    groups = w_in.shape[2] // d
    xa_col, za_col, q_col, k_col, v_col, zb_col, ga_col, gb_col = range(groups)
    assert xa_col == 0
    acts = ["id"] * groups
    acts[za_col] = acts[zb_col] = "silu"
    acts[ga_col] = acts[gb_col] = "sigmoid"
    acts[q_col] = "qscale"

    c_pad = jnp.zeros((MOD_ROWS, d), F32).at[:nb].set(c)
    x2 = x.reshape(n, d)
    for l in range(depth):
        mod = _modulation(c_pad, w_c[l], b_c[l].reshape(1, 3 * d))
        mod3 = mod[:nb].reshape(nb, 1, 3 * d)

        proj = _projection(x2, mod3, g_pre[l].reshape(1, d), w_in[l], seq, tuple(acts))
        proj3 = proj.reshape(nb, seq, groups * d)

        cw, cb = conv_w[l], conv_b[l].reshape(1, d)
        hf = _rglru(proj3, cw, cb, jnp.stack([w_r_f[l], w_i_f[l]]),
                    jnp.stack([b_r_f[l], b_i_f[l]]).reshape(2, 1, d), lam_f[l].reshape(1, d), False)
        hb = _rglru(proj3, cw, cb, jnp.stack([w_r_b[l], w_i_b[l]]),
                    jnp.stack([b_r_b[l], b_i_b[l]]).reshape(2, 1, d), lam_b[l].reshape(1, d), True)

        att = _natten(proj3, _bias_table(rpb[l], rows), q_col, k_col, v_col)

        x2 = _tail(x2, mod3, hf.reshape(n, d), hb.reshape(n, d), proj, att.reshape(n, d),
                   w_pa[l], w_pb[l], w_o[l],
                   g_final.reshape(1, d), seq, (za_col, zb_col, ga_col, gb_col),
                   final=(l == depth - 1))
    return x2.reshape(nb, seq, d)
```

```python
import functools

import jax
import jax.numpy as jnp
import numpy as np
from jax import lax
from jax.experimental import pallas as pl
from jax.experimental.pallas import tpu as pltpu

F32 = jnp.float32
BF16 = jnp.bfloat16

SUBLANES = 8
LANES = 128
MXU_TILE = 256

GRID_W = 64
A_BLOCKS = 16
CONV_W = 4
RG_C = 8.0
N_HEADS = 16
HEAD_DIM = 64
WIN_R = 8
WIN_C = 16
EPS = 1e-6
MASK_BIAS = -1e30
LOG2_E = 1.4426950408889634
RSQRT_FLOOR = 1e-30

PROJ_TM = 512
NORM_RB = 64
CAST_RB = 128
SCAN_TC = 128
TAIL_TM = 512
TAIL_SUB = 256
NAT_ROWS = 2
MOD_ROWS = 16

VMEM_LIMIT = 56 * 1024 * 1024


def _sigmoid(x):
    return 0.5 * jnp.tanh(0.5 * x) + 0.5


def _silu(x):
    return x * _sigmoid(x)


_ACTIVATIONS = {
    "id": lambda v: v,
    "silu": _silu,
    "sigmoid": _sigmoid,
    "qscale": lambda v: v * (HEAD_DIM ** -0.5 * LOG2_E),
}


def _mod_kernel(c_ref, w_ref, b_ref, o_ref):
    sc = _silu(c_ref[...]).astype(BF16)
    w = w_ref[...].astype(BF16)
    o_ref[...] = jnp.dot(sc, w, preferred_element_type=F32) + b_ref[...]


def _modulation(c_pad, w_c, b_c):
    d = w_c.shape[0]
    return pl.pallas_call(
        _mod_kernel,
        grid=(3,),
        in_specs=[
            pl.BlockSpec((MOD_ROWS, d), lambda j: (0, 0)),
            pl.BlockSpec((d, d), lambda j: (0, j)),
            pl.BlockSpec((1, d), lambda j: (0, j)),
        ],
        out_specs=pl.BlockSpec((MOD_ROWS, d), lambda j: (0, j)),
        out_shape=jax.ShapeDtypeStruct((MOD_ROWS, 3 * d), F32),
        compiler_params=pltpu.CompilerParams(
            dimension_semantics=("parallel",), vmem_limit_bytes=VMEM_LIMIT),
        name="mod",
    )(c_pad, w_c, b_c)


def _norm_rows(x_ref, mod_ref, g, h_ref, slot, r0):
    d = x_ref.shape[1]
    x = x_ref[pl.ds(r0, NORM_RB), :]
    ms = jnp.mean(x * x, axis=-1, keepdims=True)
    y = x * lax.rsqrt(ms + EPS) * g
    h = y * (1.0 + mod_ref[:, d:2 * d]) + mod_ref[:, 0:d]
    h_ref[slot, pl.ds(r0, NORM_RB), :] = h.astype(BF16)


def _proj_kernel(acts, x0_ref, mod0_ref, xn_ref, modn_ref, g_ref, w_hbm, o_ref,
                 h_ref, w_ref, stage_ref, sem):
    i = pl.program_id(0)
    tm, d = xn_ref.shape
    g = g_ref[...]

    def weight_copy(j):
        return pltpu.make_async_copy(w_hbm.at[:, pl.ds(j * d, d)], stage_ref.at[j % 2],
                                     sem.at[j % 2])

    @pl.when(i == 0)
    def _():
        weight_copy(0).start()
        for j in range(len(acts)):
            if j + 1 < len(acts):
                weight_copy(j + 1).start()
            weight_copy(j).wait()

            def cast(k, carry, j=j):
                r0 = pl.multiple_of(k * CAST_RB, CAST_RB)
                w_ref[pl.ds(r0, CAST_RB), j * d:(j + 1) * d] = (
                    stage_ref[j % 2, pl.ds(r0, CAST_RB), :].astype(BF16))
                return carry
            lax.fori_loop(0, d // CAST_RB, cast, 0)

        def body(k, carry):
            _norm_rows(x0_ref, mod0_ref, g, h_ref, 0, pl.multiple_of(k * NORM_RB, NORM_RB))
            return carry
        lax.fori_loop(0, tm // NORM_RB, body, 0)

    cur = i % 2
    for k in range(tm // NORM_RB):
        _norm_rows(xn_ref, modn_ref, g, h_ref, 1 - cur, k * NORM_RB)
    for j, act in enumerate(acts):
        cols = slice(j * d, (j + 1) * d)
        acc = jnp.dot(h_ref[cur], w_ref[:, cols], preferred_element_type=F32)
        o_ref[:, cols] = _ACTIVATIONS[act](acc).astype(BF16)


def _projection(x2, mod3, g_pre, w_in, seq, acts):
    n, d = x2.shape
    n_tiles = n // PROJ_TM
    nxt = lambda i: jnp.minimum(i + 1, n_tiles - 1)
    return pl.pallas_call(
        functools.partial(_proj_kernel, acts),
        grid=(n_tiles,),
        in_specs=[
            pl.BlockSpec((PROJ_TM, d), lambda i: (0, 0), pipeline_mode=pl.Buffered(1)),
            pl.BlockSpec((None, 1, 3 * d), lambda i: (0, 0, 0)),
            pl.BlockSpec((PROJ_TM, d), lambda i: (nxt(i), 0)),
            pl.BlockSpec((None, 1, 3 * d), lambda i: ((nxt(i) * PROJ_TM) // seq, 0, 0)),
            pl.BlockSpec((1, d), lambda i: (0, 0)),
            pl.BlockSpec(memory_space=pl.ANY),
        ],
        out_specs=pl.BlockSpec((PROJ_TM, w_in.shape[1]), lambda i: (i, 0)),
        out_shape=jax.ShapeDtypeStruct((n, w_in.shape[1]), BF16),
        scratch_shapes=[
            pltpu.VMEM((2, PROJ_TM, d), BF16),
            pltpu.VMEM(w_in.shape, BF16),
            pltpu.VMEM((2, d, d), w_in.dtype),
            pltpu.SemaphoreType.DMA((2,)),
        ],
        compiler_params=pltpu.CompilerParams(
            dimension_semantics=("arbitrary",), vmem_limit_bytes=VMEM_LIMIT),
        name="proj",
    )(x2, mod3, x2, mod3, g_pre, w_in)


def _rglru_kernel(rev, merge, xa_ref, prev_ref, next_ref, shift_ref, cw_ref, cb_ref, wblk_ref,
                  bg_ref, lam_ref, *rest):
    if merge:
        ho_ref, za_ref = rest[:2]
        rest = rest[2:]
    o_ref, wg_ref, sa_ref, su_ref, sh_ref, carry_ref = rest
    i = pl.program_id(0)
    n_chunks = pl.num_programs(0) - 1
    nb, tc, d = xa_ref.shape
    groups = tc // SUBLANES
    n_tiles = d // MXU_TILE
    n_gates, n_blk, bw, _ = wblk_ref.shape
    per_tile = MXU_TILE // bw
    ig = jnp.minimum(i, n_chunks - 1)
    ci = n_chunks - 1 - ig if rev else ig
    cur = i % 2
    old = 1 - cur

    @pl.when(i == 0)
    def _():
        carry_ref[...] = jnp.zeros_like(carry_ref)
        sa_ref[1] = jnp.zeros(sa_ref.shape[1:], F32)
        su_ref[1] = jnp.zeros(su_ref.shape[1:], F32)
        wg_ref[...] = jnp.zeros_like(wg_ref)
        for gate in range(n_gates):
            for blk in range(n_blk):
                off = (blk % per_tile) * bw
                wg_ref[gate, blk // per_tile, off:off + bw, off:off + bw] = (
                    wblk_ref[gate, blk].astype(BF16))

    has_prev = ci > 0
    has_next = ci < n_chunks - 1
    cwh = 0.5 * cw_ref[...]
    cbh = 0.5 * cb_ref[...]
    lam = lam_ref[...]
    softplus_neg_lam = jnp.maximum(-lam, 0.0) + jnp.log1p(jnp.exp(-jnp.abs(lam)))
    c8l = (-0.5 * RG_C * LOG2_E) * softplus_neg_lam
    brh = 0.5 * bg_ref[0]
    bih = 0.5 * bg_ref[1]
    shift_m = shift_ref[...]
    taps = [k for k in range(CONV_W) if k != CONV_W // 2]

    def scan_row(b):
        h = carry_ref[b]
        for gg in range(groups):
            g = groups - 1 - gg if rev else gg
            for jj in range(SUBLANES):
                j = SUBLANES - 1 - jj if rev else jj
                tok = pl.ds(j, SUBLANES, stride=SUBLANES)
                h = sa_ref[old, b, g, tok, :] * h + su_ref[old, b, g, tok, :]
                sh_ref[b, g, tok, :] = h
            if gg % 2 == 1:
                gp = g // 2
                r0 = gp * 2 * SUBLANES
                for cblk in range(d // LANES):
                    blk = slice(cblk * SUBLANES, (cblk + 1) * SUBLANES)
                    rows16 = jnp.concatenate(
                        [sh_ref[b, 2 * gp, blk, :], sh_ref[b, 2 * gp + 1, blk, :]], axis=0)
                    tile = (b, pl.ds(r0, 2 * SUBLANES), slice(cblk * LANES, (cblk + 1) * LANES))
                    if merge:
                        rows16 = (rows16 + ho_ref[tile].astype(F32)) * za_ref[tile].astype(F32)
                    o_ref[tile] = rows16.astype(o_ref.dtype)
        carry_ref[b] = h

    def gates(b):
        xb = xa_ref[b]
        halo = jnp.concatenate([
            jnp.where(has_prev, prev_ref[b], jnp.zeros_like(prev_ref[b])),
            jnp.where(has_next, next_ref[b], jnp.zeros_like(next_ref[b]))], axis=0)
        sh = jnp.dot(shift_m, jnp.concatenate([xb, halo], axis=0), preferred_element_type=F32)
        xh = cbh + cwh[CONV_W // 2:CONV_W // 2 + 1] * xb.astype(F32)
        for n, k in enumerate(taps):
            xh = xh + cwh[k:k + 1] * sh[n * tc:(n + 1) * tc]
        xhb = xh.astype(BF16)
        for k in range(n_tiles):
            sl = slice(k * MXU_TILE, (k + 1) * MXU_TILE)
            xk = xhb[:, sl]
            tr = jnp.tanh(jnp.dot(xk, wg_ref[0, k], preferred_element_type=F32) + brh[:, sl])
            ti = jnp.tanh(jnp.dot(xk, wg_ref[1, k], preferred_element_type=F32) + bih[:, sl])
            a = jnp.exp2(c8l[:, sl] * (tr + 1.0))
            z = 1.0 - a * a
            u = (z * lax.rsqrt(jnp.maximum(z, RSQRT_FLOOR))) * ((ti + 1.0) * xh[:, sl])
            for c2 in range(MXU_TILE // LANES):
                cblk = k * (MXU_TILE // LANES) + c2
                rows = slice(cblk * SUBLANES, (cblk + 1) * SUBLANES)
                lanes = slice(c2 * LANES, (c2 + 1) * LANES)
                sa_ref[cur, b, :, rows, :] = a[:, lanes].reshape(groups, SUBLANES, LANES)
                su_ref[cur, b, :, rows, :] = u[:, lanes].reshape(groups, SUBLANES, LANES)

    def body(b, carry):
        scan_row(b)
        gates(b)
        return carry

    lax.fori_loop(0, nb, body, 0)


def _shift_matrix(tc):
    taps = [k for k in range(CONV_W) if k != CONV_W // 2]
    m = np.zeros((len(taps) * tc, tc + 2 * SUBLANES), np.float32)
    for n, k in enumerate(taps):
        for t in range(tc):
            s = t - CONV_W // 2 + k
            if s < 0:
                col = tc + SUBLANES + s
            elif s >= tc:
                col = tc + SUBLANES + (s - tc)
            else:
                col = s
            m[n * tc + t, col] = 1.0
    return m


def _rglru(proj3, conv_w, conv_b, wblk, bg, lam, rev, h_other=None, za_col=None):
    nb, seq, _ = proj3.shape
    merge = h_other is not None
    d = conv_w.shape[1]
    n_chunks = seq // SCAN_TC
    n_groups = seq // SUBLANES
    gpc = SCAN_TC // SUBLANES
    proj4 = proj3.reshape(nb, n_groups, SUBLANES, proj3.shape[2])

    def chunk(i):
        return n_chunks - 1 - i if rev else i

    def gated(i):
        return chunk(jnp.minimum(i, n_chunks - 1))

    def scanned(i):
        return chunk(jnp.maximum(i - 1, 0))

    n_tiles = d // MXU_TILE
    n_gates = wblk.shape[0]
    shift_m = jnp.asarray(_shift_matrix(SCAN_TC), BF16)
    whole = lambda a: pl.BlockSpec(a.shape, lambda i: (0,) * a.ndim)
    merge_specs, merge_args = [], []
    if merge:
        merge_specs = [pl.BlockSpec((nb, SCAN_TC, d), lambda i: (0, scanned(i), 0)),
                       pl.BlockSpec((nb, SCAN_TC, d), lambda i: (0, scanned(i), za_col))]
        merge_args = [h_other, proj3]
    return pl.pallas_call(
        functools.partial(_rglru_kernel, rev, merge),
        grid=(n_chunks + 1,),
        in_specs=[
            pl.BlockSpec((nb, SCAN_TC, d), lambda i: (0, gated(i), 0)),
            pl.BlockSpec((nb, None, SUBLANES, d),
                         lambda i: (0, jnp.maximum(gated(i) * gpc - 1, 0), 0, 0)),
            pl.BlockSpec((nb, None, SUBLANES, d),
                         lambda i: (0, jnp.minimum((gated(i) + 1) * gpc, n_groups - 1), 0, 0)),
            whole(shift_m), whole(conv_w), whole(conv_b), whole(wblk), whole(bg), whole(lam),
        ] + merge_specs,
        out_specs=pl.BlockSpec((nb, SCAN_TC, d), lambda i: (0, scanned(i), 0)),
        out_shape=jax.ShapeDtypeStruct((nb, seq, d), BF16),
        scratch_shapes=[
            pltpu.VMEM((n_gates, n_tiles, MXU_TILE, MXU_TILE), BF16),
            pltpu.VMEM((2, nb, gpc, d // LANES * SUBLANES, LANES), F32),
            pltpu.VMEM((2, nb, gpc, d // LANES * SUBLANES, LANES), F32),
            pltpu.VMEM((nb, gpc, d // LANES * SUBLANES, LANES), F32),
            pltpu.VMEM((nb, SUBLANES, LANES), F32),
        ],
        compiler_params=pltpu.CompilerParams(
            dimension_semantics=("arbitrary",), vmem_limit_bytes=VMEM_LIMIT),
        name="rglru_bwd" if rev else "rglru_fwd",
    )(proj3, proj4, proj4, shift_m, conv_w, conv_b, wblk, bg, lam, *merge_args)


def _natten_kernel(q_ref, k_ref, v_ref, *rest):
    zb_ref, bias_refs, o_ref = rest[0], rest[1:-1], rest[-1]
    step = pl.program_id(1)
    rows = pl.num_programs(1) * NAT_ROWS
    gw = GRID_W
    d = q_ref.shape[1]
    win_r = min(WIN_R, rows)
    lane = lax.broadcasted_iota(jnp.int32, (gw, LANES), 1)
    lo = lane < HEAD_DIM
    zero = jnp.zeros((gw, LANES), BF16)
    n_pairs = d // LANES
    wins = []
    scores = []
    for rr in range(NAT_ROWS):
        r = step * NAT_ROWS + rr
        r0 = jnp.clip(r - win_r // 2, 0, rows - win_r)
        k0 = pl.multiple_of(r0 * gw, gw)
        wins.append(k0)
        for p in range(n_pairs):
            sl = slice(p * LANES, (p + 1) * LANES)
            qp = q_ref[rr * gw:(rr + 1) * gw, sl]
            qs = jnp.concatenate([jnp.where(lo, qp, zero), jnp.where(lo, zero, qp)], axis=0)
            scores.append(lax.dot_general(qs, k_ref[pl.ds(k0, win_r * gw), sl],
                                          (((1,), (1,)), ((), ())), preferred_element_type=F32))
    probs = []
    for rr in range(NAT_ROWS):
        for p in range(n_pairs):
            s = scores[rr * n_pairs + p] + bias_refs[rr][p]
            m = jnp.max(s, axis=-1, keepdims=True)
            probs.append(jnp.exp2(s - m).astype(BF16))
    ones = jnp.ones((win_r * gw, LANES), BF16)
    for rr in range(NAT_ROWS):
        for p in range(n_pairs):
            sl = slice(p * LANES, (p + 1) * LANES)
            v_ext = jnp.concatenate([v_ref[pl.ds(wins[rr], win_r * gw), sl], ones], axis=1)
            pv = jnp.dot(probs[rr * n_pairs + p], v_ext, preferred_element_type=F32)
            o = pv[:, :LANES] / pv[:, LANES:]
            o = jnp.where(lo, o[:gw], o[gw:]) * zb_ref[rr * gw:(rr + 1) * gw, sl].astype(F32)
            o_ref[rr * gw:(rr + 1) * gw, sl] = o.astype(BF16)


def _natten(proj3, bias_tab, q_col, k_col, v_col, zb_col):
    nb, seq, _ = proj3.shape
    d = N_HEADS * HEAD_DIM
    rows = seq // GRID_W
    win_r = min(WIN_R, rows)
    n_pairs = d // LANES

    def pattern(r):
        r0 = jnp.clip(r - win_r // 2, 0, rows - win_r)
        return r0 - r + (WIN_R - 1)

    def bias_spec(rr):
        return pl.BlockSpec((None, n_pairs, 2 * GRID_W, win_r * GRID_W),
                            lambda b, i: (pattern(i * NAT_ROWS + rr), 0, 0, 0))

    return pl.pallas_call(
        _natten_kernel,
        grid=(nb, rows // NAT_ROWS),
        in_specs=[
            pl.BlockSpec((None, NAT_ROWS * GRID_W, d), lambda b, i: (b, i, q_col)),
            pl.BlockSpec((None, seq, d), lambda b, i: (b, 0, k_col)),
            pl.BlockSpec((None, seq, d), lambda b, i: (b, 0, v_col)),
            pl.BlockSpec((None, NAT_ROWS * GRID_W, d), lambda b, i: (b, i, zb_col)),
        ] + [bias_spec(rr) for rr in range(NAT_ROWS)],
        out_specs=pl.BlockSpec((None, NAT_ROWS * GRID_W, d), lambda b, i: (b, i, 0)),
        out_shape=jax.ShapeDtypeStruct((nb, seq, d), BF16),
        compiler_params=pltpu.CompilerParams(
            dimension_semantics=("parallel", "arbitrary"), vmem_limit_bytes=VMEM_LIMIT),
        name="natten",
    )(proj3, proj3, proj3, proj3, *([bias_tab] * NAT_ROWS))


def _bias_kernel(rp_ref, o_ref):
    d0 = pl.program_id(0)
    n_pairs, _, keys = o_ref.shape
    gw = GRID_W
    q_idx = lax.broadcasted_iota(jnp.int32, (gw, LANES), 0)
    lane = lax.broadcasted_iota(jnp.int32, (gw, LANES), 1)
    col = jnp.where(lane < gw, lane, lane - gw)
    c0 = jnp.clip(q_idx - WIN_C // 2, 0, gw - WIN_C)
    valid = (col >= c0) & (col < c0 + WIN_C)
    first = lane < gw
    for pair in range(n_pairs):
        for hh in range(2):
            h = 2 * pair + hh
            for t in range(keys // LANES):
                va = jnp.broadcast_to(rp_ref[h, pl.ds(d0 + 2 * t, 1), :], (gw, LANES))
                vb = jnp.broadcast_to(rp_ref[h, pl.ds(d0 + 2 * t + 1, 1), :], (gw, LANES))
                ta = pltpu.roll(va, 0, 1, stride=1, stride_axis=0)
                tb = pltpu.roll(vb, gw, 1, stride=1, stride_axis=0)
                tile = jnp.where(valid, LOG2_E * jnp.where(first, ta, tb), MASK_BIAS)
                o_ref[pair, hh * gw:(hh + 1) * gw, t * LANES:(t + 1) * LANES] = tile


def _bias_table(rpb, rows):
    win_r = min(WIN_R, rows)
    assert 2 * GRID_W == LANES and win_r % 2 == 0
    nh, nd, _ = rpb.shape
    rp = jnp.concatenate([rpb[:, :, WIN_C - 1:],
                          jnp.zeros((nh, nd, LANES - (2 * WIN_C - 1)), rpb.dtype),
                          rpb[:, :, :WIN_C - 1]], axis=-1).astype(F32)
    n_pat = WIN_R
    return pl.pallas_call(
        _bias_kernel,
        grid=(n_pat,),
        in_specs=[pl.BlockSpec(rp.shape, lambda p: (0, 0, 0))],
        out_specs=pl.BlockSpec((None, nh // 2, 2 * GRID_W, win_r * GRID_W), lambda p: (p, 0, 0, 0)),
        out_shape=jax.ShapeDtypeStruct((n_pat, nh // 2, 2 * GRID_W, win_r * GRID_W), F32),
        compiler_params=pltpu.CompilerParams(
            dimension_semantics=("parallel",), vmem_limit_bytes=VMEM_LIMIT),
        name="bias_table",
    )(rp)


def _tail_kernel(final, x_ref, mod_ref, ya_ref, yb_ref, ga_ref, gb_ref,
                 wpa_ref, wpb_ref, wo_ref, gf_ref, o_ref, w_scr):
    d = x_ref.shape[1]

    @pl.when(pl.program_id(0) == 0)
    def _():
        for n, w_ref in enumerate((wpa_ref, wpb_ref, wo_ref)):
            w_scr[n] = w_ref[...].astype(BF16)

    for s in range(x_ref.shape[0] // TAIL_SUB):
        rows = slice(s * TAIL_SUB, (s + 1) * TAIL_SUB)
        ya = jnp.dot(ya_ref[rows, :], w_scr[0], preferred_element_type=F32)
        yb = jnp.dot(yb_ref[rows, :], w_scr[1], preferred_element_type=F32)
        merged = ga_ref[rows, :].astype(F32) * ya + gb_ref[rows, :].astype(F32) * yb
        res = jnp.dot(merged.astype(BF16), w_scr[2], preferred_element_type=F32)
        xo = x_ref[rows, :] + mod_ref[:, 2 * d:3 * d] * res
        if final:
            ms = jnp.mean(xo * xo, axis=-1, keepdims=True)
            xo = xo * lax.rsqrt(ms + EPS) * gf_ref[...]
        o_ref[rows, :] = xo


def _tail(x2, mod3, ya, yb, proj, w_pa, w_pb, w_o, g_final, seq, cols, final):
    n, d = x2.shape
    ga_col, gb_col = cols
    tok = lambda i: (i, 0)
    wspec = pl.BlockSpec((d, d), lambda i: (0, 0), pipeline_mode=pl.Buffered(1))
    return pl.pallas_call(
        functools.partial(_tail_kernel, final),
        grid=(n // TAIL_TM,),
        in_specs=[
            pl.BlockSpec((TAIL_TM, d), tok),
            pl.BlockSpec((None, 1, 3 * d), lambda i: ((i * TAIL_TM) // seq, 0, 0)),
            pl.BlockSpec((TAIL_TM, d), tok),
            pl.BlockSpec((TAIL_TM, d), tok),
            pl.BlockSpec((TAIL_TM, d), lambda i: (i, ga_col)),
            pl.BlockSpec((TAIL_TM, d), lambda i: (i, gb_col)),
            wspec, wspec, wspec,
            pl.BlockSpec((1, d), lambda i: (0, 0)),
        ],
        out_specs=pl.BlockSpec((TAIL_TM, d), tok),
        out_shape=jax.ShapeDtypeStruct((n, d), F32),
        scratch_shapes=[pltpu.VMEM((3, d, d), BF16)],
        compiler_params=pltpu.CompilerParams(
            dimension_semantics=("arbitrary",), vmem_limit_bytes=VMEM_LIMIT),
        name="tail",
    )(x2, mod3, ya, yb, proj, proj, w_pa, w_pb, w_o, g_final)

def kernel(x, c, g_pre, w_c, b_c, w_in, conv_w, conv_b, w_r_f, b_r_f, w_i_f, b_i_f, lam_f,
           w_r_b, b_r_b, w_i_b, b_i_b, lam_b, rpb, w_pa, w_pb, w_o, g_final):
    nb, seq, d = x.shape
    depth = g_pre.shape[0]
    n = nb * seq
    rows = seq // GRID_W
    groups = w_in.shape[2] // d
    xa_col, za_col, q_col, k_col, v_col, zb_col, ga_col, gb_col = range(groups)
    assert xa_col == 0
    acts = ["id"] * groups
    acts[za_col] = acts[zb_col] = "silu"
    acts[ga_col] = acts[gb_col] = "sigmoid"
    acts[q_col] = "qscale"

    c_pad = jnp.zeros((MOD_ROWS, d), F32).at[:nb].set(c)
    x2 = x.reshape(n, d)
    for l in range(depth):
        mod = _modulation(c_pad, w_c[l], b_c[l].reshape(1, 3 * d))
        mod3 = mod[:nb].reshape(nb, 1, 3 * d)

        proj = _projection(x2, mod3, g_pre[l].reshape(1, d), w_in[l], seq, tuple(acts))
        proj3 = proj.reshape(nb, seq, groups * d)

        cw, cb = conv_w[l], conv_b[l].reshape(1, d)
        hf = _rglru(proj3, cw, cb, jnp.stack([w_r_f[l], w_i_f[l]]),
                    jnp.stack([b_r_f[l], b_i_f[l]]).reshape(2, 1, d), lam_f[l].reshape(1, d), False)
        ya = _rglru(proj3, cw, cb, jnp.stack([w_r_b[l], w_i_b[l]]),
                    jnp.stack([b_r_b[l], b_i_b[l]]).reshape(2, 1, d), lam_b[l].reshape(1, d), True,
                    h_other=hf, za_col=za_col)

        yb = _natten(proj3, _bias_table(rpb[l], rows), q_col, k_col, v_col, zb_col)

        x2 = _tail(x2, mod3, ya.reshape(n, d), yb.reshape(n, d), proj, w_pa[l], w_pb[l], w_o[l],
                   g_final.reshape(1, d), seq, (ga_col, gb_col), final=(l == depth - 1))
    return x2.reshape(nb, seq, d)
```

```python
import functools

import jax
import jax.numpy as jnp
import numpy as np
from jax import lax
from jax.experimental import pallas as pl
from jax.experimental.pallas import tpu as pltpu

F32 = jnp.float32
BF16 = jnp.bfloat16

SUBLANES = 8
LANES = 128
MXU_TILE = 256

GRID_W = 64
A_BLOCKS = 16
CONV_W = 4
RG_C = 8.0
N_HEADS = 16
HEAD_DIM = 64
WIN_R = 8
WIN_C = 16
EPS = 1e-6
MASK_BIAS = -1e30
LOG2_E = 1.4426950408889634
RSQRT_FLOOR = 1e-30

PROJ_TM = 512
NORM_RB = 64
CAST_RB = 128
SCAN_TC = 128
TAIL_TM = 512
TAIL_SUB = 256
NAT_ROWS = 2
MOD_ROWS = 16

VMEM_LIMIT = 56 * 1024 * 1024


def _sigmoid(x):
    return 0.5 * jnp.tanh(0.5 * x) + 0.5


def _silu(x):
    return x * _sigmoid(x)


_ACTIVATIONS = {
    "id": lambda v: v,
    "silu": _silu,
    "sigmoid": _sigmoid,
    "qscale": lambda v: v * (HEAD_DIM ** -0.5 * LOG2_E),
}


def _mod_kernel(c_ref, w_ref, b_ref, o_ref):
    sc = _silu(c_ref[...]).astype(BF16)
    w = w_ref[...].astype(BF16)
    o_ref[...] = jnp.dot(sc, w, preferred_element_type=F32) + b_ref[...]


def _modulation(c_pad, w_c, b_c):
    d = w_c.shape[0]
    return pl.pallas_call(
        _mod_kernel,
        grid=(3,),
        in_specs=[
            pl.BlockSpec((MOD_ROWS, d), lambda j: (0, 0)),
            pl.BlockSpec((d, d), lambda j: (0, j)),
            pl.BlockSpec((1, d), lambda j: (0, j)),
        ],
        out_specs=pl.BlockSpec((MOD_ROWS, d), lambda j: (0, j)),
        out_shape=jax.ShapeDtypeStruct((MOD_ROWS, 3 * d), F32),
        compiler_params=pltpu.CompilerParams(
            dimension_semantics=("parallel",), vmem_limit_bytes=VMEM_LIMIT),
        name="mod",
    )(c_pad, w_c, b_c)


def _norm_rows(x_ref, mod_ref, g, h_ref, slot, r0):
    d = x_ref.shape[1]
    x = x_ref[pl.ds(r0, NORM_RB), :]
    ms = jnp.mean(x * x, axis=-1, keepdims=True)
    y = x * lax.rsqrt(ms + EPS) * g
    h = y * (1.0 + mod_ref[:, d:2 * d]) + mod_ref[:, 0:d]
    h_ref[slot, pl.ds(r0, NORM_RB), :] = h.astype(BF16)


def _proj_kernel(acts, x0_ref, mod0_ref, xn_ref, modn_ref, g_ref, w_hbm, o_ref,
                 h_ref, w_ref, stage_ref, sem):
    i = pl.program_id(0)
    tm, d = xn_ref.shape
    g = g_ref[...]

    def weight_copy(j):
        return pltpu.make_async_copy(w_hbm.at[:, pl.ds(j * d, d)], stage_ref.at[j % 2],
                                     sem.at[j % 2])

    @pl.when(i == 0)
    def _():
        weight_copy(0).start()
        for j in range(len(acts)):
            if j + 1 < len(acts):
                weight_copy(j + 1).start()
            weight_copy(j).wait()

            def cast(k, carry, j=j):
                r0 = pl.multiple_of(k * CAST_RB, CAST_RB)
                w_ref[pl.ds(r0, CAST_RB), j * d:(j + 1) * d] = (
                    stage_ref[j % 2, pl.ds(r0, CAST_RB), :].astype(BF16))
                return carry
            lax.fori_loop(0, d // CAST_RB, cast, 0)

        def body(k, carry):
            _norm_rows(x0_ref, mod0_ref, g, h_ref, 0, pl.multiple_of(k * NORM_RB, NORM_RB))
            return carry
        lax.fori_loop(0, tm // NORM_RB, body, 0)

    cur = i % 2
    for k in range(tm // NORM_RB):
        _norm_rows(xn_ref, modn_ref, g, h_ref, 1 - cur, k * NORM_RB)
    for j, act in enumerate(acts):
        cols = slice(j * d, (j + 1) * d)
        acc = jnp.dot(h_ref[cur], w_ref[:, cols], preferred_element_type=F32)
        o_ref[:, cols] = _ACTIVATIONS[act](acc).astype(BF16)


def _projection(x2, mod3, g_pre, w_in, seq, acts):
    n, d = x2.shape
    n_tiles = n // PROJ_TM
    nxt = lambda i: jnp.minimum(i + 1, n_tiles - 1)
    return pl.pallas_call(
        functools.partial(_proj_kernel, acts),
        grid=(n_tiles,),
        in_specs=[
            pl.BlockSpec((PROJ_TM, d), lambda i: (0, 0), pipeline_mode=pl.Buffered(1)),
            pl.BlockSpec((None, 1, 3 * d), lambda i: (0, 0, 0)),
            pl.BlockSpec((PROJ_TM, d), lambda i: (nxt(i), 0)),
            pl.BlockSpec((None, 1, 3 * d), lambda i: ((nxt(i) * PROJ_TM) // seq, 0, 0)),
            pl.BlockSpec((1, d), lambda i: (0, 0)),
            pl.BlockSpec(memory_space=pl.ANY),
        ],
        out_specs=pl.BlockSpec((PROJ_TM, w_in.shape[1]), lambda i: (i, 0)),
        out_shape=jax.ShapeDtypeStruct((n, w_in.shape[1]), BF16),
        scratch_shapes=[
            pltpu.VMEM((2, PROJ_TM, d), BF16),
            pltpu.VMEM(w_in.shape, BF16),
            pltpu.VMEM((2, d, d), w_in.dtype),
            pltpu.SemaphoreType.DMA((2,)),
        ],
        compiler_params=pltpu.CompilerParams(
            dimension_semantics=("arbitrary",), vmem_limit_bytes=VMEM_LIMIT),
        name="proj",
    )(x2, mod3, x2, mod3, g_pre, w_in)


def _rglru_kernel(rev, reuse, *refs):
    if reuse:
        xh_ref, wblk_ref, bg_ref, lam_ref, o_ref = refs[:5]
        wg_ref, sa_ref, su_ref, sh_ref, carry_ref = refs[5:]
        nb, tc, d = xh_ref.shape
    else:
        (xa_ref, prev_ref, next_ref, shift_ref, cw_ref, cb_ref, wblk_ref, bg_ref, lam_ref,
         o_ref, xh_out_ref) = refs[:11]
        wg_ref, sa_ref, su_ref, sh_ref, carry_ref = refs[11:]
        nb, tc, d = xa_ref.shape
    i = pl.program_id(0)
    n_chunks = pl.num_programs(0) - 1
    groups = tc // SUBLANES
    n_tiles = d // MXU_TILE
    n_gates, n_blk, bw, _ = wblk_ref.shape
    per_tile = MXU_TILE // bw
    ig = jnp.minimum(i, n_chunks - 1)
    ci = n_chunks - 1 - ig if rev else ig
    cur = i % 2
    old = 1 - cur

    @pl.when(i == 0)
    def _():
        carry_ref[...] = jnp.zeros_like(carry_ref)
        sa_ref[1] = jnp.zeros(sa_ref.shape[1:], F32)
        su_ref[1] = jnp.zeros(su_ref.shape[1:], F32)
        wg_ref[...] = jnp.zeros_like(wg_ref)
        for gate in range(n_gates):
            for blk in range(n_blk):
                off = (blk % per_tile) * bw
                wg_ref[gate, blk // per_tile, off:off + bw, off:off + bw] = (
                    wblk_ref[gate, blk].astype(BF16))

    has_prev = ci > 0
    has_next = ci < n_chunks - 1
    lam = lam_ref[...]
    softplus_neg_lam = jnp.maximum(-lam, 0.0) + jnp.log1p(jnp.exp(-jnp.abs(lam)))
    c8l = (-0.5 * RG_C * LOG2_E) * softplus_neg_lam
    brh = 0.5 * bg_ref[0]
    bih = 0.5 * bg_ref[1]
    if not reuse:
        cwh = 0.5 * cw_ref[...]
        cbh = 0.5 * cb_ref[...]
        shift_m = shift_ref[...]
        taps = [k for k in range(CONV_W) if k != CONV_W // 2]

    def scan_row(b):
        h = carry_ref[b]
        for gg in range(groups):
            g = groups - 1 - gg if rev else gg
            for jj in range(SUBLANES):
                j = SUBLANES - 1 - jj if rev else jj
                tok = pl.ds(j, SUBLANES, stride=SUBLANES)
                h = sa_ref[old, b, g, tok, :] * h + su_ref[old, b, g, tok, :]
                sh_ref[b, g, tok, :] = h
            if gg % 2 == 1:
                gp = g // 2
                r0 = gp * 2 * SUBLANES
                for cblk in range(d // LANES):
                    blk = slice(cblk * SUBLANES, (cblk + 1) * SUBLANES)
                    rows16 = jnp.concatenate(
                        [sh_ref[b, 2 * gp, blk, :], sh_ref[b, 2 * gp + 1, blk, :]], axis=0)
                    o_ref[b, pl.ds(r0, 2 * SUBLANES), cblk * LANES:(cblk + 1) * LANES] = (
                        rows16.astype(o_ref.dtype))
        carry_ref[b] = h

    def conv_half(b):
        xb = xa_ref[b]
        halo = jnp.concatenate([
            jnp.where(has_prev, prev_ref[b], jnp.zeros_like(prev_ref[b])),
            jnp.where(has_next, next_ref[b], jnp.zeros_like(next_ref[b]))], axis=0)
        sh = jnp.dot(shift_m, jnp.concatenate([xb, halo], axis=0), preferred_element_type=F32)
        xh = cbh + cwh[CONV_W // 2:CONV_W // 2 + 1] * xb.astype(F32)
        for n, k in enumerate(taps):
            xh = xh + cwh[k:k + 1] * sh[n * tc:(n + 1) * tc]
        return xh

    def gates(b):
        if reuse:
            xhb = xh_ref[b]
            xh = xhb.astype(F32)
        else:
            xh = conv_half(b)
            xhb = xh.astype(BF16)
            xh_out_ref[b] = xhb
        for k in range(n_tiles):
            sl = slice(k * MXU_TILE, (k + 1) * MXU_TILE)
            xk = xhb[:, sl]
            tr = jnp.tanh(jnp.dot(xk, wg_ref[0, k], preferred_element_type=F32) + brh[:, sl])
            ti = jnp.tanh(jnp.dot(xk, wg_ref[1, k], preferred_element_type=F32) + bih[:, sl])
            a = jnp.exp2(c8l[:, sl] * (tr + 1.0))
            z = 1.0 - a * a
            u = (z * lax.rsqrt(jnp.maximum(z, RSQRT_FLOOR))) * ((ti + 1.0) * xh[:, sl])
            for c2 in range(MXU_TILE // LANES):
                cblk = k * (MXU_TILE // LANES) + c2
                rows = slice(cblk * SUBLANES, (cblk + 1) * SUBLANES)
                lanes = slice(c2 * LANES, (c2 + 1) * LANES)
                sa_ref[cur, b, :, rows, :] = a[:, lanes].reshape(groups, SUBLANES, LANES)
                su_ref[cur, b, :, rows, :] = u[:, lanes].reshape(groups, SUBLANES, LANES)

    def body(b, carry):
        scan_row(b)
        gates(b)
        return carry

    lax.fori_loop(0, nb, body, 0)


def _shift_matrix(tc):
    taps = [k for k in range(CONV_W) if k != CONV_W // 2]
    m = np.zeros((len(taps) * tc, tc + 2 * SUBLANES), np.float32)
    for n, k in enumerate(taps):
        for t in range(tc):
            s = t - CONV_W // 2 + k
            if s < 0:
                col = tc + SUBLANES + s
            elif s >= tc:
                col = tc + SUBLANES + (s - tc)
            else:
                col = s
            m[n * tc + t, col] = 1.0
    return m


def _rglru(proj3, conv_w, conv_b, wblk, bg, lam, rev, xh=None):
    nb, seq, _ = proj3.shape
    reuse = xh is not None
    d = wblk.shape[1] * wblk.shape[2]
    n_chunks = seq // SCAN_TC
    n_groups = seq // SUBLANES
    gpc = SCAN_TC // SUBLANES

    def chunk(i):
        return n_chunks - 1 - i if rev else i

    def gated(i):
        return chunk(jnp.minimum(i, n_chunks - 1))

    def scanned(i):
        return chunk(jnp.maximum(i - 1, 0))

    n_tiles = d // MXU_TILE
    n_gates = wblk.shape[0]
    whole = lambda a: pl.BlockSpec(a.shape, lambda i: (0,) * a.ndim)
    gated_spec = pl.BlockSpec((nb, SCAN_TC, d), lambda i: (0, gated(i), 0))
    h_spec = pl.BlockSpec((nb, SCAN_TC, d), lambda i: (0, scanned(i), 0))
    h_shape = jax.ShapeDtypeStruct((nb, seq, d), BF16)
    if reuse:
        in_specs = [gated_spec, whole(wblk), whole(bg), whole(lam)]
        args = (xh, wblk, bg, lam)
        out_specs, out_shape = h_spec, h_shape
    else:
        proj4 = proj3.reshape(nb, n_groups, SUBLANES, proj3.shape[2])
        shift_m = jnp.asarray(_shift_matrix(SCAN_TC), BF16)
        in_specs = [
            gated_spec,
            pl.BlockSpec((nb, None, SUBLANES, d),
                         lambda i: (0, jnp.maximum(gated(i) * gpc - 1, 0), 0, 0)),
            pl.BlockSpec((nb, None, SUBLANES, d),
                         lambda i: (0, jnp.minimum((gated(i) + 1) * gpc, n_groups - 1), 0, 0)),
            whole(shift_m), whole(conv_w), whole(conv_b), whole(wblk), whole(bg), whole(lam),
        ]
        args = (proj3, proj4, proj4, shift_m, conv_w, conv_b, wblk, bg, lam)
        out_specs, out_shape = [h_spec, gated_spec], [h_shape, h_shape]
    return pl.pallas_call(
        functools.partial(_rglru_kernel, rev, reuse),
        grid=(n_chunks + 1,),
        in_specs=in_specs,
        out_specs=out_specs,
        out_shape=out_shape,
        scratch_shapes=[
            pltpu.VMEM((n_gates, n_tiles, MXU_TILE, MXU_TILE), BF16),
            pltpu.VMEM((2, nb, gpc, d // LANES * SUBLANES, LANES), F32),
            pltpu.VMEM((2, nb, gpc, d // LANES * SUBLANES, LANES), F32),
            pltpu.VMEM((nb, gpc, d // LANES * SUBLANES, LANES), F32),
            pltpu.VMEM((nb, SUBLANES, LANES), F32),
        ],
        compiler_params=pltpu.CompilerParams(
            dimension_semantics=("arbitrary",), vmem_limit_bytes=VMEM_LIMIT),
        name="rglru_bwd" if rev else "rglru_fwd",
    )(*args)


def _natten_kernel(q_ref, k_ref, v_ref, *rest):
    bias_refs, o_ref = rest[:-1], rest[-1]
    step = pl.program_id(1)
    rows = pl.num_programs(1) * NAT_ROWS
    gw = GRID_W
    d = q_ref.shape[1]
    win_r = min(WIN_R, rows)
    lane = lax.broadcasted_iota(jnp.int32, (gw, LANES), 1)
    lo = lane < HEAD_DIM
    zero = jnp.zeros((gw, LANES), BF16)
    n_pairs = d // LANES
    wins = []
    scores = []
    for rr in range(NAT_ROWS):
        r = step * NAT_ROWS + rr
        r0 = jnp.clip(r - win_r // 2, 0, rows - win_r)
        k0 = pl.multiple_of(r0 * gw, gw)
        wins.append(k0)
        for p in range(n_pairs):
            sl = slice(p * LANES, (p + 1) * LANES)
            qp = q_ref[rr * gw:(rr + 1) * gw, sl]
            qs = jnp.concatenate([jnp.where(lo, qp, zero), jnp.where(lo, zero, qp)], axis=0)
            scores.append(lax.dot_general(qs, k_ref[pl.ds(k0, win_r * gw), sl],
                                          (((1,), (1,)), ((), ())), preferred_element_type=F32))
    probs = []
    for rr in range(NAT_ROWS):
        for p in range(n_pairs):
            s = scores[rr * n_pairs + p] + bias_refs[rr][p]
            m = jnp.max(s, axis=-1, keepdims=True)
            probs.append(jnp.exp2(s - m).astype(BF16))
    ones = jnp.ones((win_r * gw, LANES), BF16)
    for rr in range(NAT_ROWS):
        for p in range(n_pairs):
            sl = slice(p * LANES, (p + 1) * LANES)
            v_ext = jnp.concatenate([v_ref[pl.ds(wins[rr], win_r * gw), sl], ones], axis=1)
            pv = jnp.dot(probs[rr * n_pairs + p], v_ext, preferred_element_type=F32)
            o = pv[:, :LANES] / pv[:, LANES:]
            o_ref[rr * gw:(rr + 1) * gw, sl] = jnp.where(lo, o[:gw], o[gw:]).astype(BF16)


def _natten(proj3, bias_tab, q_col, k_col, v_col):
    nb, seq, _ = proj3.shape
    d = N_HEADS * HEAD_DIM
    rows = seq // GRID_W
    win_r = min(WIN_R, rows)
    n_pairs = d // LANES

    def pattern(r):
        r0 = jnp.clip(r - win_r // 2, 0, rows - win_r)
        return r0 - r + (WIN_R - 1)

    def bias_spec(rr):
        return pl.BlockSpec((None, n_pairs, 2 * GRID_W, win_r * GRID_W),
                            lambda b, i: (pattern(i * NAT_ROWS + rr), 0, 0, 0))

    return pl.pallas_call(
        _natten_kernel,
        grid=(nb, rows // NAT_ROWS),
        in_specs=[
            pl.BlockSpec((None, NAT_ROWS * GRID_W, d), lambda b, i: (b, i, q_col)),
            pl.BlockSpec((None, seq, d), lambda b, i: (b, 0, k_col)),
            pl.BlockSpec((None, seq, d), lambda b, i: (b, 0, v_col)),
        ] + [bias_spec(rr) for rr in range(NAT_ROWS)],
        out_specs=pl.BlockSpec((None, NAT_ROWS * GRID_W, d), lambda b, i: (b, i, 0)),
        out_shape=jax.ShapeDtypeStruct((nb, seq, d), BF16),
        compiler_params=pltpu.CompilerParams(
            dimension_semantics=("parallel", "arbitrary"), vmem_limit_bytes=VMEM_LIMIT),
        name="natten",
    )(proj3, proj3, proj3, *([bias_tab] * NAT_ROWS))


def _bias_kernel(rp_ref, o_ref):
    d0 = pl.program_id(0)
    n_pairs, _, keys = o_ref.shape
    gw = GRID_W
    q_idx = lax.broadcasted_iota(jnp.int32, (gw, LANES), 0)
    lane = lax.broadcasted_iota(jnp.int32, (gw, LANES), 1)
    col = jnp.where(lane < gw, lane, lane - gw)
    c0 = jnp.clip(q_idx - WIN_C // 2, 0, gw - WIN_C)
    valid = (col >= c0) & (col < c0 + WIN_C)
    first = lane < gw
    for pair in range(n_pairs):
        for hh in range(2):
            h = 2 * pair + hh
            for t in range(keys // LANES):
                va = jnp.broadcast_to(rp_ref[h, pl.ds(d0 + 2 * t, 1), :], (gw, LANES))
                vb = jnp.broadcast_to(rp_ref[h, pl.ds(d0 + 2 * t + 1, 1), :], (gw, LANES))
                ta = pltpu.roll(va, 0, 1, stride=1, stride_axis=0)
                tb = pltpu.roll(vb, gw, 1, stride=1, stride_axis=0)
                tile = jnp.where(valid, LOG2_E * jnp.where(first, ta, tb), MASK_BIAS)
                o_ref[pair, hh * gw:(hh + 1) * gw, t * LANES:(t + 1) * LANES] = tile


def _bias_table(rpb, rows):
    win_r = min(WIN_R, rows)
    assert 2 * GRID_W == LANES and win_r % 2 == 0
    nh, nd, _ = rpb.shape
    rp = jnp.concatenate([rpb[:, :, WIN_C - 1:],
                          jnp.zeros((nh, nd, LANES - (2 * WIN_C - 1)), rpb.dtype),
                          rpb[:, :, :WIN_C - 1]], axis=-1).astype(F32)
    n_pat = WIN_R
    return pl.pallas_call(
        _bias_kernel,
        grid=(n_pat,),
        in_specs=[pl.BlockSpec(rp.shape, lambda p: (0, 0, 0))],
        out_specs=pl.BlockSpec((None, nh // 2, 2 * GRID_W, win_r * GRID_W), lambda p: (p, 0, 0, 0)),
        out_shape=jax.ShapeDtypeStruct((n_pat, nh // 2, 2 * GRID_W, win_r * GRID_W), F32),
        compiler_params=pltpu.CompilerParams(
            dimension_semantics=("parallel",), vmem_limit_bytes=VMEM_LIMIT),
        name="bias_table",
    )(rp)


def _tail_kernel(final, x_ref, mod_ref, hf_ref, hb_ref, za_ref, zb_ref, ga_ref, gb_ref, att_ref,
                 wpa_ref, wpb_ref, wo_ref, gf_ref, o_ref, w_scr):
    d = x_ref.shape[1]

    @pl.when(pl.program_id(0) == 0)
    def _():
        for n, w_ref in enumerate((wpa_ref, wpb_ref, wo_ref)):
            w_scr[n] = w_ref[...].astype(BF16)

    for s in range(x_ref.shape[0] // TAIL_SUB):
        rows = slice(s * TAIL_SUB, (s + 1) * TAIL_SUB)
        h = hf_ref[rows, :].astype(F32) + hb_ref[rows, :].astype(F32)
        ya = (h * za_ref[rows, :].astype(F32)).astype(BF16)
        ya = jnp.dot(ya, w_scr[0], preferred_element_type=F32)
        yb = (att_ref[rows, :].astype(F32) * zb_ref[rows, :].astype(F32)).astype(BF16)
        yb = jnp.dot(yb, w_scr[1], preferred_element_type=F32)
        merged = ga_ref[rows, :].astype(F32) * ya + gb_ref[rows, :].astype(F32) * yb
        res = jnp.dot(merged.astype(BF16), w_scr[2], preferred_element_type=F32)
        xo = x_ref[rows, :] + mod_ref[:, 2 * d:3 * d] * res
        if final:
            ms = jnp.mean(xo * xo, axis=-1, keepdims=True)
            xo = xo * lax.rsqrt(ms + EPS) * gf_ref[...]
        o_ref[rows, :] = xo


def _tail(x2, mod3, hf, hb, proj, att2, w_pa, w_pb, w_o, g_final, seq, cols, final):
    n, d = x2.shape
    za_col, zb_col, ga_col, gb_col = cols
    tok = lambda i: (i, 0)
    wspec = pl.BlockSpec((d, d), lambda i: (0, 0), pipeline_mode=pl.Buffered(1))
    return pl.pallas_call(
        functools.partial(_tail_kernel, final),
        grid=(n // TAIL_TM,),
        in_specs=[
            pl.BlockSpec((TAIL_TM, d), tok),
            pl.BlockSpec((None, 1, 3 * d), lambda i: ((i * TAIL_TM) // seq, 0, 0)),
            pl.BlockSpec((TAIL_TM, d), tok),
            pl.BlockSpec((TAIL_TM, d), tok),
            pl.BlockSpec((TAIL_TM, d), lambda i: (i, za_col)),
            pl.BlockSpec((TAIL_TM, d), lambda i: (i, zb_col)),
            pl.BlockSpec((TAIL_TM, d), lambda i: (i, ga_col)),
            pl.BlockSpec((TAIL_TM, d), lambda i: (i, gb_col)),
            pl.BlockSpec((TAIL_TM, d), tok),
            wspec, wspec, wspec,
            pl.BlockSpec((1, d), lambda i: (0, 0)),
        ],
        out_specs=pl.BlockSpec((TAIL_TM, d), tok),
        out_shape=jax.ShapeDtypeStruct((n, d), F32),
        scratch_shapes=[pltpu.VMEM((3, d, d), BF16)],
        compiler_params=pltpu.CompilerParams(
            dimension_semantics=("arbitrary",), vmem_limit_bytes=VMEM_LIMIT),
        name="tail",
    )(x2, mod3, hf, hb, proj, proj, proj, proj, att2, w_pa, w_pb, w_o, g_final)


def kernel(x, c, g_pre, w_c, b_c, w_in, conv_w, conv_b, w_r_f, b_r_f, w_i_f, b_i_f, lam_f,
           w_r_b, b_r_b, w_i_b, b_i_b, lam_b, rpb, w_pa, w_pb, w_o, g_final):
    nb, seq, d = x.shape
    depth = g_pre.shape[0]
    n = nb * seq
    rows = seq // GRID_W
    groups = w_in.shape[2] // d
    xa_col, za_col, q_col, k_col, v_col, zb_col, ga_col, gb_col = range(groups)
    assert xa_col == 0
    acts = ["id"] * groups
    acts[za_col] = acts[zb_col] = "silu"
    acts[ga_col] = acts[gb_col] = "sigmoid"
    acts[q_col] = "qscale"

    c_pad = jnp.zeros((MOD_ROWS, d), F32).at[:nb].set(c)
    x2 = x.reshape(n, d)
    for l in range(depth):
        mod = _modulation(c_pad, w_c[l], b_c[l].reshape(1, 3 * d))
        mod3 = mod[:nb].reshape(nb, 1, 3 * d)

        proj = _projection(x2, mod3, g_pre[l].reshape(1, d), w_in[l], seq, tuple(acts))
        proj3 = proj.reshape(nb, seq, groups * d)

        cw, cb = conv_w[l], conv_b[l].reshape(1, d)
        hf, xh = _rglru(proj3, cw, cb, jnp.stack([w_r_f[l], w_i_f[l]]),
                        jnp.stack([b_r_f[l], b_i_f[l]]).reshape(2, 1, d), lam_f[l].reshape(1, d),
                        False)
        hb = _rglru(proj3, cw, cb, jnp.stack([w_r_b[l], w_i_b[l]]),
                    jnp.stack([b_r_b[l], b_i_b[l]]).reshape(2, 1, d), lam_b[l].reshape(1, d), True,
                    xh=xh)

        att = _natten(proj3, _bias_table(rpb[l], rows), q_col, k_col, v_col)

        x2 = _tail(x2, mod3, hf.reshape(n, d), hb.reshape(n, d), proj, att.reshape(n, d),
                   w_pa[l], w_pb[l], w_o[l],
                   g_final.reshape(1, d), seq, (za_col, zb_col, ga_col, gb_col),
                   final=(l == depth - 1))
    return x2.reshape(nb, seq, d)
```

```python
import functools

import jax
import jax.numpy as jnp
import numpy as np
from jax import lax
from jax.experimental import pallas as pl
from jax.experimental.pallas import tpu as pltpu

F32 = jnp.float32
BF16 = jnp.bfloat16

SUBLANES = 8
LANES = 128
MXU_TILE = 256

GRID_W = 64
A_BLOCKS = 16
CONV_W = 4
RG_C = 8.0
N_HEADS = 16
HEAD_DIM = 64
WIN_R = 8
WIN_C = 16
EPS = 1e-6
MASK_BIAS = -1e30
LOG2_E = 1.4426950408889634
RSQRT_FLOOR = 1e-30

PROJ_TM = 512
NORM_RB = 64
CAST_RB = 128
SCAN_TC = 128
TAIL_TM = 512
TAIL_SUB = 256
NAT_ROWS = 4
MOD_ROWS = 16

VMEM_LIMIT = 56 * 1024 * 1024
NAT_VMEM_LIMIT = 61 * 1024 * 1024


def _sigmoid(x):
    return 0.5 * jnp.tanh(0.5 * x) + 0.5


def _silu(x):
    return x * _sigmoid(x)


_ACTIVATIONS = {
    "id": lambda v: v,
    "silu": _silu,
    "sigmoid": _sigmoid,
    "qscale": lambda v: v * (HEAD_DIM ** -0.5 * LOG2_E),
}


def _mod_kernel(c_ref, w_ref, b_ref, o_ref):
    sc = _silu(c_ref[...]).astype(BF16)
    w = w_ref[...].astype(BF16)
    o_ref[...] = jnp.dot(sc, w, preferred_element_type=F32) + b_ref[...]


def _modulation(c_pad, w_c, b_c):
    d = w_c.shape[0]
    return pl.pallas_call(
        _mod_kernel,
        grid=(3,),
        in_specs=[
            pl.BlockSpec((MOD_ROWS, d), lambda j: (0, 0)),
            pl.BlockSpec((d, d), lambda j: (0, j)),
            pl.BlockSpec((1, d), lambda j: (0, j)),
        ],
        out_specs=pl.BlockSpec((MOD_ROWS, d), lambda j: (0, j)),
        out_shape=jax.ShapeDtypeStruct((MOD_ROWS, 3 * d), F32),
        compiler_params=pltpu.CompilerParams(
            dimension_semantics=("parallel",), vmem_limit_bytes=VMEM_LIMIT),
        name="mod",
    )(c_pad, w_c, b_c)


def _norm_rows(x_ref, mod_ref, g, h_ref, slot, r0):
    d = x_ref.shape[1]
    x = x_ref[pl.ds(r0, NORM_RB), :]
    ms = jnp.mean(x * x, axis=-1, keepdims=True)
    y = x * lax.rsqrt(ms + EPS) * g
    h = y * (1.0 + mod_ref[:, d:2 * d]) + mod_ref[:, 0:d]
    h_ref[slot, pl.ds(r0, NORM_RB), :] = h.astype(BF16)


def _proj_kernel(acts, x0_ref, mod0_ref, xn_ref, modn_ref, g_ref, w_hbm, o_ref,
                 h_ref, w_ref, stage_ref, sem):
    i = pl.program_id(0)
    tm, d = xn_ref.shape
    g = g_ref[...]

    def weight_copy(j):
        return pltpu.make_async_copy(w_hbm.at[:, pl.ds(j * d, d)], stage_ref.at[j % 2],
                                     sem.at[j % 2])

    @pl.when(i == 0)
    def _():
        weight_copy(0).start()
        for j in range(len(acts)):
            if j + 1 < len(acts):
                weight_copy(j + 1).start()
            weight_copy(j).wait()

            def cast(k, carry, j=j):
                r0 = pl.multiple_of(k * CAST_RB, CAST_RB)
                w_ref[pl.ds(r0, CAST_RB), j * d:(j + 1) * d] = (
                    stage_ref[j % 2, pl.ds(r0, CAST_RB), :].astype(BF16))
                return carry
            lax.fori_loop(0, d // CAST_RB, cast, 0)

        def body(k, carry):
            _norm_rows(x0_ref, mod0_ref, g, h_ref, 0, pl.multiple_of(k * NORM_RB, NORM_RB))
            return carry
        lax.fori_loop(0, tm // NORM_RB, body, 0)

    cur = i % 2
    for k in range(tm // NORM_RB):
        _norm_rows(xn_ref, modn_ref, g, h_ref, 1 - cur, k * NORM_RB)
    for j, act in enumerate(acts):
        cols = slice(j * d, (j + 1) * d)
        acc = jnp.dot(h_ref[cur], w_ref[:, cols], preferred_element_type=F32)
        o_ref[:, cols] = _ACTIVATIONS[act](acc).astype(BF16)


def _projection(x2, mod3, g_pre, w_in, seq, acts):
    n, d = x2.shape
    n_tiles = n // PROJ_TM
    nxt = lambda i: jnp.minimum(i + 1, n_tiles - 1)
    return pl.pallas_call(
        functools.partial(_proj_kernel, acts),
        grid=(n_tiles,),
        in_specs=[
            pl.BlockSpec((PROJ_TM, d), lambda i: (0, 0), pipeline_mode=pl.Buffered(1)),
            pl.BlockSpec((None, 1, 3 * d), lambda i: (0, 0, 0)),
            pl.BlockSpec((PROJ_TM, d), lambda i: (nxt(i), 0)),
            pl.BlockSpec((None, 1, 3 * d), lambda i: ((nxt(i) * PROJ_TM) // seq, 0, 0)),
            pl.BlockSpec((1, d), lambda i: (0, 0)),
            pl.BlockSpec(memory_space=pl.ANY),
        ],
        out_specs=pl.BlockSpec((PROJ_TM, w_in.shape[1]), lambda i: (i, 0)),
        out_shape=jax.ShapeDtypeStruct((n, w_in.shape[1]), BF16),
        scratch_shapes=[
            pltpu.VMEM((2, PROJ_TM, d), BF16),
            pltpu.VMEM(w_in.shape, BF16),
            pltpu.VMEM((2, d, d), w_in.dtype),
            pltpu.SemaphoreType.DMA((2,)),
        ],
        compiler_params=pltpu.CompilerParams(
            dimension_semantics=("arbitrary",), vmem_limit_bytes=VMEM_LIMIT),
        name="proj",
    )(x2, mod3, x2, mod3, g_pre, w_in)


def _rglru_kernel(rev, reuse, *refs):
    if reuse:
        xh_ref, wblk_ref, bg_ref, lam_ref, o_ref = refs[:5]
        wg_ref, sa_ref, su_ref, sh_ref, carry_ref = refs[5:]
        nb, tc, d = xh_ref.shape
    else:
        (xa_ref, prev_ref, next_ref, shift_ref, cw_ref, cb_ref, wblk_ref, bg_ref, lam_ref,
         o_ref, xh_out_ref) = refs[:11]
        wg_ref, sa_ref, su_ref, sh_ref, carry_ref = refs[11:]
        nb, tc, d = xa_ref.shape
    i = pl.program_id(0)
    n_chunks = pl.num_programs(0) - 1
    groups = tc // SUBLANES
    n_tiles = d // MXU_TILE
    n_gates, n_blk, bw, _ = wblk_ref.shape
    per_tile = MXU_TILE // bw
    ig = jnp.minimum(i, n_chunks - 1)
    ci = n_chunks - 1 - ig if rev else ig
    cur = i % 2
    old = 1 - cur

    @pl.when(i == 0)
    def _():
        carry_ref[...] = jnp.zeros_like(carry_ref)
        sa_ref[1] = jnp.zeros(sa_ref.shape[1:], F32)
        su_ref[1] = jnp.zeros(su_ref.shape[1:], F32)
        wg_ref[...] = jnp.zeros_like(wg_ref)
        for gate in range(n_gates):
            for blk in range(n_blk):
                off = (blk % per_tile) * bw
                wg_ref[gate, blk // per_tile, off:off + bw, off:off + bw] = (
                    wblk_ref[gate, blk].astype(BF16))

    has_prev = ci > 0
    has_next = ci < n_chunks - 1
    lam = lam_ref[...]
    softplus_neg_lam = jnp.maximum(-lam, 0.0) + jnp.log1p(jnp.exp(-jnp.abs(lam)))
    c8l = (-0.5 * RG_C * LOG2_E) * softplus_neg_lam
    brh = 0.5 * bg_ref[0]
    bih = 0.5 * bg_ref[1]
    if not reuse:
        cwh = 0.5 * cw_ref[...]
        cbh = 0.5 * cb_ref[...]
        shift_m = shift_ref[...]
        taps = [k for k in range(CONV_W) if k != CONV_W // 2]

    def scan_row(b):
        h = carry_ref[b]
        for gg in range(groups):
            g = groups - 1 - gg if rev else gg
            for jj in range(SUBLANES):
                j = SUBLANES - 1 - jj if rev else jj
                tok = pl.ds(j, SUBLANES, stride=SUBLANES)
                h = sa_ref[old, b, g, tok, :] * h + su_ref[old, b, g, tok, :]
                sh_ref[b, g, tok, :] = h
            if gg % 2 == 1:
                gp = g // 2
                r0 = gp * 2 * SUBLANES
                for cblk in range(d // LANES):
                    blk = slice(cblk * SUBLANES, (cblk + 1) * SUBLANES)
                    rows16 = jnp.concatenate(
                        [sh_ref[b, 2 * gp, blk, :], sh_ref[b, 2 * gp + 1, blk, :]], axis=0)
                    o_ref[b, pl.ds(r0, 2 * SUBLANES), cblk * LANES:(cblk + 1) * LANES] = (
                        rows16.astype(o_ref.dtype))
        carry_ref[b] = h

    def conv_half(b):
        xb = xa_ref[b]
        halo = jnp.concatenate([
            jnp.where(has_prev, prev_ref[b], jnp.zeros_like(prev_ref[b])),
            jnp.where(has_next, next_ref[b], jnp.zeros_like(next_ref[b]))], axis=0)
        sh = jnp.dot(shift_m, jnp.concatenate([xb, halo], axis=0), preferred_element_type=F32)
        xh = cbh + cwh[CONV_W // 2:CONV_W // 2 + 1] * xb.astype(F32)
        for n, k in enumerate(taps):
            xh = xh + cwh[k:k + 1] * sh[n * tc:(n + 1) * tc]
        return xh

    def gates(b):
        if reuse:
            xhb = xh_ref[b]
            xh = xhb.astype(F32)
        else:
            xh = conv_half(b)
            xhb = xh.astype(BF16)
            xh_out_ref[b] = xhb
        for k in range(n_tiles):
            sl = slice(k * MXU_TILE, (k + 1) * MXU_TILE)
            xk = xhb[:, sl]
            tr = jnp.tanh(jnp.dot(xk, wg_ref[0, k], preferred_element_type=F32) + brh[:, sl])
            ti = jnp.tanh(jnp.dot(xk, wg_ref[1, k], preferred_element_type=F32) + bih[:, sl])
            a = jnp.exp2(c8l[:, sl] * (tr + 1.0))
            z = 1.0 - a * a
            u = (z * lax.rsqrt(jnp.maximum(z, RSQRT_FLOOR))) * ((ti + 1.0) * xh[:, sl])
            for c2 in range(MXU_TILE // LANES):
                cblk = k * (MXU_TILE // LANES) + c2
                rows = slice(cblk * SUBLANES, (cblk + 1) * SUBLANES)
                lanes = slice(c2 * LANES, (c2 + 1) * LANES)
                sa_ref[cur, b, :, rows, :] = a[:, lanes].reshape(groups, SUBLANES, LANES)
                su_ref[cur, b, :, rows, :] = u[:, lanes].reshape(groups, SUBLANES, LANES)

    def body(b, carry):
        scan_row(b)
        gates(b)
        return carry

    lax.fori_loop(0, nb, body, 0)


def _shift_matrix(tc):
    taps = [k for k in range(CONV_W) if k != CONV_W // 2]
    m = np.zeros((len(taps) * tc, tc + 2 * SUBLANES), np.float32)
    for n, k in enumerate(taps):
        for t in range(tc):
            s = t - CONV_W // 2 + k
            if s < 0:
                col = tc + SUBLANES + s
            elif s >= tc:
                col = tc + SUBLANES + (s - tc)
            else:
                col = s
            m[n * tc + t, col] = 1.0
    return m


def _rglru(proj3, conv_w, conv_b, wblk, bg, lam, rev, xh=None):
    nb, seq, _ = proj3.shape
    reuse = xh is not None
    d = wblk.shape[1] * wblk.shape[2]
    n_chunks = seq // SCAN_TC
    n_groups = seq // SUBLANES
    gpc = SCAN_TC // SUBLANES

    def chunk(i):
        return n_chunks - 1 - i if rev else i

    def gated(i):
        return chunk(jnp.minimum(i, n_chunks - 1))

    def scanned(i):
        return chunk(jnp.maximum(i - 1, 0))

    n_tiles = d // MXU_TILE
    n_gates = wblk.shape[0]
    whole = lambda a: pl.BlockSpec(a.shape, lambda i: (0,) * a.ndim)
    gated_spec = pl.BlockSpec((nb, SCAN_TC, d), lambda i: (0, gated(i), 0))
    h_spec = pl.BlockSpec((nb, SCAN_TC, d), lambda i: (0, scanned(i), 0))
    h_shape = jax.ShapeDtypeStruct((nb, seq, d), BF16)
    if reuse:
        in_specs = [gated_spec, whole(wblk), whole(bg), whole(lam)]
        args = (xh, wblk, bg, lam)
        out_specs, out_shape = h_spec, h_shape
    else:
        proj4 = proj3.reshape(nb, n_groups, SUBLANES, proj3.shape[2])
        shift_m = jnp.asarray(_shift_matrix(SCAN_TC), BF16)
        in_specs = [
            gated_spec,
            pl.BlockSpec((nb, None, SUBLANES, d),
                         lambda i: (0, jnp.maximum(gated(i) * gpc - 1, 0), 0, 0)),
            pl.BlockSpec((nb, None, SUBLANES, d),
                         lambda i: (0, jnp.minimum((gated(i) + 1) * gpc, n_groups - 1), 0, 0)),
            whole(shift_m), whole(conv_w), whole(conv_b), whole(wblk), whole(bg), whole(lam),
        ]
        args = (proj3, proj4, proj4, shift_m, conv_w, conv_b, wblk, bg, lam)
        out_specs, out_shape = [h_spec, gated_spec], [h_shape, h_shape]
    return pl.pallas_call(
        functools.partial(_rglru_kernel, rev, reuse),
        grid=(n_chunks + 1,),
        in_specs=in_specs,
        out_specs=out_specs,
        out_shape=out_shape,
        scratch_shapes=[
            pltpu.VMEM((n_gates, n_tiles, MXU_TILE, MXU_TILE), BF16),
            pltpu.VMEM((2, nb, gpc, d // LANES * SUBLANES, LANES), F32),
            pltpu.VMEM((2, nb, gpc, d // LANES * SUBLANES, LANES), F32),
            pltpu.VMEM((nb, gpc, d // LANES * SUBLANES, LANES), F32),
            pltpu.VMEM((nb, SUBLANES, LANES), F32),
        ],
        compiler_params=pltpu.CompilerParams(
            dimension_semantics=("arbitrary",), vmem_limit_bytes=VMEM_LIMIT),
        name="rglru_bwd" if rev else "rglru_fwd",
    )(*args)


def _natten_kernel(q_ref, k_ref, v_ref, *rest):
    bias_ref, o_ref = rest
    step = pl.program_id(1)
    rows = pl.num_programs(1) * NAT_ROWS
    gw = GRID_W
    d = q_ref.shape[1]
    win_r = min(WIN_R, rows)
    lane = lax.broadcasted_iota(jnp.int32, (gw, LANES), 1)
    lo = lane < HEAD_DIM
    zero = jnp.zeros((gw, LANES), BF16)
    n_pairs = d // LANES
    wins = []
    pats = []
    scores = []
    for rr in range(NAT_ROWS):
        r = step * NAT_ROWS + rr
        r0 = jnp.clip(r - win_r // 2, 0, rows - win_r)
        k0 = pl.multiple_of(r0 * gw, gw)
        wins.append(k0)
        pats.append(r0 - r + (WIN_R - 1))
        for p in range(n_pairs):
            sl = slice(p * LANES, (p + 1) * LANES)
            qp = q_ref[rr * gw:(rr + 1) * gw, sl]
            qs = jnp.concatenate([jnp.where(lo, qp, zero), jnp.where(lo, zero, qp)], axis=0)
            scores.append(lax.dot_general(qs, k_ref[pl.ds(k0, win_r * gw), sl],
                                          (((1,), (1,)), ((), ())), preferred_element_type=F32))
    probs = []
    for rr in range(NAT_ROWS):
        for p in range(n_pairs):
            s = scores[rr * n_pairs + p] + bias_ref[pats[rr], p]
            m = jnp.max(s, axis=-1, keepdims=True)
            probs.append(jnp.exp2(s - m).astype(BF16))
    ones = jnp.ones((win_r * gw, LANES), BF16)
    for rr in range(NAT_ROWS):
        for p in range(n_pairs):
            sl = slice(p * LANES, (p + 1) * LANES)
            v_ext = jnp.concatenate([v_ref[pl.ds(wins[rr], win_r * gw), sl], ones], axis=1)
            pv = jnp.dot(probs[rr * n_pairs + p], v_ext, preferred_element_type=F32)
            o = pv[:, :LANES] / pv[:, LANES:]
            o_ref[rr * gw:(rr + 1) * gw, sl] = jnp.where(lo, o[:gw], o[gw:]).astype(BF16)


def _natten(proj3, bias_tab, q_col, k_col, v_col):
    nb, seq, _ = proj3.shape
    d = N_HEADS * HEAD_DIM
    rows = seq // GRID_W
    return pl.pallas_call(
        _natten_kernel,
        grid=(nb, rows // NAT_ROWS),
        in_specs=[
            pl.BlockSpec((None, NAT_ROWS * GRID_W, d), lambda b, i: (b, i, q_col)),
            pl.BlockSpec((None, seq, d), lambda b, i: (b, 0, k_col)),
            pl.BlockSpec((None, seq, d), lambda b, i: (b, 0, v_col)),
            pl.BlockSpec(bias_tab.shape, lambda b, i: (0, 0, 0, 0), pipeline_mode=pl.Buffered(1)),
        ],
        out_specs=pl.BlockSpec((None, NAT_ROWS * GRID_W, d), lambda b, i: (b, i, 0)),
        out_shape=jax.ShapeDtypeStruct((nb, seq, d), BF16),
        compiler_params=pltpu.CompilerParams(
            dimension_semantics=("parallel", "arbitrary"), vmem_limit_bytes=NAT_VMEM_LIMIT),
        name="natten",
    )(proj3, proj3, proj3, bias_tab)


def _bias_kernel(rp_ref, o_ref):
    d0 = pl.program_id(0)
    n_pairs, _, keys = o_ref.shape
    gw = GRID_W
    q_idx = lax.broadcasted_iota(jnp.int32, (gw, LANES), 0)
    lane = lax.broadcasted_iota(jnp.int32, (gw, LANES), 1)
    col = jnp.where(lane < gw, lane, lane - gw)
    c0 = jnp.clip(q_idx - WIN_C // 2, 0, gw - WIN_C)
    valid = (col >= c0) & (col < c0 + WIN_C)
    first = lane < gw
    for pair in range(n_pairs):
        for hh in range(2):
            h = 2 * pair + hh
            for t in range(keys // LANES):
                va = jnp.broadcast_to(rp_ref[h, pl.ds(d0 + 2 * t, 1), :], (gw, LANES))
                vb = jnp.broadcast_to(rp_ref[h, pl.ds(d0 + 2 * t + 1, 1), :], (gw, LANES))
                ta = pltpu.roll(va, 0, 1, stride=1, stride_axis=0)
                tb = pltpu.roll(vb, gw, 1, stride=1, stride_axis=0)
                tile = jnp.where(valid, LOG2_E * jnp.where(first, ta, tb), MASK_BIAS)
                o_ref[pair, hh * gw:(hh + 1) * gw, t * LANES:(t + 1) * LANES] = tile


def _bias_table(rpb, rows):
    win_r = min(WIN_R, rows)
    assert 2 * GRID_W == LANES and win_r % 2 == 0
    nh, nd, _ = rpb.shape
    rp = jnp.concatenate([rpb[:, :, WIN_C - 1:],
                          jnp.zeros((nh, nd, LANES - (2 * WIN_C - 1)), rpb.dtype),
                          rpb[:, :, :WIN_C - 1]], axis=-1).astype(F32)
    n_pat = WIN_R
    return pl.pallas_call(
        _bias_kernel,
        grid=(n_pat,),
        in_specs=[pl.BlockSpec(rp.shape, lambda p: (0, 0, 0))],
        out_specs=pl.BlockSpec((None, nh // 2, 2 * GRID_W, win_r * GRID_W), lambda p: (p, 0, 0, 0)),
        out_shape=jax.ShapeDtypeStruct((n_pat, nh // 2, 2 * GRID_W, win_r * GRID_W), F32),
        compiler_params=pltpu.CompilerParams(
            dimension_semantics=("parallel",), vmem_limit_bytes=VMEM_LIMIT),
        name="bias_table",
    )(rp)


def _tail_kernel(final, x_ref, mod_ref, hf_ref, hb_ref, za_ref, zb_ref, ga_ref, gb_ref, att_ref,
                 wpa_ref, wpb_ref, wo_ref, gf_ref, o_ref, w_scr):
    d = x_ref.shape[1]

    @pl.when(pl.program_id(0) == 0)
    def _():
        for n, w_ref in enumerate((wpa_ref, wpb_ref, wo_ref)):
            w_scr[n] = w_ref[...].astype(BF16)

    for s in range(x_ref.shape[0] // TAIL_SUB):
        rows = slice(s * TAIL_SUB, (s + 1) * TAIL_SUB)
        h = hf_ref[rows, :].astype(F32) + hb_ref[rows, :].astype(F32)
        ya = (h * za_ref[rows, :].astype(F32)).astype(BF16)
        ya = jnp.dot(ya, w_scr[0], preferred_element_type=F32)
        yb = (att_ref[rows, :].astype(F32) * zb_ref[rows, :].astype(F32)).astype(BF16)
        yb = jnp.dot(yb, w_scr[1], preferred_element_type=F32)
        merged = ga_ref[rows, :].astype(F32) * ya + gb_ref[rows, :].astype(F32) * yb
        res = jnp.dot(merged.astype(BF16), w_scr[2], preferred_element_type=F32)
        xo = x_ref[rows, :] + mod_ref[:, 2 * d:3 * d] * res
        if final:
            ms = jnp.mean(xo * xo, axis=-1, keepdims=True)
            xo = xo * lax.rsqrt(ms + EPS) * gf_ref[...]
        o_ref[rows, :] = xo


def _tail(x2, mod3, hf, hb, proj, att2, w_pa, w_pb, w_o, g_final, seq, cols, final):
    n, d = x2.shape
    za_col, zb_col, ga_col, gb_col = cols
    tok = lambda i: (i, 0)
    wspec = pl.BlockSpec((d, d), lambda i: (0, 0), pipeline_mode=pl.Buffered(1))
    return pl.pallas_call(
        functools.partial(_tail_kernel, final),
        grid=(n // TAIL_TM,),
        in_specs=[
            pl.BlockSpec((TAIL_TM, d), tok),
            pl.BlockSpec((None, 1, 3 * d), lambda i: ((i * TAIL_TM) // seq, 0, 0)),
            pl.BlockSpec((TAIL_TM, d), tok),
            pl.BlockSpec((TAIL_TM, d), tok),
            pl.BlockSpec((TAIL_TM, d), lambda i: (i, za_col)),
            pl.BlockSpec((TAIL_TM, d), lambda i: (i, zb_col)),
            pl.BlockSpec((TAIL_TM, d), lambda i: (i, ga_col)),
            pl.BlockSpec((TAIL_TM, d), lambda i: (i, gb_col)),
            pl.BlockSpec((TAIL_TM, d), tok),
            wspec, wspec, wspec,
            pl.BlockSpec((1, d), lambda i: (0, 0)),
        ],
        out_specs=pl.BlockSpec((TAIL_TM, d), tok),
        out_shape=jax.ShapeDtypeStruct((n, d), F32),
        scratch_shapes=[pltpu.VMEM((3, d, d), BF16)],
        compiler_params=pltpu.CompilerParams(
            dimension_semantics=("arbitrary",), vmem_limit_bytes=VMEM_LIMIT),
        name="tail",
    )(x2, mod3, hf, hb, proj, proj, proj, proj, att2, w_pa, w_pb, w_o, g_final)


def kernel(x, c, g_pre, w_c, b_c, w_in, conv_w, conv_b, w_r_f, b_r_f, w_i_f, b_i_f, lam_f,
           w_r_b, b_r_b, w_i_b, b_i_b, lam_b, rpb, w_pa, w_pb, w_o, g_final):
    nb, seq, d = x.shape
    depth = g_pre.shape[0]
    n = nb * seq
    rows = seq // GRID_W
    groups = w_in.shape[2] // d
    xa_col, za_col, q_col, k_col, v_col, zb_col, ga_col, gb_col = range(groups)
    assert xa_col == 0
    acts = ["id"] * groups
    acts[za_col] = acts[zb_col] = "silu"
    acts[ga_col] = acts[gb_col] = "sigmoid"
    acts[q_col] = "qscale"

    c_pad = jnp.zeros((MOD_ROWS, d), F32).at[:nb].set(c)
    x2 = x.reshape(n, d)
    for l in range(depth):
        mod = _modulation(c_pad, w_c[l], b_c[l].reshape(1, 3 * d))
        mod3 = mod[:nb].reshape(nb, 1, 3 * d)

        proj = _projection(x2, mod3, g_pre[l].reshape(1, d), w_in[l], seq, tuple(acts))
        proj3 = proj.reshape(nb, seq, groups * d)

        cw, cb = conv_w[l], conv_b[l].reshape(1, d)
        hf, xh = _rglru(proj3, cw, cb, jnp.stack([w_r_f[l], w_i_f[l]]),
                        jnp.stack([b_r_f[l], b_i_f[l]]).reshape(2, 1, d), lam_f[l].reshape(1, d),
                        False)
        hb = _rglru(proj3, cw, cb, jnp.stack([w_r_b[l], w_i_b[l]]),
                    jnp.stack([b_r_b[l], b_i_b[l]]).reshape(2, 1, d), lam_b[l].reshape(1, d), True,
                    xh=xh)

        att = _natten(proj3, _bias_table(rpb[l], rows), q_col, k_col, v_col)

        x2 = _tail(x2, mod3, hf.reshape(n, d), hb.reshape(n, d), proj, att.reshape(n, d),
                   w_pa[l], w_pb[l], w_o[l],
                   g_final.reshape(1, d), seq, (za_col, zb_col, ga_col, gb_col),
                   final=(l == depth - 1))
    return x2.reshape(nb, seq, d)
```

```python
import functools

import jax
import jax.numpy as jnp
import numpy as np
from jax import lax
from jax.experimental import pallas as pl
from jax.experimental.pallas import tpu as pltpu

F32 = jnp.float32
BF16 = jnp.bfloat16

SUBLANES = 8
LANES = 128
MXU_TILE = 256

GRID_W = 64
A_BLOCKS = 16
CONV_W = 4
RG_C = 8.0
N_HEADS = 16
HEAD_DIM = 64
WIN_R = 8
WIN_C = 16
EPS = 1e-6
MASK_BIAS = -1e30
LOG2_E = 1.4426950408889634
RSQRT_FLOOR = 1e-30

PROJ_TM = 512
NORM_RB = 64
CAST_RB = 128
SCAN_TC = 128
TAIL_TM = 512
TAIL_SUB = 256
NAT_ROWS = 8
MOD_ROWS = 16

VMEM_LIMIT = 56 * 1024 * 1024
NAT_VMEM_LIMIT = 61 * 1024 * 1024


def _sigmoid(x):
    return 0.5 * jnp.tanh(0.5 * x) + 0.5


def _silu(x):
    return x * _sigmoid(x)


_ACTIVATIONS = {
    "id": lambda v: v,
    "silu": _silu,
    "sigmoid": _sigmoid,
    "qscale": lambda v: v * (HEAD_DIM ** -0.5 * LOG2_E),
}


def _mod_kernel(c_ref, w_ref, b_ref, o_ref):
    sc = _silu(c_ref[...]).astype(BF16)
    w = w_ref[...].astype(BF16)
    o_ref[...] = jnp.dot(sc, w, preferred_element_type=F32) + b_ref[...]


def _modulation(c_pad, w_c, b_c):
    d = w_c.shape[0]
    return pl.pallas_call(
        _mod_kernel,
        grid=(3,),
        in_specs=[
            pl.BlockSpec((MOD_ROWS, d), lambda j: (0, 0)),
            pl.BlockSpec((d, d), lambda j: (0, j)),
            pl.BlockSpec((1, d), lambda j: (0, j)),
        ],
        out_specs=pl.BlockSpec((MOD_ROWS, d), lambda j: (0, j)),
        out_shape=jax.ShapeDtypeStruct((MOD_ROWS, 3 * d), F32),
        compiler_params=pltpu.CompilerParams(
            dimension_semantics=("parallel",), vmem_limit_bytes=VMEM_LIMIT),
        name="mod",
    )(c_pad, w_c, b_c)


def _norm_rows(x_ref, mod_ref, g, h_ref, slot, r0):
    d = x_ref.shape[1]
    x = x_ref[pl.ds(r0, NORM_RB), :]
    ms = jnp.mean(x * x, axis=-1, keepdims=True)
    y = x * lax.rsqrt(ms + EPS) * g
    h = y * (1.0 + mod_ref[:, d:2 * d]) + mod_ref[:, 0:d]
    h_ref[slot, pl.ds(r0, NORM_RB), :] = h.astype(BF16)


def _proj_kernel(acts, x0_ref, mod0_ref, xn_ref, modn_ref, g_ref, w_hbm, o_ref,
                 h_ref, w_ref, stage_ref, sem):
    i = pl.program_id(0)
    tm, d = xn_ref.shape
    g = g_ref[...]

    def weight_copy(j):
        return pltpu.make_async_copy(w_hbm.at[:, pl.ds(j * d, d)], stage_ref.at[j % 2],
                                     sem.at[j % 2])

    @pl.when(i == 0)
    def _():
        weight_copy(0).start()
        for j in range(len(acts)):
            if j + 1 < len(acts):
                weight_copy(j + 1).start()
            weight_copy(j).wait()

            def cast(k, carry, j=j):
                r0 = pl.multiple_of(k * CAST_RB, CAST_RB)
                w_ref[pl.ds(r0, CAST_RB), j * d:(j + 1) * d] = (
                    stage_ref[j % 2, pl.ds(r0, CAST_RB), :].astype(BF16))
                return carry
            lax.fori_loop(0, d // CAST_RB, cast, 0)

        def body(k, carry):
            _norm_rows(x0_ref, mod0_ref, g, h_ref, 0, pl.multiple_of(k * NORM_RB, NORM_RB))
            return carry
        lax.fori_loop(0, tm // NORM_RB, body, 0)

    cur = i % 2
    for k in range(tm // NORM_RB):
        _norm_rows(xn_ref, modn_ref, g, h_ref, 1 - cur, k * NORM_RB)
    for j, act in enumerate(acts):
        cols = slice(j * d, (j + 1) * d)
        acc = jnp.dot(h_ref[cur], w_ref[:, cols], preferred_element_type=F32)
        o_ref[:, cols] = _ACTIVATIONS[act](acc).astype(BF16)


def _projection(x2, mod3, g_pre, w_in, seq, acts):
    n, d = x2.shape
    n_tiles = n // PROJ_TM
    nxt = lambda i: jnp.minimum(i + 1, n_tiles - 1)
    return pl.pallas_call(
        functools.partial(_proj_kernel, acts),
        grid=(n_tiles,),
        in_specs=[
            pl.BlockSpec((PROJ_TM, d), lambda i: (0, 0), pipeline_mode=pl.Buffered(1)),
            pl.BlockSpec((None, 1, 3 * d), lambda i: (0, 0, 0)),
            pl.BlockSpec((PROJ_TM, d), lambda i: (nxt(i), 0)),
            pl.BlockSpec((None, 1, 3 * d), lambda i: ((nxt(i) * PROJ_TM) // seq, 0, 0)),
            pl.BlockSpec((1, d), lambda i: (0, 0)),
            pl.BlockSpec(memory_space=pl.ANY),
        ],
        out_specs=pl.BlockSpec((PROJ_TM, w_in.shape[1]), lambda i: (i, 0)),
        out_shape=jax.ShapeDtypeStruct((n, w_in.shape[1]), BF16),
        scratch_shapes=[
            pltpu.VMEM((2, PROJ_TM, d), BF16),
            pltpu.VMEM(w_in.shape, BF16),
            pltpu.VMEM((2, d, d), w_in.dtype),
            pltpu.SemaphoreType.DMA((2,)),
        ],
        compiler_params=pltpu.CompilerParams(
            dimension_semantics=("arbitrary",), vmem_limit_bytes=VMEM_LIMIT),
        name="proj",
    )(x2, mod3, x2, mod3, g_pre, w_in)


def _rglru_kernel(rev, reuse, *refs):
    if reuse:
        xh_ref, wblk_ref, bg_ref, lam_ref, o_ref = refs[:5]
        wg_ref, sa_ref, su_ref, sh_ref, carry_ref = refs[5:]
        nb, tc, d = xh_ref.shape
    else:
        (xa_ref, prev_ref, next_ref, shift_ref, cw_ref, cb_ref, wblk_ref, bg_ref, lam_ref,
         o_ref, xh_out_ref) = refs[:11]
        wg_ref, sa_ref, su_ref, sh_ref, carry_ref = refs[11:]
        nb, tc, d = xa_ref.shape
    i = pl.program_id(0)
    n_chunks = pl.num_programs(0) - 1
    groups = tc // SUBLANES
    n_tiles = d // MXU_TILE
    n_gates, n_blk, bw, _ = wblk_ref.shape
    per_tile = MXU_TILE // bw
    ig = jnp.minimum(i, n_chunks - 1)
    ci = n_chunks - 1 - ig if rev else ig
    cur = i % 2
    old = 1 - cur

    @pl.when(i == 0)
    def _():
        carry_ref[...] = jnp.zeros_like(carry_ref)
        sa_ref[1] = jnp.zeros(sa_ref.shape[1:], F32)
        su_ref[1] = jnp.zeros(su_ref.shape[1:], F32)
        wg_ref[...] = jnp.zeros_like(wg_ref)
        for gate in range(n_gates):
            for blk in range(n_blk):
                off = (blk % per_tile) * bw
                wg_ref[gate, blk // per_tile, off:off + bw, off:off + bw] = (
                    wblk_ref[gate, blk].astype(BF16))

    has_prev = ci > 0
    has_next = ci < n_chunks - 1
    lam = lam_ref[...]
    softplus_neg_lam = jnp.maximum(-lam, 0.0) + jnp.log1p(jnp.exp(-jnp.abs(lam)))
    c8l = (-0.5 * RG_C * LOG2_E) * softplus_neg_lam
    brh = 0.5 * bg_ref[0]
    bih = 0.5 * bg_ref[1]
    if not reuse:
        cwh = 0.5 * cw_ref[...]
        cbh = 0.5 * cb_ref[...]
        shift_m = shift_ref[...]
        taps = [k for k in range(CONV_W) if k != CONV_W // 2]

    def scan_row(b):
        h = carry_ref[b]
        for gg in range(groups):
            g = groups - 1 - gg if rev else gg
            for jj in range(SUBLANES):
                j = SUBLANES - 1 - jj if rev else jj
                tok = pl.ds(j, SUBLANES, stride=SUBLANES)
                h = sa_ref[old, b, g, tok, :] * h + su_ref[old, b, g, tok, :]
                sh_ref[b, g, tok, :] = h
            if gg % 2 == 1:
                gp = g // 2
                r0 = gp * 2 * SUBLANES
                for cblk in range(d // LANES):
                    blk = slice(cblk * SUBLANES, (cblk + 1) * SUBLANES)
                    rows16 = jnp.concatenate(
                        [sh_ref[b, 2 * gp, blk, :], sh_ref[b, 2 * gp + 1, blk, :]], axis=0)
                    o_ref[b, pl.ds(r0, 2 * SUBLANES), cblk * LANES:(cblk + 1) * LANES] = (
                        rows16.astype(o_ref.dtype))
        carry_ref[b] = h

    def conv_half(b):
        xb = xa_ref[b]
        halo = jnp.concatenate([
            jnp.where(has_prev, prev_ref[b], jnp.zeros_like(prev_ref[b])),
            jnp.where(has_next, next_ref[b], jnp.zeros_like(next_ref[b]))], axis=0)
        sh = jnp.dot(shift_m, jnp.concatenate([xb, halo], axis=0), preferred_element_type=F32)
        xh = cbh + cwh[CONV_W // 2:CONV_W // 2 + 1] * xb.astype(F32)
        for n, k in enumerate(taps):
            xh = xh + cwh[k:k + 1] * sh[n * tc:(n + 1) * tc]
        return xh

    def gates(b):
        if reuse:
            xhb = xh_ref[b]
            xh = xhb.astype(F32)
        else:
            xh = conv_half(b)
            xhb = xh.astype(BF16)
            xh_out_ref[b] = xhb
        for k in range(n_tiles):
            sl = slice(k * MXU_TILE, (k + 1) * MXU_TILE)
            xk = xhb[:, sl]
            tr = jnp.tanh(jnp.dot(xk, wg_ref[0, k], preferred_element_type=F32) + brh[:, sl])
            ti = jnp.tanh(jnp.dot(xk, wg_ref[1, k], preferred_element_type=F32) + bih[:, sl])
            a = jnp.exp2(c8l[:, sl] * (tr + 1.0))
            z = 1.0 - a * a
            u = (z * lax.rsqrt(jnp.maximum(z, RSQRT_FLOOR))) * ((ti + 1.0) * xh[:, sl])
            for c2 in range(MXU_TILE // LANES):
                cblk = k * (MXU_TILE // LANES) + c2
                rows = slice(cblk * SUBLANES, (cblk + 1) * SUBLANES)
                lanes = slice(c2 * LANES, (c2 + 1) * LANES)
                sa_ref[cur, b, :, rows, :] = a[:, lanes].reshape(groups, SUBLANES, LANES)
                su_ref[cur, b, :, rows, :] = u[:, lanes].reshape(groups, SUBLANES, LANES)

    def body(b, carry):
        scan_row(b)
        gates(b)
        return carry

    lax.fori_loop(0, nb, body, 0)


def _shift_matrix(tc):
    taps = [k for k in range(CONV_W) if k != CONV_W // 2]
    m = np.zeros((len(taps) * tc, tc + 2 * SUBLANES), np.float32)
    for n, k in enumerate(taps):
        for t in range(tc):
            s = t - CONV_W // 2 + k
            if s < 0:
                col = tc + SUBLANES + s
            elif s >= tc:
                col = tc + SUBLANES + (s - tc)
            else:
                col = s
            m[n * tc + t, col] = 1.0
    return m


def _rglru(proj3, conv_w, conv_b, wblk, bg, lam, rev, xh=None):
    nb, seq, _ = proj3.shape
    reuse = xh is not None
    d = wblk.shape[1] * wblk.shape[2]
    n_chunks = seq // SCAN_TC
    n_groups = seq // SUBLANES
    gpc = SCAN_TC // SUBLANES

    def chunk(i):
        return n_chunks - 1 - i if rev else i

    def gated(i):
        return chunk(jnp.minimum(i, n_chunks - 1))

    def scanned(i):
        return chunk(jnp.maximum(i - 1, 0))

    n_tiles = d // MXU_TILE
    n_gates = wblk.shape[0]
    whole = lambda a: pl.BlockSpec(a.shape, lambda i: (0,) * a.ndim)
    gated_spec = pl.BlockSpec((nb, SCAN_TC, d), lambda i: (0, gated(i), 0))
    h_spec = pl.BlockSpec((nb, SCAN_TC, d), lambda i: (0, scanned(i), 0))
    h_shape = jax.ShapeDtypeStruct((nb, seq, d), BF16)
    if reuse:
        in_specs = [gated_spec, whole(wblk), whole(bg), whole(lam)]
        args = (xh, wblk, bg, lam)
        out_specs, out_shape = h_spec, h_shape
    else:
        proj4 = proj3.reshape(nb, n_groups, SUBLANES, proj3.shape[2])
        shift_m = jnp.asarray(_shift_matrix(SCAN_TC), BF16)
        in_specs = [
            gated_spec,
            pl.BlockSpec((nb, None, SUBLANES, d),
                         lambda i: (0, jnp.maximum(gated(i) * gpc - 1, 0), 0, 0)),
            pl.BlockSpec((nb, None, SUBLANES, d),
                         lambda i: (0, jnp.minimum((gated(i) + 1) * gpc, n_groups - 1), 0, 0)),
            whole(shift_m), whole(conv_w), whole(conv_b), whole(wblk), whole(bg), whole(lam),
        ]
        args = (proj3, proj4, proj4, shift_m, conv_w, conv_b, wblk, bg, lam)
        out_specs, out_shape = [h_spec, gated_spec], [h_shape, h_shape]
    return pl.pallas_call(
        functools.partial(_rglru_kernel, rev, reuse),
        grid=(n_chunks + 1,),
        in_specs=in_specs,
        out_specs=out_specs,
        out_shape=out_shape,
        scratch_shapes=[
            pltpu.VMEM((n_gates, n_tiles, MXU_TILE, MXU_TILE), BF16),
            pltpu.VMEM((2, nb, gpc, d // LANES * SUBLANES, LANES), F32),
            pltpu.VMEM((2, nb, gpc, d // LANES * SUBLANES, LANES), F32),
            pltpu.VMEM((nb, gpc, d // LANES * SUBLANES, LANES), F32),
            pltpu.VMEM((nb, SUBLANES, LANES), F32),
        ],
        compiler_params=pltpu.CompilerParams(
            dimension_semantics=("arbitrary",), vmem_limit_bytes=VMEM_LIMIT),
        name="rglru_bwd" if rev else "rglru_fwd",
    )(*args)


def _natten_kernel(k_col, v_col, q_ref, kv_hbm, bias_ref, o_ref, k_buf, v_buf, sem):
    batch = pl.program_id(0)
    step = pl.program_id(1)
    n_steps = pl.num_programs(1)
    rows = n_steps * NAT_ROWS
    gw = GRID_W
    d = q_ref.shape[1]
    win_r = min(WIN_R, rows)
    held = k_buf.shape[1] // gw

    def first_held_row(s):
        return jnp.clip(s * NAT_ROWS - win_r // 2, 0, rows - held)

    def window_copies(b, s, slot):
        tok = pl.multiple_of(first_held_row(s) * gw, gw)
        return [pltpu.make_async_copy(kv_hbm.at[b, pl.ds(tok, held * gw), pl.ds(col * d, d)],
                                      buf.at[slot], sem.at[n, slot])
                for n, (col, buf) in enumerate(((k_col, k_buf), (v_col, v_buf)))]

    t = batch * n_steps + step
    slot = t % 2

    @pl.when(t == 0)
    def _():
        for c in window_copies(0, 0, 0):
            c.start()

    @pl.when(t + 1 < pl.num_programs(0) * n_steps)
    def _():
        wrap = step + 1 == n_steps
        for c in window_copies(jnp.where(wrap, batch + 1, batch), jnp.where(wrap, 0, step + 1),
                               1 - slot):
            c.start()

    for c in window_copies(batch, step, slot):
        c.wait()

    lane = lax.broadcasted_iota(jnp.int32, (gw, LANES), 1)
    lo = lane < HEAD_DIM
    zero = jnp.zeros((gw, LANES), BF16)
    n_pairs = d // LANES
    w0 = first_held_row(step)
    wins = []
    pats = []
    scores = []
    for rr in range(NAT_ROWS):
        r = step * NAT_ROWS + rr
        r0 = jnp.clip(r - win_r // 2, 0, rows - win_r)
        k0 = pl.multiple_of((r0 - w0) * gw, gw)
        wins.append(k0)
        pats.append(r0 - r + (WIN_R - 1))
        for p in range(n_pairs):
            sl = slice(p * LANES, (p + 1) * LANES)
            qp = q_ref[rr * gw:(rr + 1) * gw, sl]
            qs = jnp.concatenate([jnp.where(lo, qp, zero), jnp.where(lo, zero, qp)], axis=0)
            scores.append(lax.dot_general(qs, k_buf[slot, pl.ds(k0, win_r * gw), sl],
                                          (((1,), (1,)), ((), ())), preferred_element_type=F32))
    probs = []
    for rr in range(NAT_ROWS):
        for p in range(n_pairs):
            s = scores[rr * n_pairs + p] + bias_ref[pats[rr], p]
            m = jnp.max(s, axis=-1, keepdims=True)
            probs.append(jnp.exp2(s - m).astype(BF16))
    ones = jnp.ones((win_r * gw, LANES), BF16)
    for rr in range(NAT_ROWS):
        for p in range(n_pairs):
            sl = slice(p * LANES, (p + 1) * LANES)
            v_ext = jnp.concatenate([v_buf[slot, pl.ds(wins[rr], win_r * gw), sl], ones], axis=1)
            pv = jnp.dot(probs[rr * n_pairs + p], v_ext, preferred_element_type=F32)
            o = pv[:, :LANES] / pv[:, LANES:]
            o_ref[rr * gw:(rr + 1) * gw, sl] = jnp.where(lo, o[:gw], o[gw:]).astype(BF16)


def _natten(proj3, bias_tab, q_col, k_col, v_col):
    nb, seq, _ = proj3.shape
    d = N_HEADS * HEAD_DIM
    rows = seq // GRID_W
    held = min(rows, NAT_ROWS + min(WIN_R, rows))
    return pl.pallas_call(
        functools.partial(_natten_kernel, k_col, v_col),
        grid=(nb, rows // NAT_ROWS),
        in_specs=[
            pl.BlockSpec((None, NAT_ROWS * GRID_W, d), lambda b, i: (b, i, q_col)),
            pl.BlockSpec(memory_space=pl.ANY),
            pl.BlockSpec(bias_tab.shape, lambda b, i: (0, 0, 0, 0), pipeline_mode=pl.Buffered(1)),
        ],
        out_specs=pl.BlockSpec((None, NAT_ROWS * GRID_W, d), lambda b, i: (b, i, 0)),
        out_shape=jax.ShapeDtypeStruct((nb, seq, d), BF16),
        scratch_shapes=[pltpu.VMEM((2, held * GRID_W, d), BF16),
                        pltpu.VMEM((2, held * GRID_W, d), BF16),
                        pltpu.SemaphoreType.DMA((2, 2))],
        compiler_params=pltpu.CompilerParams(
            dimension_semantics=("arbitrary", "arbitrary"), vmem_limit_bytes=NAT_VMEM_LIMIT),
        name="natten",
    )(proj3, proj3, bias_tab)


def _bias_kernel(rp_ref, o_ref):
    d0 = pl.program_id(0)
    n_pairs, _, keys = o_ref.shape
    gw = GRID_W
    q_idx = lax.broadcasted_iota(jnp.int32, (gw, LANES), 0)
    lane = lax.broadcasted_iota(jnp.int32, (gw, LANES), 1)
    col = jnp.where(lane < gw, lane, lane - gw)
    c0 = jnp.clip(q_idx - WIN_C // 2, 0, gw - WIN_C)
    valid = (col >= c0) & (col < c0 + WIN_C)
    first = lane < gw
    for pair in range(n_pairs):
        for hh in range(2):
            h = 2 * pair + hh
            for t in range(keys // LANES):
                va = jnp.broadcast_to(rp_ref[h, pl.ds(d0 + 2 * t, 1), :], (gw, LANES))
                vb = jnp.broadcast_to(rp_ref[h, pl.ds(d0 + 2 * t + 1, 1), :], (gw, LANES))
                ta = pltpu.roll(va, 0, 1, stride=1, stride_axis=0)
                tb = pltpu.roll(vb, gw, 1, stride=1, stride_axis=0)
                tile = jnp.where(valid, LOG2_E * jnp.where(first, ta, tb), MASK_BIAS)
                o_ref[pair, hh * gw:(hh + 1) * gw, t * LANES:(t + 1) * LANES] = tile


def _bias_table(rpb, rows):
    win_r = min(WIN_R, rows)
    assert 2 * GRID_W == LANES and win_r % 2 == 0
    nh, nd, _ = rpb.shape
    rp = jnp.concatenate([rpb[:, :, WIN_C - 1:],
                          jnp.zeros((nh, nd, LANES - (2 * WIN_C - 1)), rpb.dtype),
                          rpb[:, :, :WIN_C - 1]], axis=-1).astype(F32)
    n_pat = WIN_R
    return pl.pallas_call(
        _bias_kernel,
        grid=(n_pat,),
        in_specs=[pl.BlockSpec(rp.shape, lambda p: (0, 0, 0))],
        out_specs=pl.BlockSpec((None, nh // 2, 2 * GRID_W, win_r * GRID_W), lambda p: (p, 0, 0, 0)),
        out_shape=jax.ShapeDtypeStruct((n_pat, nh // 2, 2 * GRID_W, win_r * GRID_W), F32),
        compiler_params=pltpu.CompilerParams(
            dimension_semantics=("parallel",), vmem_limit_bytes=VMEM_LIMIT),
        name="bias_table",
    )(rp)


def _tail_kernel(final, x_ref, mod_ref, hf_ref, hb_ref, za_ref, zb_ref, ga_ref, gb_ref, att_ref,
                 wpa_ref, wpb_ref, wo_ref, gf_ref, o_ref, w_scr):
    d = x_ref.shape[1]

    @pl.when(pl.program_id(0) == 0)
    def _():
        for n, w_ref in enumerate((wpa_ref, wpb_ref, wo_ref)):
            w_scr[n] = w_ref[...].astype(BF16)

    for s in range(x_ref.shape[0] // TAIL_SUB):
        rows = slice(s * TAIL_SUB, (s + 1) * TAIL_SUB)
        h = hf_ref[rows, :].astype(F32) + hb_ref[rows, :].astype(F32)
        ya = (h * za_ref[rows, :].astype(F32)).astype(BF16)
        ya = jnp.dot(ya, w_scr[0], preferred_element_type=F32)
        yb = (att_ref[rows, :].astype(F32) * zb_ref[rows, :].astype(F32)).astype(BF16)
        yb = jnp.dot(yb, w_scr[1], preferred_element_type=F32)
        merged = ga_ref[rows, :].astype(F32) * ya + gb_ref[rows, :].astype(F32) * yb
        res = jnp.dot(merged.astype(BF16), w_scr[2], preferred_element_type=F32)
        xo = x_ref[rows, :] + mod_ref[:, 2 * d:3 * d] * res
        if final:
            ms = jnp.mean(xo * xo, axis=-1, keepdims=True)
            xo = xo * lax.rsqrt(ms + EPS) * gf_ref[...]
        o_ref[rows, :] = xo


def _tail(x2, mod3, hf, hb, proj, att2, w_pa, w_pb, w_o, g_final, seq, cols, final):
    n, d = x2.shape
    za_col, zb_col, ga_col, gb_col = cols
    tok = lambda i: (i, 0)
    wspec = pl.BlockSpec((d, d), lambda i: (0, 0), pipeline_mode=pl.Buffered(1))
    return pl.pallas_call(
        functools.partial(_tail_kernel, final),
        grid=(n // TAIL_TM,),
        in_specs=[
            pl.BlockSpec((TAIL_TM, d), tok),
            pl.BlockSpec((None, 1, 3 * d), lambda i: ((i * TAIL_TM) // seq, 0, 0)),
            pl.BlockSpec((TAIL_TM, d), tok),
            pl.BlockSpec((TAIL_TM, d), tok),
            pl.BlockSpec((TAIL_TM, d), lambda i: (i, za_col)),
            pl.BlockSpec((TAIL_TM, d), lambda i: (i, zb_col)),
            pl.BlockSpec((TAIL_TM, d), lambda i: (i, ga_col)),
            pl.BlockSpec((TAIL_TM, d), lambda i: (i, gb_col)),
            pl.BlockSpec((TAIL_TM, d), tok),
            wspec, wspec, wspec,
            pl.BlockSpec((1, d), lambda i: (0, 0)),
        ],
        out_specs=pl.BlockSpec((TAIL_TM, d), tok),
        out_shape=jax.ShapeDtypeStruct((n, d), F32),
        scratch_shapes=[pltpu.VMEM((3, d, d), BF16)],
        compiler_params=pltpu.CompilerParams(
            dimension_semantics=("arbitrary",), vmem_limit_bytes=VMEM_LIMIT),
        name="tail",
    )(x2, mod3, hf, hb, proj, proj, proj, proj, att2, w_pa, w_pb, w_o, g_final)


def kernel(x, c, g_pre, w_c, b_c, w_in, conv_w, conv_b, w_r_f, b_r_f, w_i_f, b_i_f, lam_f,
           w_r_b, b_r_b, w_i_b, b_i_b, lam_b, rpb, w_pa, w_pb, w_o, g_final):
    nb, seq, d = x.shape
    depth = g_pre.shape[0]
    n = nb * seq
    rows = seq // GRID_W
    groups = w_in.shape[2] // d
    xa_col, za_col, q_col, k_col, v_col, zb_col, ga_col, gb_col = range(groups)
    assert xa_col == 0
    acts = ["id"] * groups
    acts[za_col] = acts[zb_col] = "silu"
    acts[ga_col] = acts[gb_col] = "sigmoid"
    acts[q_col] = "qscale"

    c_pad = jnp.zeros((MOD_ROWS, d), F32).at[:nb].set(c)
    x2 = x.reshape(n, d)
    for l in range(depth):
        mod = _modulation(c_pad, w_c[l], b_c[l].reshape(1, 3 * d))
        mod3 = mod[:nb].reshape(nb, 1, 3 * d)

        proj = _projection(x2, mod3, g_pre[l].reshape(1, d), w_in[l], seq, tuple(acts))
        proj3 = proj.reshape(nb, seq, groups * d)

        cw, cb = conv_w[l], conv_b[l].reshape(1, d)
        hf, xh = _rglru(proj3, cw, cb, jnp.stack([w_r_f[l], w_i_f[l]]),
                        jnp.stack([b_r_f[l], b_i_f[l]]).reshape(2, 1, d), lam_f[l].reshape(1, d),
                        False)
        hb = _rglru(proj3, cw, cb, jnp.stack([w_r_b[l], w_i_b[l]]),
                    jnp.stack([b_r_b[l], b_i_b[l]]).reshape(2, 1, d), lam_b[l].reshape(1, d), True,
                    xh=xh)

        att = _natten(proj3, _bias_table(rpb[l], rows), q_col, k_col, v_col)

        x2 = _tail(x2, mod3, hf.reshape(n, d), hb.reshape(n, d), proj, att.reshape(n, d),
                   w_pa[l], w_pb[l], w_o[l],
                   g_final.reshape(1, d), seq, (za_col, zb_col, ga_col, gb_col),
                   final=(l == depth - 1))
    return x2.reshape(nb, seq, d)
```

```python
import functools

import jax
import jax.numpy as jnp
import numpy as np
from jax import lax
from jax.experimental import pallas as pl
from jax.experimental.pallas import tpu as pltpu

F32 = jnp.float32
BF16 = jnp.bfloat16

SUBLANES = 8
LANES = 128
MXU_TILE = 256

GRID_W = 64
A_BLOCKS = 16
CONV_W = 4
RG_C = 8.0
N_HEADS = 16
HEAD_DIM = 64
WIN_R = 8
WIN_C = 16
EPS = 1e-6
MASK_BIAS = -1e30
LOG2_E = 1.4426950408889634
RSQRT_FLOOR = 1e-30

PROJ_TM = 512
NORM_RB = 64
CAST_RB = 128
SCAN_TC = 128
TAIL_TM = 512
TAIL_SUB = 256
NAT_ROWS = 8
MOD_ROWS = 16

VMEM_LIMIT = 56 * 1024 * 1024
NAT_VMEM_LIMIT = 61 * 1024 * 1024


def _sigmoid(x):
    return 0.5 * jnp.tanh(0.5 * x) + 0.5


def _silu(x):
    return x * _sigmoid(x)


_ACTIVATIONS = {
    "id": lambda v: v,
    "silu": _silu,
    "sigmoid": _sigmoid,
    "qscale": lambda v: v * (HEAD_DIM ** -0.5 * LOG2_E),
}


def _mod_kernel(c_ref, w_ref, b_ref, o_ref):
    sc = _silu(c_ref[...]).astype(BF16)
    w = w_ref[...].astype(BF16)
    o_ref[...] = jnp.dot(sc, w, preferred_element_type=F32) + b_ref[...]


def _modulation(c_pad, w_c, b_c):
    d = w_c.shape[0]
    return pl.pallas_call(
        _mod_kernel,
        grid=(3,),
        in_specs=[
            pl.BlockSpec((MOD_ROWS, d), lambda j: (0, 0)),
            pl.BlockSpec((d, d), lambda j: (0, j)),
            pl.BlockSpec((1, d), lambda j: (0, j)),
        ],
        out_specs=pl.BlockSpec((MOD_ROWS, d), lambda j: (0, j)),
        out_shape=jax.ShapeDtypeStruct((MOD_ROWS, 3 * d), F32),
        compiler_params=pltpu.CompilerParams(
            dimension_semantics=("parallel",), vmem_limit_bytes=VMEM_LIMIT),
        name="mod",
    )(c_pad, w_c, b_c)


def _norm_rows(x_ref, mod_ref, g, h_ref, slot, r0):
    d = x_ref.shape[1]
    x = x_ref[pl.ds(r0, NORM_RB), :]
    ms = jnp.mean(x * x, axis=-1, keepdims=True)
    y = x * lax.rsqrt(ms + EPS) * g
    h = y * (1.0 + mod_ref[:, d:2 * d]) + mod_ref[:, 0:d]
    h_ref[slot, pl.ds(r0, NORM_RB), :] = h.astype(BF16)


def _proj_kernel(acts, x0_ref, mod0_ref, xn_ref, modn_ref, g_ref, w_hbm, o_ref,
                 h_ref, w_ref, stage_ref, sem):
    i = pl.program_id(0)
    tm, d = xn_ref.shape
    g = g_ref[...]

    def weight_copy(j):
        return pltpu.make_async_copy(w_hbm.at[:, pl.ds(j * d, d)], stage_ref.at[j % 2],
                                     sem.at[j % 2])

    @pl.when(i == 0)
    def _():
        weight_copy(0).start()
        for j in range(len(acts)):
            if j + 1 < len(acts):
                weight_copy(j + 1).start()
            weight_copy(j).wait()

            def cast(k, carry, j=j):
                r0 = pl.multiple_of(k * CAST_RB, CAST_RB)
                w_ref[pl.ds(r0, CAST_RB), j * d:(j + 1) * d] = (
                    stage_ref[j % 2, pl.ds(r0, CAST_RB), :].astype(BF16))
                return carry
            lax.fori_loop(0, d // CAST_RB, cast, 0)

        def body(k, carry):
            _norm_rows(x0_ref, mod0_ref, g, h_ref, 0, pl.multiple_of(k * NORM_RB, NORM_RB))
            return carry
        lax.fori_loop(0, tm // NORM_RB, body, 0)

    cur = i % 2
    for k in range(tm // NORM_RB):
        _norm_rows(xn_ref, modn_ref, g, h_ref, 1 - cur, k * NORM_RB)
    for j, act in enumerate(acts):
        cols = slice(j * d, (j + 1) * d)
        acc = jnp.dot(h_ref[cur], w_ref[:, cols], preferred_element_type=F32)
        o_ref[:, cols] = _ACTIVATIONS[act](acc).astype(BF16)


def _projection(x2, mod3, g_pre, w_in, seq, acts):
    n, d = x2.shape
    n_tiles = n // PROJ_TM
    nxt = lambda i: jnp.minimum(i + 1, n_tiles - 1)
    return pl.pallas_call(
        functools.partial(_proj_kernel, acts),
        grid=(n_tiles,),
        in_specs=[
            pl.BlockSpec((PROJ_TM, d), lambda i: (0, 0), pipeline_mode=pl.Buffered(1)),
            pl.BlockSpec((None, 1, 3 * d), lambda i: (0, 0, 0)),
            pl.BlockSpec((PROJ_TM, d), lambda i: (nxt(i), 0)),
            pl.BlockSpec((None, 1, 3 * d), lambda i: ((nxt(i) * PROJ_TM) // seq, 0, 0)),
            pl.BlockSpec((1, d), lambda i: (0, 0)),
            pl.BlockSpec(memory_space=pl.ANY),
        ],
        out_specs=pl.BlockSpec((PROJ_TM, w_in.shape[1]), lambda i: (i, 0)),
        out_shape=jax.ShapeDtypeStruct((n, w_in.shape[1]), BF16),
        scratch_shapes=[
            pltpu.VMEM((2, PROJ_TM, d), BF16),
            pltpu.VMEM(w_in.shape, BF16),
            pltpu.VMEM((2, d, d), w_in.dtype),
            pltpu.SemaphoreType.DMA((2,)),
        ],
        compiler_params=pltpu.CompilerParams(
            dimension_semantics=("arbitrary",), vmem_limit_bytes=VMEM_LIMIT),
        name="proj",
    )(x2, mod3, x2, mod3, g_pre, w_in)


def _rglru_kernel(rev, reuse, *refs):
    if reuse:
        xh_ref, wblk_ref, bg_ref, lam_ref, o_ref = refs[:5]
        wg_ref, sa_ref, su_ref, sh_ref, carry_ref = refs[5:]
        nb, tc, d = xh_ref.shape
    else:
        (xa_ref, prev_ref, next_ref, shift_ref, cw_ref, cb_ref, wblk_ref, bg_ref, lam_ref,
         o_ref, xh_out_ref) = refs[:11]
        wg_ref, sa_ref, su_ref, sh_ref, carry_ref = refs[11:]
        nb, tc, d = xa_ref.shape
    i = pl.program_id(0)
    n_chunks = pl.num_programs(0) - 1
    groups = tc // SUBLANES
    n_tiles = d // MXU_TILE
    n_gates, n_blk, bw, _ = wblk_ref.shape
    per_tile = MXU_TILE // bw
    ig = jnp.minimum(i, n_chunks - 1)
    ci = n_chunks - 1 - ig if rev else ig
    cur = i % 2
    old = 1 - cur

    @pl.when(i == 0)
    def _():
        carry_ref[...] = jnp.zeros_like(carry_ref)
        sa_ref[1] = jnp.zeros(sa_ref.shape[1:], F32)
        su_ref[1] = jnp.zeros(su_ref.shape[1:], F32)
        wg_ref[...] = jnp.zeros_like(wg_ref)
        for gate in range(n_gates):
            for blk in range(n_blk):
                off = (blk % per_tile) * bw
                wg_ref[gate, blk // per_tile, off:off + bw, off:off + bw] = (
                    wblk_ref[gate, blk].astype(BF16))

    has_prev = ci > 0
    has_next = ci < n_chunks - 1
    lam = lam_ref[...]
    softplus_neg_lam = jnp.maximum(-lam, 0.0) + jnp.log1p(jnp.exp(-jnp.abs(lam)))
    c8l = (-0.5 * RG_C * LOG2_E) * softplus_neg_lam
    brh = 0.5 * bg_ref[0]
    bih = 0.5 * bg_ref[1]
    if not reuse:
        cwh = 0.5 * cw_ref[...]
        cbh = 0.5 * cb_ref[...]
        shift_m = shift_ref[...]
        taps = [k for k in range(CONV_W) if k != CONV_W // 2]

    def scan_row(b):
        h = carry_ref[b]
        for gg in range(groups):
            g = groups - 1 - gg if rev else gg
            for jj in range(SUBLANES):
                j = SUBLANES - 1 - jj if rev else jj
                tok = pl.ds(j, SUBLANES, stride=SUBLANES)
                h = sa_ref[old, b, g, tok, :] * h + su_ref[old, b, g, tok, :]
                sh_ref[b, g, tok, :] = h
            if gg % 2 == 1:
                gp = g // 2
                r0 = gp * 2 * SUBLANES
                for cblk in range(d // LANES):
                    blk = slice(cblk * SUBLANES, (cblk + 1) * SUBLANES)
                    rows16 = jnp.concatenate(
                        [sh_ref[b, 2 * gp, blk, :], sh_ref[b, 2 * gp + 1, blk, :]], axis=0)
                    o_ref[b, pl.ds(r0, 2 * SUBLANES), cblk * LANES:(cblk + 1) * LANES] = (
                        rows16.astype(o_ref.dtype))
        carry_ref[b] = h

    def conv_half(b):
        xb = xa_ref[b]
        halo = jnp.concatenate([
            jnp.where(has_prev, prev_ref[b], jnp.zeros_like(prev_ref[b])),
            jnp.where(has_next, next_ref[b], jnp.zeros_like(next_ref[b]))], axis=0)
        sh = jnp.dot(shift_m, jnp.concatenate([xb, halo], axis=0), preferred_element_type=F32)
        xh = cbh + cwh[CONV_W // 2:CONV_W // 2 + 1] * xb.astype(F32)
        for n, k in enumerate(taps):
            xh = xh + cwh[k:k + 1] * sh[n * tc:(n + 1) * tc]
        return xh

    def gates(b):
        if reuse:
            xhb = xh_ref[b]
            xh = xhb.astype(F32)
        else:
            xh = conv_half(b)
            xhb = xh.astype(BF16)
            xh_out_ref[b] = xhb
        for k in range(n_tiles):
            sl = slice(k * MXU_TILE, (k + 1) * MXU_TILE)
            xk = xhb[:, sl]
            tr = jnp.tanh(jnp.dot(xk, wg_ref[0, k], preferred_element_type=F32) + brh[:, sl])
            ti = jnp.tanh(jnp.dot(xk, wg_ref[1, k], preferred_element_type=F32) + bih[:, sl])
            a = jnp.exp2(c8l[:, sl] * (tr + 1.0))
            z = 1.0 - a * a
            u = (z * lax.rsqrt(jnp.maximum(z, RSQRT_FLOOR))) * ((ti + 1.0) * xh[:, sl])
            for c2 in range(MXU_TILE // LANES):
                cblk = k * (MXU_TILE // LANES) + c2
                rows = slice(cblk * SUBLANES, (cblk + 1) * SUBLANES)
                lanes = slice(c2 * LANES, (c2 + 1) * LANES)
                sa_ref[cur, b, :, rows, :] = a[:, lanes].reshape(groups, SUBLANES, LANES)
                su_ref[cur, b, :, rows, :] = u[:, lanes].reshape(groups, SUBLANES, LANES)

    def body(b, carry):
        scan_row(b)
        gates(b)
        return carry

    lax.fori_loop(0, nb, body, 0)


def _shift_matrix(tc):
    taps = [k for k in range(CONV_W) if k != CONV_W // 2]
    m = np.zeros((len(taps) * tc, tc + 2 * SUBLANES), np.float32)
    for n, k in enumerate(taps):
        for t in range(tc):
            s = t - CONV_W // 2 + k
            if s < 0:
                col = tc + SUBLANES + s
            elif s >= tc:
                col = tc + SUBLANES + (s - tc)
            else:
                col = s
            m[n * tc + t, col] = 1.0
    return m


def _rglru(proj3, conv_w, conv_b, wblk, bg, lam, rev, xh=None):
    nb, seq, _ = proj3.shape
    reuse = xh is not None
    d = wblk.shape[1] * wblk.shape[2]
    n_chunks = seq // SCAN_TC
    n_groups = seq // SUBLANES
    gpc = SCAN_TC // SUBLANES

    def chunk(i):
        return n_chunks - 1 - i if rev else i

    def gated(i):
        return chunk(jnp.minimum(i, n_chunks - 1))

    def scanned(i):
        return chunk(jnp.maximum(i - 1, 0))

    n_tiles = d // MXU_TILE
    n_gates = wblk.shape[0]
    whole = lambda a: pl.BlockSpec(a.shape, lambda i: (0,) * a.ndim)
    gated_spec = pl.BlockSpec((nb, SCAN_TC, d), lambda i: (0, gated(i), 0))
    h_spec = pl.BlockSpec((nb, SCAN_TC, d), lambda i: (0, scanned(i), 0))
    h_shape = jax.ShapeDtypeStruct((nb, seq, d), BF16)
    if reuse:
        in_specs = [gated_spec, whole(wblk), whole(bg), whole(lam)]
        args = (xh, wblk, bg, lam)
        out_specs, out_shape = h_spec, h_shape
    else:
        proj4 = proj3.reshape(nb, n_groups, SUBLANES, proj3.shape[2])
        shift_m = jnp.asarray(_shift_matrix(SCAN_TC), BF16)
        in_specs = [
            gated_spec,
            pl.BlockSpec((nb, None, SUBLANES, d),
                         lambda i: (0, jnp.maximum(gated(i) * gpc - 1, 0), 0, 0)),
            pl.BlockSpec((nb, None, SUBLANES, d),
                         lambda i: (0, jnp.minimum((gated(i) + 1) * gpc, n_groups - 1), 0, 0)),
            whole(shift_m), whole(conv_w), whole(conv_b), whole(wblk), whole(bg), whole(lam),
        ]
        args = (proj3, proj4, proj4, shift_m, conv_w, conv_b, wblk, bg, lam)
        out_specs, out_shape = [h_spec, gated_spec], [h_shape, h_shape]
    return pl.pallas_call(
        functools.partial(_rglru_kernel, rev, reuse),
        grid=(n_chunks + 1,),
        in_specs=in_specs,
        out_specs=out_specs,
        out_shape=out_shape,
        scratch_shapes=[
            pltpu.VMEM((n_gates, n_tiles, MXU_TILE, MXU_TILE), BF16),
            pltpu.VMEM((2, nb, gpc, d // LANES * SUBLANES, LANES), F32),
            pltpu.VMEM((2, nb, gpc, d // LANES * SUBLANES, LANES), F32),
            pltpu.VMEM((nb, gpc, d // LANES * SUBLANES, LANES), F32),
            pltpu.VMEM((nb, SUBLANES, LANES), F32),
        ],
        compiler_params=pltpu.CompilerParams(
            dimension_semantics=("arbitrary",), vmem_limit_bytes=VMEM_LIMIT),
        name="rglru_bwd" if rev else "rglru_fwd",
    )(*args)


def _natten_kernel(k_col, v_col, q_ref, kv_hbm, bias_ref, o_ref, k_buf, v_buf, sem):
    batch = pl.program_id(0)
    step = pl.program_id(1)
    n_steps = pl.num_programs(1)
    rows = n_steps * NAT_ROWS
    gw = GRID_W
    d = q_ref.shape[1]
    win_r = min(WIN_R, rows)
    held = k_buf.shape[1] // gw

    def first_held_row(s):
        return jnp.clip(s * NAT_ROWS - win_r // 2, 0, rows - held)

    def window_copies(b, s, slot):
        tok = pl.multiple_of(first_held_row(s) * gw, gw)
        return [pltpu.make_async_copy(kv_hbm.at[b, pl.ds(tok, held * gw), pl.ds(col * d, d)],
                                      buf.at[slot], sem.at[n, slot])
                for n, (col, buf) in enumerate(((k_col, k_buf), (v_col, v_buf)))]

    t = batch * n_steps + step
    slot = t % 2

    @pl.when(t == 0)
    def _():
        for c in window_copies(0, 0, 0):
            c.start()

    @pl.when(t + 1 < pl.num_programs(0) * n_steps)
    def _():
        wrap = step + 1 == n_steps
        for c in window_copies(jnp.where(wrap, batch + 1, batch), jnp.where(wrap, 0, step + 1),
                               1 - slot):
            c.start()

    for c in window_copies(batch, step, slot):
        c.wait()

    lane = lax.broadcasted_iota(jnp.int32, (gw, LANES), 1)
    lo = lane < HEAD_DIM
    zero = jnp.zeros((gw, LANES), BF16)
    n_pairs = d // LANES
    w0 = first_held_row(step)
    wins = []
    pats = []
    scores = []
    for rr in range(NAT_ROWS):
        r = step * NAT_ROWS + rr
        r0 = jnp.clip(r - win_r // 2, 0, rows - win_r)
        k0 = pl.multiple_of((r0 - w0) * gw, gw)
        wins.append(k0)
        pats.append(r0 - r + (WIN_R - 1))
        for p in range(n_pairs):
            sl = slice(p * LANES, (p + 1) * LANES)
            qp = q_ref[rr * gw:(rr + 1) * gw, sl]
            qs = jnp.concatenate([jnp.where(lo, qp, zero), jnp.where(lo, zero, qp)], axis=0)
            scores.append(lax.dot_general(qs, k_buf[slot, pl.ds(k0, win_r * gw), sl],
                                          (((1,), (1,)), ((), ())), preferred_element_type=F32))
    probs = []
    for rr in range(NAT_ROWS):
        for p in range(n_pairs):
            bias = jnp.concatenate([bias_ref[pats[rr] + 2 * t, p] for t in range(win_r // 2)],
                                   axis=1)
            s = scores[rr * n_pairs + p] + bias
            m = jnp.max(s, axis=-1, keepdims=True)
            probs.append(jnp.exp2(s - m).astype(BF16))
    ones = jnp.ones((win_r * gw, LANES), BF16)
    for rr in range(NAT_ROWS):
        for p in range(n_pairs):
            sl = slice(p * LANES, (p + 1) * LANES)
            v_ext = jnp.concatenate([v_buf[slot, pl.ds(wins[rr], win_r * gw), sl], ones], axis=1)
            pv = jnp.dot(probs[rr * n_pairs + p], v_ext, preferred_element_type=F32)
            o = pv[:, :LANES] / pv[:, LANES:]
            o_ref[rr * gw:(rr + 1) * gw, sl] = jnp.where(lo, o[:gw], o[gw:]).astype(BF16)


def _natten(proj3, bias_tab, q_col, k_col, v_col):
    nb, seq, _ = proj3.shape
    d = N_HEADS * HEAD_DIM
    rows = seq // GRID_W
    held = min(rows, NAT_ROWS + min(WIN_R, rows))
    return pl.pallas_call(
        functools.partial(_natten_kernel, k_col, v_col),
        grid=(nb, rows // NAT_ROWS),
        in_specs=[
            pl.BlockSpec((None, NAT_ROWS * GRID_W, d), lambda b, i: (b, i, q_col)),
            pl.BlockSpec(memory_space=pl.ANY),
            pl.BlockSpec(bias_tab.shape, lambda b, i: (0, 0, 0, 0), pipeline_mode=pl.Buffered(1)),
        ],
        out_specs=pl.BlockSpec((None, NAT_ROWS * GRID_W, d), lambda b, i: (b, i, 0)),
        out_shape=jax.ShapeDtypeStruct((nb, seq, d), BF16),
        scratch_shapes=[pltpu.VMEM((2, held * GRID_W, d), BF16),
                        pltpu.VMEM((2, held * GRID_W, d), BF16),
                        pltpu.SemaphoreType.DMA((2, 2))],
        compiler_params=pltpu.CompilerParams(
            dimension_semantics=("arbitrary", "arbitrary"), vmem_limit_bytes=NAT_VMEM_LIMIT),
        name="natten",
    )(proj3, proj3, bias_tab)


def _bias_kernel(rp_ref, o_ref):
    e = pl.program_id(0)
    n_pairs = o_ref.shape[0]
    gw = GRID_W
    q_idx = lax.broadcasted_iota(jnp.int32, (gw, LANES), 0)
    lane = lax.broadcasted_iota(jnp.int32, (gw, LANES), 1)
    col = jnp.where(lane < gw, lane, lane - gw)
    c0 = jnp.clip(q_idx - WIN_C // 2, 0, gw - WIN_C)
    valid = (col >= c0) & (col < c0 + WIN_C)
    first = lane < gw
    for pair in range(n_pairs):
        for hh in range(2):
            h = 2 * pair + hh
            va = jnp.broadcast_to(rp_ref[h, pl.ds(e, 1), :], (gw, LANES))
            vb = jnp.broadcast_to(rp_ref[h, pl.ds(e + 1, 1), :], (gw, LANES))
            ta = pltpu.roll(va, 0, 1, stride=1, stride_axis=0)
            tb = pltpu.roll(vb, gw, 1, stride=1, stride_axis=0)
            tile = jnp.where(valid, LOG2_E * jnp.where(first, ta, tb), MASK_BIAS)
            o_ref[pair, hh * gw:(hh + 1) * gw, :] = tile


def _bias_table(rpb, rows):
    win_r = min(WIN_R, rows)
    assert 2 * GRID_W == LANES and win_r % 2 == 0
    nh, nd, _ = rpb.shape
    rp = jnp.concatenate([rpb[:, :, WIN_C - 1:],
                          jnp.zeros((nh, nd, LANES - (2 * WIN_C - 1)), rpb.dtype),
                          rpb[:, :, :WIN_C - 1]], axis=-1).astype(F32)
    n_off = WIN_R + win_r - 2
    return pl.pallas_call(
        _bias_kernel,
        grid=(n_off,),
        in_specs=[pl.BlockSpec(rp.shape, lambda e: (0, 0, 0))],
        out_specs=pl.BlockSpec((None, nh // 2, 2 * GRID_W, LANES), lambda e: (e, 0, 0, 0)),
        out_shape=jax.ShapeDtypeStruct((n_off, nh // 2, 2 * GRID_W, LANES), F32),
        compiler_params=pltpu.CompilerParams(
            dimension_semantics=("parallel",), vmem_limit_bytes=VMEM_LIMIT),
        name="bias_table",
    )(rp)


def _tail_kernel(final, x_ref, mod_ref, hf_ref, hb_ref, za_ref, zb_ref, ga_ref, gb_ref, att_ref,
                 wpa_ref, wpb_ref, wo_ref, gf_ref, o_ref, w_scr):
    d = x_ref.shape[1]

    @pl.when(pl.program_id(0) == 0)
    def _():
        for n, w_ref in enumerate((wpa_ref, wpb_ref, wo_ref)):
            w_scr[n] = w_ref[...].astype(BF16)

    for s in range(x_ref.shape[0] // TAIL_SUB):
        rows = slice(s * TAIL_SUB, (s + 1) * TAIL_SUB)
        h = hf_ref[rows, :].astype(F32) + hb_ref[rows, :].astype(F32)
        ya = (h * za_ref[rows, :].astype(F32)).astype(BF16)
        ya = jnp.dot(ya, w_scr[0], preferred_element_type=F32)
        yb = (att_ref[rows, :].astype(F32) * zb_ref[rows, :].astype(F32)).astype(BF16)
        yb = jnp.dot(yb, w_scr[1], preferred_element_type=F32)
        merged = ga_ref[rows, :].astype(F32) * ya + gb_ref[rows, :].astype(F32) * yb
        res = jnp.dot(merged.astype(BF16), w_scr[2], preferred_element_type=F32)
        xo = x_ref[rows, :] + mod_ref[:, 2 * d:3 * d] * res
        if final:
            ms = jnp.mean(xo * xo, axis=-1, keepdims=True)
            xo = xo * lax.rsqrt(ms + EPS) * gf_ref[...]
        o_ref[rows, :] = xo


def _tail(x2, mod3, hf, hb, proj, att2, w_pa, w_pb, w_o, g_final, seq, cols, final):
    n, d = x2.shape
    za_col, zb_col, ga_col, gb_col = cols
    tok = lambda i: (i, 0)
    wspec = pl.BlockSpec((d, d), lambda i: (0, 0), pipeline_mode=pl.Buffered(1))
    return pl.pallas_call(
        functools.partial(_tail_kernel, final),
        grid=(n // TAIL_TM,),
        in_specs=[
            pl.BlockSpec((TAIL_TM, d), tok),
            pl.BlockSpec((None, 1, 3 * d), lambda i: ((i * TAIL_TM) // seq, 0, 0)),
            pl.BlockSpec((TAIL_TM, d), tok),
            pl.BlockSpec((TAIL_TM, d), tok),
            pl.BlockSpec((TAIL_TM, d), lambda i: (i, za_col)),
            pl.BlockSpec((TAIL_TM, d), lambda i: (i, zb_col)),
            pl.BlockSpec((TAIL_TM, d), lambda i: (i, ga_col)),
            pl.BlockSpec((TAIL_TM, d), lambda i: (i, gb_col)),
            pl.BlockSpec((TAIL_TM, d), tok),
            wspec, wspec, wspec,
            pl.BlockSpec((1, d), lambda i: (0, 0)),
        ],
        out_specs=pl.BlockSpec((TAIL_TM, d), tok),
        out_shape=jax.ShapeDtypeStruct((n, d), F32),
        scratch_shapes=[pltpu.VMEM((3, d, d), BF16)],
        compiler_params=pltpu.CompilerParams(
            dimension_semantics=("arbitrary",), vmem_limit_bytes=VMEM_LIMIT),
        name="tail",
    )(x2, mod3, hf, hb, proj, proj, proj, proj, att2, w_pa, w_pb, w_o, g_final)


def kernel(x, c, g_pre, w_c, b_c, w_in, conv_w, conv_b, w_r_f, b_r_f, w_i_f, b_i_f, lam_f,
           w_r_b, b_r_b, w_i_b, b_i_b, lam_b, rpb, w_pa, w_pb, w_o, g_final):
    nb, seq, d = x.shape
    depth = g_pre.shape[0]
    n = nb * seq
    rows = seq // GRID_W
    groups = w_in.shape[2] // d
    xa_col, za_col, q_col, k_col, v_col, zb_col, ga_col, gb_col = range(groups)
    assert xa_col == 0
    acts = ["id"] * groups
    acts[za_col] = acts[zb_col] = "silu"
    acts[ga_col] = acts[gb_col] = "sigmoid"
    acts[q_col] = "qscale"

    c_pad = jnp.zeros((MOD_ROWS, d), F32).at[:nb].set(c)
    x2 = x.reshape(n, d)
    for l in range(depth):
        mod = _modulation(c_pad, w_c[l], b_c[l].reshape(1, 3 * d))
        mod3 = mod[:nb].reshape(nb, 1, 3 * d)

        proj = _projection(x2, mod3, g_pre[l].reshape(1, d), w_in[l], seq, tuple(acts))
        proj3 = proj.reshape(nb, seq, groups * d)

        cw, cb = conv_w[l], conv_b[l].reshape(1, d)
        hf, xh = _rglru(proj3, cw, cb, jnp.stack([w_r_f[l], w_i_f[l]]),
                        jnp.stack([b_r_f[l], b_i_f[l]]).reshape(2, 1, d), lam_f[l].reshape(1, d),
                        False)
        hb = _rglru(proj3, cw, cb, jnp.stack([w_r_b[l], w_i_b[l]]),
                    jnp.stack([b_r_b[l], b_i_b[l]]).reshape(2, 1, d), lam_b[l].reshape(1, d), True,
                    xh=xh)

        att = _natten(proj3, _bias_table(rpb[l], rows), q_col, k_col, v_col)

        x2 = _tail(x2, mod3, hf.reshape(n, d), hb.reshape(n, d), proj, att.reshape(n, d),
                   w_pa[l], w_pb[l], w_o[l],
                   g_final.reshape(1, d), seq, (za_col, zb_col, ga_col, gb_col),
                   final=(l == depth - 1))
    return x2.reshape(nb, seq, d)
```

```python
import functools

import jax
import jax.numpy as jnp
import numpy as np
from jax import lax
from jax.experimental import pallas as pl
from jax.experimental.pallas import tpu as pltpu

F32 = jnp.float32
BF16 = jnp.bfloat16

SUBLANES = 8
LANES = 128
MXU_TILE = 256

GRID_W = 64
A_BLOCKS = 16
CONV_W = 4
RG_C = 8.0
N_HEADS = 16
HEAD_DIM = 64
WIN_R = 8
WIN_C = 16
EPS = 1e-6
MASK_BIAS = -1e30
LOG2_E = 1.4426950408889634
RSQRT_FLOOR = 1e-30

PROJ_TM = 512
NORM_RB = 64
CAST_RB = 128
SCAN_TC = 128
TAIL_TM = 512
TAIL_SUB = 256
NAT_ROWS = 8
MOD_ROWS = 16

VMEM_LIMIT = 56 * 1024 * 1024
NAT_VMEM_LIMIT = 61 * 1024 * 1024


def _sigmoid(x):
    return 0.5 * jnp.tanh(0.5 * x) + 0.5


def _silu(x):
    return x * _sigmoid(x)


_ACTIVATIONS = {
    "id": lambda v: v,
    "silu": _silu,
    "sigmoid": _sigmoid,
    "qscale": lambda v: v * (HEAD_DIM ** -0.5 * LOG2_E),
}


def _mod_kernel(c_ref, w_ref, b_ref, o_ref, pad_ref):
    nb = c_ref.shape[0]
    pad_ref[...] = jnp.zeros_like(pad_ref)
    pad_ref[0:nb, :] = c_ref[...]
    sc = _silu(pad_ref[...]).astype(BF16)
    w = w_ref[...].astype(BF16)
    mod = jnp.dot(sc, w, preferred_element_type=F32) + b_ref[...]
    for b in range(nb):
        o_ref[b] = mod[b:b + 1, :]


def _modulation(c, w_c, b_c):
    nb, d = c.shape
    assert nb <= MOD_ROWS
    return pl.pallas_call(
        _mod_kernel,
        grid=(3,),
        in_specs=[
            pl.BlockSpec((nb, d), lambda j: (0, 0)),
            pl.BlockSpec((d, d), lambda j: (0, j)),
            pl.BlockSpec((1, d), lambda j: (0, j)),
        ],
        out_specs=pl.BlockSpec((nb, 1, d), lambda j: (0, 0, j)),
        out_shape=jax.ShapeDtypeStruct((nb, 1, 3 * d), F32),
        scratch_shapes=[pltpu.VMEM((MOD_ROWS, d), F32)],
        compiler_params=pltpu.CompilerParams(
            dimension_semantics=("parallel",), vmem_limit_bytes=VMEM_LIMIT),
        name="mod",
    )(c, w_c, b_c)


def _norm_rows(x_ref, mod_ref, g, h_ref, slot, r0):
    d = x_ref.shape[1]
    x = x_ref[pl.ds(r0, NORM_RB), :]
    ms = jnp.mean(x * x, axis=-1, keepdims=True)
    y = x * lax.rsqrt(ms + EPS) * g
    h = y * (1.0 + mod_ref[:, d:2 * d]) + mod_ref[:, 0:d]
    h_ref[slot, pl.ds(r0, NORM_RB), :] = h.astype(BF16)


def _proj_kernel(acts, x0_ref, mod0_ref, xn_ref, modn_ref, g_ref, w_hbm, o_ref,
                 h_ref, w_ref, stage_ref, sem):
    i = pl.program_id(0)
    tm, d = xn_ref.shape
    g = g_ref[...]

    def weight_copy(j):
        return pltpu.make_async_copy(w_hbm.at[:, pl.ds(j * d, d)], stage_ref.at[j % 2],
                                     sem.at[j % 2])

    @pl.when(i == 0)
    def _():
        weight_copy(0).start()
        for j in range(len(acts)):
            if j + 1 < len(acts):
                weight_copy(j + 1).start()
            weight_copy(j).wait()

            def cast(k, carry, j=j):
                r0 = pl.multiple_of(k * CAST_RB, CAST_RB)
                w_ref[pl.ds(r0, CAST_RB), j * d:(j + 1) * d] = (
                    stage_ref[j % 2, pl.ds(r0, CAST_RB), :].astype(BF16))
                return carry
            lax.fori_loop(0, d // CAST_RB, cast, 0)

        def body(k, carry):
            _norm_rows(x0_ref, mod0_ref, g, h_ref, 0, pl.multiple_of(k * NORM_RB, NORM_RB))
            return carry
        lax.fori_loop(0, tm // NORM_RB, body, 0)

    cur = i % 2
    for k in range(tm // NORM_RB):
        _norm_rows(xn_ref, modn_ref, g, h_ref, 1 - cur, k * NORM_RB)
    for j, act in enumerate(acts):
        cols = slice(j * d, (j + 1) * d)
        acc = jnp.dot(h_ref[cur], w_ref[:, cols], preferred_element_type=F32)
        o_ref[:, cols] = _ACTIVATIONS[act](acc).astype(BF16)


def _projection(x2, mod3, g_pre, w_in, seq, acts):
    n, d = x2.shape
    n_tiles = n // PROJ_TM
    nxt = lambda i: jnp.minimum(i + 1, n_tiles - 1)
    return pl.pallas_call(
        functools.partial(_proj_kernel, acts),
        grid=(n_tiles,),
        in_specs=[
            pl.BlockSpec((PROJ_TM, d), lambda i: (0, 0), pipeline_mode=pl.Buffered(1)),
            pl.BlockSpec((None, 1, 3 * d), lambda i: (0, 0, 0)),
            pl.BlockSpec((PROJ_TM, d), lambda i: (nxt(i), 0)),
            pl.BlockSpec((None, 1, 3 * d), lambda i: ((nxt(i) * PROJ_TM) // seq, 0, 0)),
            pl.BlockSpec((1, d), lambda i: (0, 0)),
            pl.BlockSpec(memory_space=pl.ANY),
        ],
        out_specs=pl.BlockSpec((PROJ_TM, w_in.shape[1]), lambda i: (i, 0)),
        out_shape=jax.ShapeDtypeStruct((n, w_in.shape[1]), BF16),
        scratch_shapes=[
            pltpu.VMEM((2, PROJ_TM, d), BF16),
            pltpu.VMEM(w_in.shape, BF16),
            pltpu.VMEM((2, d, d), w_in.dtype),
            pltpu.SemaphoreType.DMA((2,)),
        ],
        compiler_params=pltpu.CompilerParams(
            dimension_semantics=("arbitrary",), vmem_limit_bytes=VMEM_LIMIT),
        name="proj",
    )(x2, mod3, x2, mod3, g_pre, w_in)


def _rglru_kernel(rev, reuse, *refs):
    if reuse:
        xh_ref, wr_ref, wi_ref, br_ref, bi_ref, lam_ref, o_ref = refs[:7]
        wg_ref, sa_ref, su_ref, sh_ref, carry_ref = refs[7:]
        nb, tc, d = xh_ref.shape
    else:
        (xa_ref, prev_ref, next_ref, shift_ref, cw_ref, cb_ref, wr_ref, wi_ref, br_ref, bi_ref,
         lam_ref, o_ref, xh_out_ref) = refs[:13]
        wg_ref, sa_ref, su_ref, sh_ref, carry_ref = refs[13:]
        nb, tc, d = xa_ref.shape
    i = pl.program_id(0)
    n_chunks = pl.num_programs(0) - 1
    groups = tc // SUBLANES
    n_tiles = d // MXU_TILE
    n_blk, bw, _ = wr_ref.shape
    per_tile = MXU_TILE // bw
    ig = jnp.minimum(i, n_chunks - 1)
    ci = n_chunks - 1 - ig if rev else ig
    cur = i % 2
    old = 1 - cur

    @pl.when(i == 0)
    def _():
        carry_ref[...] = jnp.zeros_like(carry_ref)
        sa_ref[1] = jnp.zeros(sa_ref.shape[1:], F32)
        su_ref[1] = jnp.zeros(su_ref.shape[1:], F32)
        wg_ref[...] = jnp.zeros_like(wg_ref)
        for gate, w_ref in enumerate((wr_ref, wi_ref)):
            for blk in range(n_blk):
                off = (blk % per_tile) * bw
                wg_ref[gate, blk // per_tile, off:off + bw, off:off + bw] = (
                    w_ref[blk].astype(BF16))

    has_prev = ci > 0
    has_next = ci < n_chunks - 1
    lam = lam_ref[...]
    softplus_neg_lam = jnp.maximum(-lam, 0.0) + jnp.log1p(jnp.exp(-jnp.abs(lam)))
    c8l = (-0.5 * RG_C * LOG2_E) * softplus_neg_lam
    brh = 0.5 * br_ref[...]
    bih = 0.5 * bi_ref[...]
    if not reuse:
        cwh = 0.5 * cw_ref[...]
        cbh = 0.5 * cb_ref[...]
        shift_m = shift_ref[...]
        taps = [k for k in range(CONV_W) if k != CONV_W // 2]

    def scan_row(b):
        h = carry_ref[b]
        for gg in range(groups):
            g = groups - 1 - gg if rev else gg
            for jj in range(SUBLANES):
                j = SUBLANES - 1 - jj if rev else jj
                tok = pl.ds(j, SUBLANES, stride=SUBLANES)
                h = sa_ref[old, b, g, tok, :] * h + su_ref[old, b, g, tok, :]
                sh_ref[b, g, tok, :] = h
            if gg % 2 == 1:
                gp = g // 2
                r0 = gp * 2 * SUBLANES
                for cblk in range(d // LANES):
                    blk = slice(cblk * SUBLANES, (cblk + 1) * SUBLANES)
                    rows16 = jnp.concatenate(
                        [sh_ref[b, 2 * gp, blk, :], sh_ref[b, 2 * gp + 1, blk, :]], axis=0)
                    o_ref[b, pl.ds(r0, 2 * SUBLANES), cblk * LANES:(cblk + 1) * LANES] = (
                        rows16.astype(o_ref.dtype))
        carry_ref[b] = h

    def conv_half(b):
        xb = xa_ref[b]
        halo = jnp.concatenate([
            jnp.where(has_prev, prev_ref[b], jnp.zeros_like(prev_ref[b])),
            jnp.where(has_next, next_ref[b], jnp.zeros_like(next_ref[b]))], axis=0)
        sh = jnp.dot(shift_m, jnp.concatenate([xb, halo], axis=0), preferred_element_type=F32)
        xh = cbh + cwh[CONV_W // 2:CONV_W // 2 + 1] * xb.astype(F32)
        for n, k in enumerate(taps):
            xh = xh + cwh[k:k + 1] * sh[n * tc:(n + 1) * tc]
        return xh

    def gates(b):
        if reuse:
            xhb = xh_ref[b]
            xh = xhb.astype(F32)
        else:
            xh = conv_half(b)
            xhb = xh.astype(BF16)
            xh_out_ref[b] = xhb
        for k in range(n_tiles):
            sl = slice(k * MXU_TILE, (k + 1) * MXU_TILE)
            xk = xhb[:, sl]
            tr = jnp.tanh(jnp.dot(xk, wg_ref[0, k], preferred_element_type=F32) + brh[:, sl])
            ti = jnp.tanh(jnp.dot(xk, wg_ref[1, k], preferred_element_type=F32) + bih[:, sl])
            a = jnp.exp2(c8l[:, sl] * (tr + 1.0))
            z = 1.0 - a * a
            u = (z * lax.rsqrt(jnp.maximum(z, RSQRT_FLOOR))) * ((ti + 1.0) * xh[:, sl])
            for c2 in range(MXU_TILE // LANES):
                cblk = k * (MXU_TILE // LANES) + c2
                rows = slice(cblk * SUBLANES, (cblk + 1) * SUBLANES)
                lanes = slice(c2 * LANES, (c2 + 1) * LANES)
                sa_ref[cur, b, :, rows, :] = a[:, lanes].reshape(groups, SUBLANES, LANES)
                su_ref[cur, b, :, rows, :] = u[:, lanes].reshape(groups, SUBLANES, LANES)

    def body(b, carry):
        scan_row(b)
        gates(b)
        return carry

    lax.fori_loop(0, nb, body, 0)


def _shift_matrix(tc):
    taps = [k for k in range(CONV_W) if k != CONV_W // 2]
    m = np.zeros((len(taps) * tc, tc + 2 * SUBLANES), np.float32)
    for n, k in enumerate(taps):
        for t in range(tc):
            s = t - CONV_W // 2 + k
            if s < 0:
                col = tc + SUBLANES + s
            elif s >= tc:
                col = tc + SUBLANES + (s - tc)
            else:
                col = s
            m[n * tc + t, col] = 1.0
    return m


def _rglru(proj3, conv_w, conv_b, w_r, w_i, b_r, b_i, lam, rev, xh=None):
    nb, seq, _ = proj3.shape
    reuse = xh is not None
    d = w_r.shape[0] * w_r.shape[1]
    n_chunks = seq // SCAN_TC
    n_groups = seq // SUBLANES
    gpc = SCAN_TC // SUBLANES

    def chunk(i):
        return n_chunks - 1 - i if rev else i

    def gated(i):
        return chunk(jnp.minimum(i, n_chunks - 1))

    def scanned(i):
        return chunk(jnp.maximum(i - 1, 0))

    n_tiles = d // MXU_TILE
    gate_args = (w_r, w_i, b_r, b_i, lam)
    whole = lambda a: pl.BlockSpec(a.shape, lambda i: (0,) * a.ndim)
    gated_spec = pl.BlockSpec((nb, SCAN_TC, d), lambda i: (0, gated(i), 0))
    h_spec = pl.BlockSpec((nb, SCAN_TC, d), lambda i: (0, scanned(i), 0))
    h_shape = jax.ShapeDtypeStruct((nb, seq, d), BF16)
    if reuse:
        in_specs = [gated_spec] + [whole(a) for a in gate_args]
        args = (xh,) + gate_args
        out_specs, out_shape = h_spec, h_shape
    else:
        proj4 = proj3.reshape(nb, n_groups, SUBLANES, proj3.shape[2])
        shift_m = jnp.asarray(_shift_matrix(SCAN_TC), BF16)
        in_specs = [
            gated_spec,
            pl.BlockSpec((nb, None, SUBLANES, d),
                         lambda i: (0, jnp.maximum(gated(i) * gpc - 1, 0), 0, 0)),
            pl.BlockSpec((nb, None, SUBLANES, d),
                         lambda i: (0, jnp.minimum((gated(i) + 1) * gpc, n_groups - 1), 0, 0)),
            whole(shift_m), whole(conv_w), whole(conv_b),
        ] + [whole(a) for a in gate_args]
        args = (proj3, proj4, proj4, shift_m, conv_w, conv_b) + gate_args
        out_specs, out_shape = [h_spec, gated_spec], [h_shape, h_shape]
    return pl.pallas_call(
        functools.partial(_rglru_kernel, rev, reuse),
        grid=(n_chunks + 1,),
        in_specs=in_specs,
        out_specs=out_specs,
        out_shape=out_shape,
        scratch_shapes=[
            pltpu.VMEM((2, n_tiles, MXU_TILE, MXU_TILE), BF16),
            pltpu.VMEM((2, nb, gpc, d // LANES * SUBLANES, LANES), F32),
            pltpu.VMEM((2, nb, gpc, d // LANES * SUBLANES, LANES), F32),
            pltpu.VMEM((nb, gpc, d // LANES * SUBLANES, LANES), F32),
            pltpu.VMEM((nb, SUBLANES, LANES), F32),
        ],
        compiler_params=pltpu.CompilerParams(
            dimension_semantics=("arbitrary",), vmem_limit_bytes=VMEM_LIMIT),
        name="rglru_bwd" if rev else "rglru_fwd",
    )(*args)


def _natten_kernel(k_col, v_col, q_ref, kv_hbm, bias_ref, o_ref, k_buf, v_buf, sem):
    batch = pl.program_id(0)
    step = pl.program_id(1)
    n_steps = pl.num_programs(1)
    rows = n_steps * NAT_ROWS
    gw = GRID_W
    d = q_ref.shape[1]
    win_r = min(WIN_R, rows)
    held = k_buf.shape[1] // gw

    def first_held_row(s):
        return jnp.clip(s * NAT_ROWS - win_r // 2, 0, rows - held)

    def window_copies(b, s, slot):
        tok = pl.multiple_of(first_held_row(s) * gw, gw)
        return [pltpu.make_async_copy(kv_hbm.at[b, pl.ds(tok, held * gw), pl.ds(col * d, d)],
                                      buf.at[slot], sem.at[n, slot])
                for n, (col, buf) in enumerate(((k_col, k_buf), (v_col, v_buf)))]

    t = batch * n_steps + step
    slot = t % 2

    @pl.when(t == 0)
    def _():
        for c in window_copies(0, 0, 0):
            c.start()

    @pl.when(t + 1 < pl.num_programs(0) * n_steps)
    def _():
        wrap = step + 1 == n_steps
        for c in window_copies(jnp.where(wrap, batch + 1, batch), jnp.where(wrap, 0, step + 1),
                               1 - slot):
            c.start()

    for c in window_copies(batch, step, slot):
        c.wait()

    lane = lax.broadcasted_iota(jnp.int32, (gw, LANES), 1)
    lo = lane < HEAD_DIM
    zero = jnp.zeros((gw, LANES), BF16)
    n_pairs = d // LANES
    w0 = first_held_row(step)
    wins = []
    pats = []
    scores = []
    for rr in range(NAT_ROWS):
        r = step * NAT_ROWS + rr
        r0 = jnp.clip(r - win_r // 2, 0, rows - win_r)
        k0 = pl.multiple_of((r0 - w0) * gw, gw)
        wins.append(k0)
        pats.append(r0 - r + (WIN_R - 1))
        for p in range(n_pairs):
            sl = slice(p * LANES, (p + 1) * LANES)
            qp = q_ref[rr * gw:(rr + 1) * gw, sl]
            qs = jnp.concatenate([jnp.where(lo, qp, zero), jnp.where(lo, zero, qp)], axis=0)
            scores.append(lax.dot_general(qs, k_buf[slot, pl.ds(k0, win_r * gw), sl],
                                          (((1,), (1,)), ((), ())), preferred_element_type=F32))
    probs = []
    for rr in range(NAT_ROWS):
        for p in range(n_pairs):
            bias = jnp.concatenate([bias_ref[pats[rr] + 2 * t, p] for t in range(win_r // 2)],
                                   axis=1)
            s = scores[rr * n_pairs + p] + bias
            m = jnp.max(s, axis=-1, keepdims=True)
            probs.append(jnp.exp2(s - m).astype(BF16))
    ones = jnp.ones((win_r * gw, LANES), BF16)
    for rr in range(NAT_ROWS):
        for p in range(n_pairs):
            sl = slice(p * LANES, (p + 1) * LANES)
            v_ext = jnp.concatenate([v_buf[slot, pl.ds(wins[rr], win_r * gw), sl], ones], axis=1)
            pv = jnp.dot(probs[rr * n_pairs + p], v_ext, preferred_element_type=F32)
            o = pv[:, :LANES] / pv[:, LANES:]
            o_ref[rr * gw:(rr + 1) * gw, sl] = jnp.where(lo, o[:gw], o[gw:]).astype(BF16)


def _natten(proj3, bias_tab, q_col, k_col, v_col):
    nb, seq, _ = proj3.shape
    d = N_HEADS * HEAD_DIM
    rows = seq // GRID_W
    held = min(rows, NAT_ROWS + min(WIN_R, rows))
    return pl.pallas_call(
        functools.partial(_natten_kernel, k_col, v_col),
        grid=(nb, rows // NAT_ROWS),
        in_specs=[
            pl.BlockSpec((None, NAT_ROWS * GRID_W, d), lambda b, i: (b, i, q_col)),
            pl.BlockSpec(memory_space=pl.ANY),
            pl.BlockSpec(bias_tab.shape, lambda b, i: (0, 0, 0, 0), pipeline_mode=pl.Buffered(1)),
        ],
        out_specs=pl.BlockSpec((None, NAT_ROWS * GRID_W, d), lambda b, i: (b, i, 0)),
        out_shape=jax.ShapeDtypeStruct((nb, seq, d), BF16),
        scratch_shapes=[pltpu.VMEM((2, held * GRID_W, d), BF16),
                        pltpu.VMEM((2, held * GRID_W, d), BF16),
                        pltpu.SemaphoreType.DMA((2, 2))],
        compiler_params=pltpu.CompilerParams(
            dimension_semantics=("arbitrary", "arbitrary"), vmem_limit_bytes=NAT_VMEM_LIMIT),
        name="natten",
    )(proj3, proj3, bias_tab)


def _bias_kernel(rp_ref, o_ref):
    e = pl.program_id(0)
    n_pairs = o_ref.shape[0]
    gw = GRID_W
    q_idx = lax.broadcasted_iota(jnp.int32, (gw, LANES), 0)
    lane = lax.broadcasted_iota(jnp.int32, (gw, LANES), 1)
    col = jnp.where(lane < gw, lane, lane - gw)
    c0 = jnp.clip(q_idx - WIN_C // 2, 0, gw - WIN_C)
    valid = (col >= c0) & (col < c0 + WIN_C)
    first = lane < gw
    for pair in range(n_pairs):
        for hh in range(2):
            h = 2 * pair + hh
            va = jnp.broadcast_to(rp_ref[h, pl.ds(e, 1), :], (gw, LANES))
            vb = jnp.broadcast_to(rp_ref[h, pl.ds(e + 1, 1), :], (gw, LANES))
            ta = pltpu.roll(va, 0, 1, stride=1, stride_axis=0)
            tb = pltpu.roll(vb, gw, 1, stride=1, stride_axis=0)
            tile = jnp.where(valid, LOG2_E * jnp.where(first, ta, tb), MASK_BIAS)
            o_ref[pair, hh * gw:(hh + 1) * gw, :] = tile


def _bias_table(rpb, rows):
    win_r = min(WIN_R, rows)
    assert 2 * GRID_W == LANES and win_r % 2 == 0
    nh, nd, _ = rpb.shape
    rp = jnp.concatenate([rpb[:, :, WIN_C - 1:],
                          jnp.zeros((nh, nd, LANES - (2 * WIN_C - 1)), rpb.dtype),
                          rpb[:, :, :WIN_C - 1]], axis=-1).astype(F32)
    n_off = WIN_R + win_r - 2
    return pl.pallas_call(
        _bias_kernel,
        grid=(n_off,),
        in_specs=[pl.BlockSpec(rp.shape, lambda e: (0, 0, 0))],
        out_specs=pl.BlockSpec((None, nh // 2, 2 * GRID_W, LANES), lambda e: (e, 0, 0, 0)),
        out_shape=jax.ShapeDtypeStruct((n_off, nh // 2, 2 * GRID_W, LANES), F32),
        compiler_params=pltpu.CompilerParams(
            dimension_semantics=("parallel",), vmem_limit_bytes=VMEM_LIMIT),
        name="bias_table",
    )(rp)


def _tail_kernel(final, x_ref, mod_ref, hf_ref, hb_ref, za_ref, zb_ref, ga_ref, gb_ref, att_ref,
                 wpa_ref, wpb_ref, wo_ref, gf_ref, o_ref, w_scr):
    d = x_ref.shape[1]

    @pl.when(pl.program_id(0) == 0)
    def _():
        for n, w_ref in enumerate((wpa_ref, wpb_ref, wo_ref)):
            w_scr[n] = w_ref[...].astype(BF16)

    for s in range(x_ref.shape[0] // TAIL_SUB):
        rows = slice(s * TAIL_SUB, (s + 1) * TAIL_SUB)
        h = hf_ref[rows, :].astype(F32) + hb_ref[rows, :].astype(F32)
        ya = (h * za_ref[rows, :].astype(F32)).astype(BF16)
        ya = jnp.dot(ya, w_scr[0], preferred_element_type=F32)
        yb = (att_ref[rows, :].astype(F32) * zb_ref[rows, :].astype(F32)).astype(BF16)
        yb = jnp.dot(yb, w_scr[1], preferred_element_type=F32)
        merged = ga_ref[rows, :].astype(F32) * ya + gb_ref[rows, :].astype(F32) * yb
        res = jnp.dot(merged.astype(BF16), w_scr[2], preferred_element_type=F32)
        xo = x_ref[rows, :] + mod_ref[:, 2 * d:3 * d] * res
        if final:
            ms = jnp.mean(xo * xo, axis=-1, keepdims=True)
            xo = xo * lax.rsqrt(ms + EPS) * gf_ref[...]
        o_ref[rows, :] = xo


def _tail(x2, mod3, hf, hb, proj, att2, w_pa, w_pb, w_o, g_final, seq, cols, final):
    n, d = x2.shape
    za_col, zb_col, ga_col, gb_col = cols
    tok = lambda i: (i, 0)
    wspec = pl.BlockSpec((d, d), lambda i: (0, 0), pipeline_mode=pl.Buffered(1))
    return pl.pallas_call(
        functools.partial(_tail_kernel, final),
        grid=(n // TAIL_TM,),
        in_specs=[
            pl.BlockSpec((TAIL_TM, d), tok),
            pl.BlockSpec((None, 1, 3 * d), lambda i: ((i * TAIL_TM) // seq, 0, 0)),
            pl.BlockSpec((TAIL_TM, d), tok),
            pl.BlockSpec((TAIL_TM, d), tok),
            pl.BlockSpec((TAIL_TM, d), lambda i: (i, za_col)),
            pl.BlockSpec((TAIL_TM, d), lambda i: (i, zb_col)),
            pl.BlockSpec((TAIL_TM, d), lambda i: (i, ga_col)),
            pl.BlockSpec((TAIL_TM, d), lambda i: (i, gb_col)),
            pl.BlockSpec((TAIL_TM, d), tok),
            wspec, wspec, wspec,
            pl.BlockSpec((1, d), lambda i: (0, 0)),
        ],
        out_specs=pl.BlockSpec((TAIL_TM, d), tok),
        out_shape=jax.ShapeDtypeStruct((n, d), F32),
        scratch_shapes=[pltpu.VMEM((3, d, d), BF16)],
        compiler_params=pltpu.CompilerParams(
            dimension_semantics=("arbitrary",), vmem_limit_bytes=VMEM_LIMIT),
        name="tail",
    )(x2, mod3, hf, hb, proj, proj, proj, proj, att2, w_pa, w_pb, w_o, g_final)


def kernel(x, c, g_pre, w_c, b_c, w_in, conv_w, conv_b, w_r_f, b_r_f, w_i_f, b_i_f, lam_f,
           w_r_b, b_r_b, w_i_b, b_i_b, lam_b, rpb, w_pa, w_pb, w_o, g_final):
    nb, seq, d = x.shape
    depth = g_pre.shape[0]
    n = nb * seq
    rows = seq // GRID_W
    groups = w_in.shape[2] // d
    xa_col, za_col, q_col, k_col, v_col, zb_col, ga_col, gb_col = range(groups)
    assert xa_col == 0
    acts = ["id"] * groups
    acts[za_col] = acts[zb_col] = "silu"
    acts[ga_col] = acts[gb_col] = "sigmoid"
    acts[q_col] = "qscale"

    x2 = x.reshape(n, d)
    for l in range(depth):
        mod3 = _modulation(c, w_c[l], b_c[l].reshape(1, 3 * d))

        proj = _projection(x2, mod3, g_pre[l].reshape(1, d), w_in[l], seq, tuple(acts))
        proj3 = proj.reshape(nb, seq, groups * d)

        cw, cb = conv_w[l], conv_b[l].reshape(1, d)
        hf, xh = _rglru(proj3, cw, cb, w_r_f[l], w_i_f[l], b_r_f[l].reshape(1, d),
                        b_i_f[l].reshape(1, d), lam_f[l].reshape(1, d), False)
        hb = _rglru(proj3, cw, cb, w_r_b[l], w_i_b[l], b_r_b[l].reshape(1, d),
                    b_i_b[l].reshape(1, d), lam_b[l].reshape(1, d), True, xh=xh)

        att = _natten(proj3, _bias_table(rpb[l], rows), q_col, k_col, v_col)

        x2 = _tail(x2, mod3, hf.reshape(n, d), hb.reshape(n, d), proj, att.reshape(n, d),
                   w_pa[l], w_pb[l], w_o[l],
                   g_final.reshape(1, d), seq, (za_col, zb_col, ga_col, gb_col),
                   final=(l == depth - 1))
    return x2.reshape(nb, seq, d)
```

```python
import functools

import jax
import jax.numpy as jnp
import numpy as np
from jax import lax
from jax.experimental import pallas as pl
from jax.experimental.pallas import tpu as pltpu

F32 = jnp.float32
BF16 = jnp.bfloat16

SUBLANES = 8
LANES = 128
MXU_TILE = 256

GRID_W = 64
A_BLOCKS = 16
CONV_W = 4
RG_C = 8.0
N_HEADS = 16
HEAD_DIM = 64
WIN_R = 8
WIN_C = 16
EPS = 1e-6
MASK_BIAS = -1e30
LOG2_E = 1.4426950408889634
RSQRT_FLOOR = 1e-30

PROJ_TM = 512
NORM_RB = 64
CAST_RB = 128
SCAN_TC = 128
TAIL_TM = 512
TAIL_SUB = 256
NAT_ROWS = 8
BIAS_STEPS = 2
MOD_ROWS = 16

VMEM_LIMIT = 56 * 1024 * 1024
NAT_VMEM_LIMIT = 61 * 1024 * 1024


def _sigmoid(x):
    return 0.5 * jnp.tanh(0.5 * x) + 0.5


def _silu(x):
    return x * _sigmoid(x)


_ACTIVATIONS = {
    "id": lambda v: v,
    "silu": _silu,
    "sigmoid": _sigmoid,
    "qscale": lambda v: v * (HEAD_DIM ** -0.5 * LOG2_E),
}


def _mod_kernel(c_ref, w_ref, b_ref, o_ref, pad_ref):
    nb = c_ref.shape[0]
    pad_ref[...] = jnp.zeros_like(pad_ref)
    pad_ref[0:nb, :] = c_ref[...]
    sc = _silu(pad_ref[...]).astype(BF16)
    w = w_ref[...].astype(BF16)
    mod = jnp.dot(sc, w, preferred_element_type=F32) + b_ref[...]
    for b in range(nb):
        o_ref[b] = mod[b:b + 1, :]


def _modulation(c, w_c, b_c):
    nb, d = c.shape
    assert nb <= MOD_ROWS
    return pl.pallas_call(
        _mod_kernel,
        grid=(3,),
        in_specs=[
            pl.BlockSpec((nb, d), lambda j: (0, 0)),
            pl.BlockSpec((d, d), lambda j: (0, j)),
            pl.BlockSpec((1, d), lambda j: (0, j)),
        ],
        out_specs=pl.BlockSpec((nb, 1, d), lambda j: (0, 0, j)),
        out_shape=jax.ShapeDtypeStruct((nb, 1, 3 * d), F32),
        scratch_shapes=[pltpu.VMEM((MOD_ROWS, d), F32)],
        compiler_params=pltpu.CompilerParams(
            dimension_semantics=("parallel",), vmem_limit_bytes=VMEM_LIMIT),
        name="mod",
    )(c, w_c, b_c)


def _norm_rows(x_ref, mod_ref, g, h_ref, slot, r0):
    d = x_ref.shape[1]
    x = x_ref[pl.ds(r0, NORM_RB), :]
    ms = jnp.mean(x * x, axis=-1, keepdims=True)
    y = x * lax.rsqrt(ms + EPS) * g
    h = y * (1.0 + mod_ref[:, d:2 * d]) + mod_ref[:, 0:d]
    h_ref[slot, pl.ds(r0, NORM_RB), :] = h.astype(BF16)


def _proj_kernel(acts, x0_ref, mod0_ref, xn_ref, modn_ref, g_ref, w_hbm, o_ref,
                 h_ref, w_ref, stage_ref, sem):
    i = pl.program_id(0)
    tm, d = xn_ref.shape
    g = g_ref[...]
    cur = i % 2

    def weight_copy(j):
        return pltpu.make_async_copy(w_hbm.at[:, pl.ds(j * d, d)], stage_ref.at[j % 2],
                                     sem.at[j % 2])

    def norm_next_tile():
        for k in range(tm // NORM_RB):
            _norm_rows(xn_ref, modn_ref, g, h_ref, 1 - cur, k * NORM_RB)

    def group(j):
        cols = slice(j * d, (j + 1) * d)
        acc = jnp.dot(h_ref[cur], w_ref[:, cols], preferred_element_type=F32)
        o_ref[:, cols] = _ACTIVATIONS[acts[j]](acc).astype(BF16)

    @pl.when(i == 0)
    def _():
        for j in range(min(2, len(acts))):
            weight_copy(j).start()

        def body(k, carry):
            _norm_rows(x0_ref, mod0_ref, g, h_ref, 0, pl.multiple_of(k * NORM_RB, NORM_RB))
            return carry
        lax.fori_loop(0, tm // NORM_RB, body, 0)

        for j in range(len(acts)):
            weight_copy(j).wait()

            def cast(k, carry, j=j):
                r0 = pl.multiple_of(k * CAST_RB, CAST_RB)
                w_ref[pl.ds(r0, CAST_RB), j * d:(j + 1) * d] = (
                    stage_ref[j % 2, pl.ds(r0, CAST_RB), :].astype(BF16))
                return carry
            lax.fori_loop(0, d // CAST_RB, cast, 0)
            if j + 2 < len(acts):
                weight_copy(j + 2).start()
            group(j)
        norm_next_tile()

    @pl.when(i > 0)
    def _():
        norm_next_tile()
        for j in range(len(acts)):
            group(j)


def _projection(x2, mod3, g_pre, w_in, seq, acts):
    n, d = x2.shape
    n_tiles = n // PROJ_TM
    nxt = lambda i: jnp.minimum(i + 1, n_tiles - 1)
    return pl.pallas_call(
        functools.partial(_proj_kernel, acts),
        grid=(n_tiles,),
        in_specs=[
            pl.BlockSpec((PROJ_TM, d), lambda i: (0, 0), pipeline_mode=pl.Buffered(1)),
            pl.BlockSpec((None, 1, 3 * d), lambda i: (0, 0, 0)),
            pl.BlockSpec((PROJ_TM, d), lambda i: (nxt(i), 0)),
            pl.BlockSpec((None, 1, 3 * d), lambda i: ((nxt(i) * PROJ_TM) // seq, 0, 0)),
            pl.BlockSpec((1, d), lambda i: (0, 0)),
            pl.BlockSpec(memory_space=pl.ANY),
        ],
        out_specs=pl.BlockSpec((PROJ_TM, w_in.shape[1]), lambda i: (i, 0)),
        out_shape=jax.ShapeDtypeStruct((n, w_in.shape[1]), BF16),
        scratch_shapes=[
            pltpu.VMEM((2, PROJ_TM, d), BF16),
            pltpu.VMEM(w_in.shape, BF16),
            pltpu.VMEM((2, d, d), w_in.dtype),
            pltpu.SemaphoreType.DMA((2,)),
        ],
        compiler_params=pltpu.CompilerParams(
            dimension_semantics=("arbitrary",), vmem_limit_bytes=VMEM_LIMIT),
        name="proj",
    )(x2, mod3, x2, mod3, g_pre, w_in)


def _rglru_kernel(rev, reuse, *refs):
    if reuse:
        xh_ref, wr_ref, wi_ref, br_ref, bi_ref, lam_ref, o_ref = refs[:7]
        wg_ref, sa_ref, su_ref, sh_ref, carry_ref = refs[7:]
        nb, tc, d = xh_ref.shape
    else:
        (xa_ref, prev_ref, next_ref, shift_ref, cw_ref, cb_ref, wr_ref, wi_ref, br_ref, bi_ref,
         lam_ref, o_ref, xh_out_ref) = refs[:13]
        wg_ref, sa_ref, su_ref, sh_ref, carry_ref = refs[13:]
        nb, tc, d = xa_ref.shape
    i = pl.program_id(0)
    n_chunks = pl.num_programs(0) - 1
    groups = tc // SUBLANES
    n_tiles = d // MXU_TILE
    n_blk, bw, _ = wr_ref.shape
    per_tile = MXU_TILE // bw
    ig = jnp.minimum(i, n_chunks - 1)
    ci = n_chunks - 1 - ig if rev else ig
    cur = i % 2
    old = 1 - cur

    @pl.when(i == 0)
    def _():
        carry_ref[...] = jnp.zeros_like(carry_ref)
        sa_ref[1] = jnp.zeros(sa_ref.shape[1:], F32)
        su_ref[1] = jnp.zeros(su_ref.shape[1:], F32)
        wg_ref[...] = jnp.zeros_like(wg_ref)
        for gate, w_ref in enumerate((wr_ref, wi_ref)):
            for blk in range(n_blk):
                off = (blk % per_tile) * bw
                wg_ref[gate, blk // per_tile, off:off + bw, off:off + bw] = (
                    w_ref[blk].astype(BF16))

    has_prev = ci > 0
    has_next = ci < n_chunks - 1
    lam = lam_ref[...]
    softplus_neg_lam = jnp.maximum(-lam, 0.0) + jnp.log1p(jnp.exp(-jnp.abs(lam)))
    c8l = (-0.5 * RG_C * LOG2_E) * softplus_neg_lam
    brh = 0.5 * br_ref[...]
    bih = 0.5 * bi_ref[...]
    if not reuse:
        cwh = 0.5 * cw_ref[...]
        cbh = 0.5 * cb_ref[...]
        shift_m = shift_ref[...]
        taps = [k for k in range(CONV_W) if k != CONV_W // 2]

    def scan_row(b):
        h = carry_ref[b]
        for gg in range(groups):
            g = groups - 1 - gg if rev else gg
            for jj in range(SUBLANES):
                j = SUBLANES - 1 - jj if rev else jj
                tok = pl.ds(j, SUBLANES, stride=SUBLANES)
                h = sa_ref[old, b, g, tok, :] * h + su_ref[old, b, g, tok, :]
                sh_ref[b, g, tok, :] = h
            if gg % 2 == 1:
                gp = g // 2
                r0 = gp * 2 * SUBLANES
                for cblk in range(d // LANES):
                    blk = slice(cblk * SUBLANES, (cblk + 1) * SUBLANES)
                    rows16 = jnp.concatenate(
                        [sh_ref[b, 2 * gp, blk, :], sh_ref[b, 2 * gp + 1, blk, :]], axis=0)
                    o_ref[b, pl.ds(r0, 2 * SUBLANES), cblk * LANES:(cblk + 1) * LANES] = (
                        rows16.astype(o_ref.dtype))
        carry_ref[b] = h

    def conv_half(b):
        xb = xa_ref[b]
        halo = jnp.concatenate([
            jnp.where(has_prev, prev_ref[b], jnp.zeros_like(prev_ref[b])),
            jnp.where(has_next, next_ref[b], jnp.zeros_like(next_ref[b]))], axis=0)
        sh = jnp.dot(shift_m, jnp.concatenate([xb, halo], axis=0), preferred_element_type=F32)
        xh = cbh + cwh[CONV_W // 2:CONV_W // 2 + 1] * xb.astype(F32)
        for n, k in enumerate(taps):
            xh = xh + cwh[k:k + 1] * sh[n * tc:(n + 1) * tc]
        return xh

    def gates(b):
        if reuse:
            xhb = xh_ref[b]
            xh = xhb.astype(F32)
        else:
            xh = conv_half(b)
            xhb = xh.astype(BF16)
            xh_out_ref[b] = xhb
        for k in range(n_tiles):
            sl = slice(k * MXU_TILE, (k + 1) * MXU_TILE)
            xk = xhb[:, sl]
            tr = jnp.tanh(jnp.dot(xk, wg_ref[0, k], preferred_element_type=F32) + brh[:, sl])
            ti = jnp.tanh(jnp.dot(xk, wg_ref[1, k], preferred_element_type=F32) + bih[:, sl])
            a = jnp.exp2(c8l[:, sl] * (tr + 1.0))
            z = 1.0 - a * a
            u = (z * lax.rsqrt(jnp.maximum(z, RSQRT_FLOOR))) * ((ti + 1.0) * xh[:, sl])
            for c2 in range(MXU_TILE // LANES):
                cblk = k * (MXU_TILE // LANES) + c2
                rows = slice(cblk * SUBLANES, (cblk + 1) * SUBLANES)
                lanes = slice(c2 * LANES, (c2 + 1) * LANES)
                sa_ref[cur, b, :, rows, :] = a[:, lanes].reshape(groups, SUBLANES, LANES)
                su_ref[cur, b, :, rows, :] = u[:, lanes].reshape(groups, SUBLANES, LANES)

    def body(b, carry):
        scan_row(b)
        gates(b)
        return carry

    lax.fori_loop(0, nb, body, 0)


def _shift_matrix(tc):
    taps = [k for k in range(CONV_W) if k != CONV_W // 2]
    m = np.zeros((len(taps) * tc, tc + 2 * SUBLANES), np.float32)
    for n, k in enumerate(taps):
        for t in range(tc):
            s = t - CONV_W // 2 + k
            if s < 0:
                col = tc + SUBLANES + s
            elif s >= tc:
                col = tc + SUBLANES + (s - tc)
            else:
                col = s
            m[n * tc + t, col] = 1.0
    return m


def _rglru(proj3, conv_w, conv_b, w_r, w_i, b_r, b_i, lam, rev, xh=None):
    nb, seq, _ = proj3.shape
    reuse = xh is not None
    d = w_r.shape[0] * w_r.shape[1]
    n_chunks = seq // SCAN_TC
    n_groups = seq // SUBLANES
    gpc = SCAN_TC // SUBLANES

    def chunk(i):
        return n_chunks - 1 - i if rev else i

    def gated(i):
        return chunk(jnp.minimum(i, n_chunks - 1))

    def scanned(i):
        return chunk(jnp.maximum(i - 1, 0))

    n_tiles = d // MXU_TILE
    gate_args = (w_r, w_i, b_r, b_i, lam)
    whole = lambda a: pl.BlockSpec(a.shape, lambda i: (0,) * a.ndim)
    gated_spec = pl.BlockSpec((nb, SCAN_TC, d), lambda i: (0, gated(i), 0))
    h_spec = pl.BlockSpec((nb, SCAN_TC, d), lambda i: (0, scanned(i), 0))
    h_shape = jax.ShapeDtypeStruct((nb, seq, d), BF16)
    if reuse:
        in_specs = [gated_spec] + [whole(a) for a in gate_args]
        args = (xh,) + gate_args
        out_specs, out_shape = h_spec, h_shape
    else:
        proj4 = proj3.reshape(nb, n_groups, SUBLANES, proj3.shape[2])
        shift_m = jnp.asarray(_shift_matrix(SCAN_TC), BF16)
        in_specs = [
            gated_spec,
            pl.BlockSpec((nb, None, SUBLANES, d),
                         lambda i: (0, jnp.maximum(gated(i) * gpc - 1, 0), 0, 0)),
            pl.BlockSpec((nb, None, SUBLANES, d),
                         lambda i: (0, jnp.minimum((gated(i) + 1) * gpc, n_groups - 1), 0, 0)),
            whole(shift_m), whole(conv_w), whole(conv_b),
        ] + [whole(a) for a in gate_args]
        args = (proj3, proj4, proj4, shift_m, conv_w, conv_b) + gate_args
        out_specs, out_shape = [h_spec, gated_spec], [h_shape, h_shape]
    return pl.pallas_call(
        functools.partial(_rglru_kernel, rev, reuse),
        grid=(n_chunks + 1,),
        in_specs=in_specs,
        out_specs=out_specs,
        out_shape=out_shape,
        scratch_shapes=[
            pltpu.VMEM((2, n_tiles, MXU_TILE, MXU_TILE), BF16),
            pltpu.VMEM((2, nb, gpc, d // LANES * SUBLANES, LANES), F32),
            pltpu.VMEM((2, nb, gpc, d // LANES * SUBLANES, LANES), F32),
            pltpu.VMEM((nb, gpc, d // LANES * SUBLANES, LANES), F32),
            pltpu.VMEM((nb, SUBLANES, LANES), F32),
        ],
        compiler_params=pltpu.CompilerParams(
            dimension_semantics=("arbitrary",), vmem_limit_bytes=VMEM_LIMIT),
        name="rglru_bwd" if rev else "rglru_fwd",
    )(*args)


def _natten_kernel(k_col, v_col, q_ref, kv_hbm, bias_ref, o_ref, k_buf, v_buf, sem):
    batch = pl.program_id(0)
    step = pl.program_id(1)
    n_steps = pl.num_programs(1)
    rows = n_steps * NAT_ROWS
    gw = GRID_W
    d = q_ref.shape[1]
    win_r = min(WIN_R, rows)
    held = k_buf.shape[1] // gw

    def first_held_row(s):
        return jnp.clip(s * NAT_ROWS - win_r // 2, 0, rows - held)

    def window_copies(b, s, slot):
        tok = pl.multiple_of(first_held_row(s) * gw, gw)
        return [pltpu.make_async_copy(kv_hbm.at[b, pl.ds(tok, held * gw), pl.ds(col * d, d)],
                                      buf.at[slot], sem.at[n, slot])
                for n, (col, buf) in enumerate(((k_col, k_buf), (v_col, v_buf)))]

    t = batch * n_steps + step
    slot = t % 2

    @pl.when(t == 0)
    def _():
        for c in window_copies(0, 0, 0):
            c.start()

    @pl.when(t + 1 < pl.num_programs(0) * n_steps)
    def _():
        wrap = step + 1 == n_steps
        for c in window_copies(jnp.where(wrap, batch + 1, batch), jnp.where(wrap, 0, step + 1),
                               1 - slot):
            c.start()

    for c in window_copies(batch, step, slot):
        c.wait()

    lane = lax.broadcasted_iota(jnp.int32, (gw, LANES), 1)
    lo = lane < HEAD_DIM
    zero = jnp.zeros((gw, LANES), BF16)
    n_pairs = d // LANES
    w0 = first_held_row(step)
    wins = []
    pats = []
    scores = []
    for rr in range(NAT_ROWS):
        r = step * NAT_ROWS + rr
        r0 = jnp.clip(r - win_r // 2, 0, rows - win_r)
        k0 = pl.multiple_of((r0 - w0) * gw, gw)
        wins.append(k0)
        pats.append(r0 - r + (WIN_R - 1))
        for p in range(n_pairs):
            sl = slice(p * LANES, (p + 1) * LANES)
            qp = q_ref[rr * gw:(rr + 1) * gw, sl]
            qs = jnp.concatenate([jnp.where(lo, qp, zero), jnp.where(lo, zero, qp)], axis=0)
            scores.append(lax.dot_general(qs, k_buf[slot, pl.ds(k0, win_r * gw), sl],
                                          (((1,), (1,)), ((), ())), preferred_element_type=F32))
    probs = []
    for rr in range(NAT_ROWS):
        for p in range(n_pairs):
            bias = jnp.concatenate([bias_ref[pats[rr] + 2 * t, p] for t in range(win_r // 2)],
                                   axis=1)
            s = scores[rr * n_pairs + p] + bias
            m = jnp.max(s, axis=-1, keepdims=True)
            probs.append(jnp.exp2(s - m).astype(BF16))
    ones = jnp.ones((win_r * gw, LANES), BF16)
    for rr in range(NAT_ROWS):
        for p in range(n_pairs):
            sl = slice(p * LANES, (p + 1) * LANES)
            v_ext = jnp.concatenate([v_buf[slot, pl.ds(wins[rr], win_r * gw), sl], ones], axis=1)
            pv = jnp.dot(probs[rr * n_pairs + p], v_ext, preferred_element_type=F32)
            o = pv[:, :LANES] / pv[:, LANES:]
            o_ref[rr * gw:(rr + 1) * gw, sl] = jnp.where(lo, o[:gw], o[gw:]).astype(BF16)


def _natten(proj3, bias_tab, q_col, k_col, v_col):
    nb, seq, _ = proj3.shape
    d = N_HEADS * HEAD_DIM
    rows = seq // GRID_W
    held = min(rows, NAT_ROWS + min(WIN_R, rows))
    return pl.pallas_call(
        functools.partial(_natten_kernel, k_col, v_col),
        grid=(nb, rows // NAT_ROWS),
        in_specs=[
            pl.BlockSpec((None, NAT_ROWS * GRID_W, d), lambda b, i: (b, i, q_col)),
            pl.BlockSpec(memory_space=pl.ANY),
            pl.BlockSpec(bias_tab.shape, lambda b, i: (0, 0, 0, 0), pipeline_mode=pl.Buffered(1)),
        ],
        out_specs=pl.BlockSpec((None, NAT_ROWS * GRID_W, d), lambda b, i: (b, i, 0)),
        out_shape=jax.ShapeDtypeStruct((nb, seq, d), BF16),
        scratch_shapes=[pltpu.VMEM((2, held * GRID_W, d), BF16),
                        pltpu.VMEM((2, held * GRID_W, d), BF16),
                        pltpu.SemaphoreType.DMA((2, 2))],
        compiler_params=pltpu.CompilerParams(
            dimension_semantics=("arbitrary", "arbitrary"), vmem_limit_bytes=NAT_VMEM_LIMIT),
        name="natten",
    )(proj3, proj3, bias_tab)


def _bias_kernel(rp_ref, o_ref):
    per_step, n_pairs = o_ref.shape[:2]
    gw = GRID_W
    q_idx = lax.broadcasted_iota(jnp.int32, (gw, LANES), 0)
    lane = lax.broadcasted_iota(jnp.int32, (gw, LANES), 1)
    col = jnp.where(lane < gw, lane, lane - gw)
    c0 = jnp.clip(q_idx - WIN_C // 2, 0, gw - WIN_C)
    valid = (col >= c0) & (col < c0 + WIN_C)
    first = lane < gw
    for k in range(per_step):
        e = pl.program_id(0) * per_step + k
        for pair in range(n_pairs):
            for hh in range(2):
                h = 2 * pair + hh
                va = jnp.broadcast_to(rp_ref[h, pl.ds(e, 1), :], (gw, LANES))
                vb = jnp.broadcast_to(rp_ref[h, pl.ds(e + 1, 1), :], (gw, LANES))
                ta = pltpu.roll(va, 0, 1, stride=1, stride_axis=0)
                tb = pltpu.roll(vb, gw, 1, stride=1, stride_axis=0)
                tile = jnp.where(valid, LOG2_E * jnp.where(first, ta, tb), MASK_BIAS)
                o_ref[k, pair, hh * gw:(hh + 1) * gw, :] = tile


def _bias_table(rpb, rows):
    win_r = min(WIN_R, rows)
    assert 2 * GRID_W == LANES and win_r % 2 == 0
    nh, nd, _ = rpb.shape
    rp = jnp.concatenate([rpb[:, :, WIN_C - 1:],
                          jnp.zeros((nh, nd, LANES - (2 * WIN_C - 1)), rpb.dtype),
                          rpb[:, :, :WIN_C - 1]], axis=-1).astype(F32)
    n_off = WIN_R + win_r - 2
    assert n_off % BIAS_STEPS == 0
    return pl.pallas_call(
        _bias_kernel,
        grid=(BIAS_STEPS,),
        in_specs=[pl.BlockSpec(rp.shape, lambda e: (0, 0, 0))],
        out_specs=pl.BlockSpec((n_off // BIAS_STEPS, nh // 2, 2 * GRID_W, LANES),
                               lambda e: (e, 0, 0, 0)),
        out_shape=jax.ShapeDtypeStruct((n_off, nh // 2, 2 * GRID_W, LANES), F32),
        compiler_params=pltpu.CompilerParams(
            dimension_semantics=("parallel",), vmem_limit_bytes=VMEM_LIMIT),
        name="bias_table",
    )(rp)


def _tail_kernel(final, x_ref, mod_ref, hf_ref, hb_ref, za_ref, zb_ref, ga_ref, gb_ref, att_ref,
                 wpa_ref, wpb_ref, wo_ref, gf_ref, o_ref, w_scr):
    d = x_ref.shape[1]

    @pl.when(pl.program_id(0) == 0)
    def _():
        for n, w_ref in enumerate((wpa_ref, wpb_ref, wo_ref)):
            w_scr[n] = w_ref[...].astype(BF16)

    for s in range(x_ref.shape[0] // TAIL_SUB):
        rows = slice(s * TAIL_SUB, (s + 1) * TAIL_SUB)
        h = hf_ref[rows, :].astype(F32) + hb_ref[rows, :].astype(F32)
        ya = (h * za_ref[rows, :].astype(F32)).astype(BF16)
        ya = jnp.dot(ya, w_scr[0], preferred_element_type=F32)
        yb = (att_ref[rows, :].astype(F32) * zb_ref[rows, :].astype(F32)).astype(BF16)
        yb = jnp.dot(yb, w_scr[1], preferred_element_type=F32)
        merged = ga_ref[rows, :].astype(F32) * ya + gb_ref[rows, :].astype(F32) * yb
        res = jnp.dot(merged.astype(BF16), w_scr[2], preferred_element_type=F32)
        xo = x_ref[rows, :] + mod_ref[:, 2 * d:3 * d] * res
        if final:
            ms = jnp.mean(xo * xo, axis=-1, keepdims=True)
            xo = xo * lax.rsqrt(ms + EPS) * gf_ref[...]
        o_ref[rows, :] = xo


def _tail(x2, mod3, hf, hb, proj, att2, w_pa, w_pb, w_o, g_final, seq, cols, final):
    n, d = x2.shape
    za_col, zb_col, ga_col, gb_col = cols
    tok = lambda i: (i, 0)
    wspec = pl.BlockSpec((d, d), lambda i: (0, 0), pipeline_mode=pl.Buffered(1))
    return pl.pallas_call(
        functools.partial(_tail_kernel, final),
        grid=(n // TAIL_TM,),
        in_specs=[
            pl.BlockSpec((TAIL_TM, d), tok),
            pl.BlockSpec((None, 1, 3 * d), lambda i: ((i * TAIL_TM) // seq, 0, 0)),
            pl.BlockSpec((TAIL_TM, d), tok),
            pl.BlockSpec((TAIL_TM, d), tok),
            pl.BlockSpec((TAIL_TM, d), lambda i: (i, za_col)),
            pl.BlockSpec((TAIL_TM, d), lambda i: (i, zb_col)),
            pl.BlockSpec((TAIL_TM, d), lambda i: (i, ga_col)),
            pl.BlockSpec((TAIL_TM, d), lambda i: (i, gb_col)),
            pl.BlockSpec((TAIL_TM, d), tok),
            wspec, wspec, wspec,
            pl.BlockSpec((1, d), lambda i: (0, 0)),
        ],
        out_specs=pl.BlockSpec((TAIL_TM, d), tok),
        out_shape=jax.ShapeDtypeStruct((n, d), F32),
        scratch_shapes=[pltpu.VMEM((3, d, d), BF16)],
        compiler_params=pltpu.CompilerParams(
            dimension_semantics=("arbitrary",), vmem_limit_bytes=VMEM_LIMIT),
        name="tail",
    )(x2, mod3, hf, hb, proj, proj, proj, proj, att2, w_pa, w_pb, w_o, g_final)


def kernel(x, c, g_pre, w_c, b_c, w_in, conv_w, conv_b, w_r_f, b_r_f, w_i_f, b_i_f, lam_f,
           w_r_b, b_r_b, w_i_b, b_i_b, lam_b, rpb, w_pa, w_pb, w_o, g_final):
    nb, seq, d = x.shape
    depth = g_pre.shape[0]
    n = nb * seq
    rows = seq // GRID_W
    groups = w_in.shape[2] // d
    xa_col, za_col, q_col, k_col, v_col, zb_col, ga_col, gb_col = range(groups)
    assert xa_col == 0
    acts = ["id"] * groups
    acts[za_col] = acts[zb_col] = "silu"
    acts[ga_col] = acts[gb_col] = "sigmoid"
    acts[q_col] = "qscale"

    x2 = x.reshape(n, d)
    for l in range(depth):
        mod3 = _modulation(c, w_c[l], b_c[l].reshape(1, 3 * d))

        proj = _projection(x2, mod3, g_pre[l].reshape(1, d), w_in[l], seq, tuple(acts))
        proj3 = proj.reshape(nb, seq, groups * d)

        cw, cb = conv_w[l], conv_b[l].reshape(1, d)
        hf, xh = _rglru(proj3, cw, cb, w_r_f[l], w_i_f[l], b_r_f[l].reshape(1, d),
                        b_i_f[l].reshape(1, d), lam_f[l].reshape(1, d), False)
        hb = _rglru(proj3, cw, cb, w_r_b[l], w_i_b[l], b_r_b[l].reshape(1, d),
                    b_i_b[l].reshape(1, d), lam_b[l].reshape(1, d), True, xh=xh)

        att = _natten(proj3, _bias_table(rpb[l], rows), q_col, k_col, v_col)

        x2 = _tail(x2, mod3, hf.reshape(n, d), hb.reshape(n, d), proj, att.reshape(n, d),
                   w_pa[l], w_pb[l], w_o[l],
                   g_final.reshape(1, d), seq, (za_col, zb_col, ga_col, gb_col),
                   final=(l == depth - 1))
    return x2.reshape(nb, seq, d)
```

```python
import functools

import jax
import jax.numpy as jnp
import numpy as np
from jax import lax
from jax.experimental import pallas as pl
from jax.experimental.pallas import tpu as pltpu

F32 = jnp.float32
BF16 = jnp.bfloat16

SUBLANES = 8
LANES = 128
MXU_TILE = 256

GRID_W = 64
A_BLOCKS = 16
CONV_W = 4
RG_C = 8.0
N_HEADS = 16
HEAD_DIM = 64
WIN_R = 8
WIN_C = 16
EPS = 1e-6
MASK_BIAS = -1e30
LOG2_E = 1.4426950408889634
RSQRT_FLOOR = 1e-30

PROJ_TM = 512
NORM_RB = 64
CAST_RB = 128
SCAN_TC = 128
TAIL_TM = 512
TAIL_SUB = 256
NAT_ROWS = 8
BIAS_STEPS = 2
MOD_ROWS = 16

VMEM_LIMIT = 56 * 1024 * 1024


def _sigmoid(x):
    return 0.5 * jnp.tanh(0.5 * x) + 0.5


def _silu(x):
    return x * _sigmoid(x)


_ACTIVATIONS = {
    "id": lambda v: v,
    "silu": _silu,
    "sigmoid": _sigmoid,
    "qscale": lambda v: v * (HEAD_DIM ** -0.5 * LOG2_E),
}


def _mod_kernel(c_ref, w_ref, b_ref, o_ref, pad_ref):
    nb = c_ref.shape[0]
    pad_ref[...] = jnp.zeros_like(pad_ref)
    pad_ref[0:nb, :] = c_ref[...]
    sc = _silu(pad_ref[...]).astype(BF16)
    w = w_ref[...].astype(BF16)
    mod = jnp.dot(sc, w, preferred_element_type=F32) + b_ref[...]
    for b in range(nb):
        o_ref[b] = mod[b:b + 1, :]


def _modulation(c, w_c, b_c):
    nb, d = c.shape
    assert nb <= MOD_ROWS
    return pl.pallas_call(
        _mod_kernel,
        grid=(3,),
        in_specs=[
            pl.BlockSpec((nb, d), lambda j: (0, 0)),
            pl.BlockSpec((d, d), lambda j: (0, j)),
            pl.BlockSpec((1, d), lambda j: (0, j)),
        ],
        out_specs=pl.BlockSpec((nb, 1, d), lambda j: (0, 0, j)),
        out_shape=jax.ShapeDtypeStruct((nb, 1, 3 * d), F32),
        scratch_shapes=[pltpu.VMEM((MOD_ROWS, d), F32)],
        compiler_params=pltpu.CompilerParams(
            dimension_semantics=("parallel",), vmem_limit_bytes=VMEM_LIMIT),
        name="mod",
    )(c, w_c, b_c)


def _norm_rows(x_ref, mod_ref, g, h_ref, slot, r0):
    d = x_ref.shape[1]
    x = x_ref[pl.ds(r0, NORM_RB), :]
    ms = jnp.mean(x * x, axis=-1, keepdims=True)
    y = x * lax.rsqrt(ms + EPS) * g
    h = y * (1.0 + mod_ref[:, d:2 * d]) + mod_ref[:, 0:d]
    h_ref[slot, pl.ds(r0, NORM_RB), :] = h.astype(BF16)


def _proj_kernel(acts, x0_ref, mod0_ref, xn_ref, modn_ref, g_ref, w_hbm, o_ref,
                 h_ref, w_ref, stage_ref, sem):
    i = pl.program_id(0)
    tm, d = xn_ref.shape
    g = g_ref[...]
    cur = i % 2

    def weight_copy(j):
        return pltpu.make_async_copy(w_hbm.at[:, pl.ds(j * d, d)], stage_ref.at[j % 2],
                                     sem.at[j % 2])

    def norm_next_tile():
        for k in range(tm // NORM_RB):
            _norm_rows(xn_ref, modn_ref, g, h_ref, 1 - cur, k * NORM_RB)

    def group(j):
        cols = slice(j * d, (j + 1) * d)
        acc = jnp.dot(h_ref[cur], w_ref[:, cols], preferred_element_type=F32)
        o_ref[:, cols] = _ACTIVATIONS[acts[j]](acc).astype(BF16)

    @pl.when(i == 0)
    def _():
        for j in range(min(2, len(acts))):
            weight_copy(j).start()

        def body(k, carry):
            _norm_rows(x0_ref, mod0_ref, g, h_ref, 0, pl.multiple_of(k * NORM_RB, NORM_RB))
            return carry
        lax.fori_loop(0, tm // NORM_RB, body, 0)

        for j in range(len(acts)):
            weight_copy(j).wait()

            def cast(k, carry, j=j):
                r0 = pl.multiple_of(k * CAST_RB, CAST_RB)
                w_ref[pl.ds(r0, CAST_RB), j * d:(j + 1) * d] = (
                    stage_ref[j % 2, pl.ds(r0, CAST_RB), :].astype(BF16))
                return carry
            lax.fori_loop(0, d // CAST_RB, cast, 0)
            if j + 2 < len(acts):
                weight_copy(j + 2).start()
            group(j)
        norm_next_tile()

    @pl.when(i > 0)
    def _():
        norm_next_tile()
        for j in range(len(acts)):
            group(j)


def _projection(x2, mod3, g_pre, w_in, seq, acts):
    n, d = x2.shape
    n_tiles = n // PROJ_TM
    nxt = lambda i: jnp.minimum(i + 1, n_tiles - 1)
    return pl.pallas_call(
        functools.partial(_proj_kernel, acts),
        grid=(n_tiles,),
        in_specs=[
            pl.BlockSpec((PROJ_TM, d), lambda i: (0, 0), pipeline_mode=pl.Buffered(1)),
            pl.BlockSpec((None, 1, 3 * d), lambda i: (0, 0, 0)),
            pl.BlockSpec((PROJ_TM, d), lambda i: (nxt(i), 0)),
            pl.BlockSpec((None, 1, 3 * d), lambda i: ((nxt(i) * PROJ_TM) // seq, 0, 0)),
            pl.BlockSpec((1, d), lambda i: (0, 0)),
            pl.BlockSpec(memory_space=pl.ANY),
        ],
        out_specs=pl.BlockSpec((PROJ_TM, w_in.shape[1]), lambda i: (i, 0)),
        out_shape=jax.ShapeDtypeStruct((n, w_in.shape[1]), BF16),
        scratch_shapes=[
            pltpu.VMEM((2, PROJ_TM, d), BF16),
            pltpu.VMEM(w_in.shape, BF16),
            pltpu.VMEM((2, d, d), w_in.dtype),
            pltpu.SemaphoreType.DMA((2,)),
        ],
        compiler_params=pltpu.CompilerParams(
            dimension_semantics=("arbitrary",), vmem_limit_bytes=VMEM_LIMIT),
        name="proj",
    )(x2, mod3, x2, mod3, g_pre, w_in)


def _rglru_kernel(rev, reuse, *refs):
    if reuse:
        xh_ref, wr_ref, wi_ref, br_ref, bi_ref, lam_ref, o_ref = refs[:7]
        wg_ref, sa_ref, su_ref, sh_ref, carry_ref = refs[7:]
        nb, tc, d = xh_ref.shape
    else:
        (xa_ref, prev_ref, next_ref, shift_ref, cw_ref, cb_ref, wr_ref, wi_ref, br_ref, bi_ref,
         lam_ref, o_ref, xh_out_ref) = refs[:13]
        wg_ref, sa_ref, su_ref, sh_ref, carry_ref = refs[13:]
        nb, tc, d = xa_ref.shape
    i = pl.program_id(0)
    n_chunks = pl.num_programs(0) - 1
    groups = tc // SUBLANES
    n_tiles = d // MXU_TILE
    n_blk, bw, _ = wr_ref.shape
    per_tile = MXU_TILE // bw
    ig = jnp.minimum(i, n_chunks - 1)
    ci = n_chunks - 1 - ig if rev else ig
    cur = i % 2
    old = 1 - cur

    @pl.when(i == 0)
    def _():
        carry_ref[...] = jnp.zeros_like(carry_ref)
        sa_ref[1] = jnp.zeros(sa_ref.shape[1:], F32)
        su_ref[1] = jnp.zeros(su_ref.shape[1:], F32)
        wg_ref[...] = jnp.zeros_like(wg_ref)
        for gate, w_ref in enumerate((wr_ref, wi_ref)):
            for blk in range(n_blk):
                off = (blk % per_tile) * bw
                wg_ref[gate, blk // per_tile, off:off + bw, off:off + bw] = (
                    w_ref[blk].astype(BF16))

    has_prev = ci > 0
    has_next = ci < n_chunks - 1
    lam = lam_ref[...]
    softplus_neg_lam = jnp.maximum(-lam, 0.0) + jnp.log1p(jnp.exp(-jnp.abs(lam)))
    c8l = (-0.5 * RG_C * LOG2_E) * softplus_neg_lam
    brh = 0.5 * br_ref[...]
    bih = 0.5 * bi_ref[...]
    if not reuse:
        cwh = 0.5 * cw_ref[...]
        cbh = 0.5 * cb_ref[...]
        shift_m = shift_ref[...]
        taps = [k for k in range(CONV_W) if k != CONV_W // 2]

    def scan_row(b):
        h = carry_ref[b]
        for gg in range(groups):
            g = groups - 1 - gg if rev else gg
            for jj in range(SUBLANES):
                j = SUBLANES - 1 - jj if rev else jj
                tok = pl.ds(j, SUBLANES, stride=SUBLANES)
                h = sa_ref[old, b, g, tok, :] * h + su_ref[old, b, g, tok, :]
                sh_ref[b, g, tok, :] = h
            if gg % 2 == 1:
                gp = g // 2
                r0 = gp * 2 * SUBLANES
                for cblk in range(d // LANES):
                    blk = slice(cblk * SUBLANES, (cblk + 1) * SUBLANES)
                    rows16 = jnp.concatenate(
                        [sh_ref[b, 2 * gp, blk, :], sh_ref[b, 2 * gp + 1, blk, :]], axis=0)
                    o_ref[b, pl.ds(r0, 2 * SUBLANES), cblk * LANES:(cblk + 1) * LANES] = (
                        rows16.astype(o_ref.dtype))
        carry_ref[b] = h

    def conv_half(b):
        xb = xa_ref[b]
        halo = jnp.concatenate([
            jnp.where(has_prev, prev_ref[b], jnp.zeros_like(prev_ref[b])),
            jnp.where(has_next, next_ref[b], jnp.zeros_like(next_ref[b]))], axis=0)
        sh = jnp.dot(shift_m, jnp.concatenate([xb, halo], axis=0), preferred_element_type=F32)
        xh = cbh + cwh[CONV_W // 2:CONV_W // 2 + 1] * xb.astype(F32)
        for n, k in enumerate(taps):
            xh = xh + cwh[k:k + 1] * sh[n * tc:(n + 1) * tc]
        return xh

    def gates(b):
        if reuse:
            xhb = xh_ref[b]
            xh = xhb.astype(F32)
        else:
            xh = conv_half(b)
            xhb = xh.astype(BF16)
            xh_out_ref[b] = xhb
        for k in range(n_tiles):
            sl = slice(k * MXU_TILE, (k + 1) * MXU_TILE)
            xk = xhb[:, sl]
            tr = jnp.tanh(jnp.dot(xk, wg_ref[0, k], preferred_element_type=F32) + brh[:, sl])
            ti = jnp.tanh(jnp.dot(xk, wg_ref[1, k], preferred_element_type=F32) + bih[:, sl])
            a = jnp.exp2(c8l[:, sl] * (tr + 1.0))
            z = 1.0 - a * a
            u = (z * lax.rsqrt(jnp.maximum(z, RSQRT_FLOOR))) * ((ti + 1.0) * xh[:, sl])
            for c2 in range(MXU_TILE // LANES):
                cblk = k * (MXU_TILE // LANES) + c2
                rows = slice(cblk * SUBLANES, (cblk + 1) * SUBLANES)
                lanes = slice(c2 * LANES, (c2 + 1) * LANES)
                sa_ref[cur, b, :, rows, :] = a[:, lanes].reshape(groups, SUBLANES, LANES)
                su_ref[cur, b, :, rows, :] = u[:, lanes].reshape(groups, SUBLANES, LANES)

    def body(b, carry):
        scan_row(b)
        gates(b)
        return carry

    lax.fori_loop(0, nb, body, 0)


def _shift_matrix(tc):
    taps = [k for k in range(CONV_W) if k != CONV_W // 2]
    m = np.zeros((len(taps) * tc, tc + 2 * SUBLANES), np.float32)
    for n, k in enumerate(taps):
        for t in range(tc):
            s = t - CONV_W // 2 + k
            if s < 0:
                col = tc + SUBLANES + s
            elif s >= tc:
                col = tc + SUBLANES + (s - tc)
            else:
                col = s
            m[n * tc + t, col] = 1.0
    return m


def _rglru(proj3, conv_w, conv_b, w_r, w_i, b_r, b_i, lam, rev, xh=None):
    nb, seq, _ = proj3.shape
    reuse = xh is not None
    d = w_r.shape[0] * w_r.shape[1]
    n_chunks = seq // SCAN_TC
    n_groups = seq // SUBLANES
    gpc = SCAN_TC // SUBLANES

    def chunk(i):
        return n_chunks - 1 - i if rev else i

    def gated(i):
        return chunk(jnp.minimum(i, n_chunks - 1))

    def scanned(i):
        return chunk(jnp.maximum(i - 1, 0))

    n_tiles = d // MXU_TILE
    gate_args = (w_r, w_i, b_r, b_i, lam)
    whole = lambda a: pl.BlockSpec(a.shape, lambda i: (0,) * a.ndim)
    gated_spec = pl.BlockSpec((nb, SCAN_TC, d), lambda i: (0, gated(i), 0))
    h_spec = pl.BlockSpec((nb, SCAN_TC, d), lambda i: (0, scanned(i), 0))
    h_shape = jax.ShapeDtypeStruct((nb, seq, d), BF16)
    if reuse:
        in_specs = [gated_spec] + [whole(a) for a in gate_args]
        args = (xh,) + gate_args
        out_specs, out_shape = h_spec, h_shape
    else:
        proj4 = proj3.reshape(nb, n_groups, SUBLANES, proj3.shape[2])
        shift_m = jnp.asarray(_shift_matrix(SCAN_TC), BF16)
        in_specs = [
            gated_spec,
            pl.BlockSpec((nb, None, SUBLANES, d),
                         lambda i: (0, jnp.maximum(gated(i) * gpc - 1, 0), 0, 0)),
            pl.BlockSpec((nb, None, SUBLANES, d),
                         lambda i: (0, jnp.minimum((gated(i) + 1) * gpc, n_groups - 1), 0, 0)),
            whole(shift_m), whole(conv_w), whole(conv_b),
        ] + [whole(a) for a in gate_args]
        args = (proj3, proj4, proj4, shift_m, conv_w, conv_b) + gate_args
        out_specs, out_shape = [h_spec, gated_spec], [h_shape, h_shape]
    return pl.pallas_call(
        functools.partial(_rglru_kernel, rev, reuse),
        grid=(n_chunks + 1,),
        in_specs=in_specs,
        out_specs=out_specs,
        out_shape=out_shape,
        scratch_shapes=[
            pltpu.VMEM((2, n_tiles, MXU_TILE, MXU_TILE), BF16),
            pltpu.VMEM((2, nb, gpc, d // LANES * SUBLANES, LANES), F32),
            pltpu.VMEM((2, nb, gpc, d // LANES * SUBLANES, LANES), F32),
            pltpu.VMEM((nb, gpc, d // LANES * SUBLANES, LANES), F32),
            pltpu.VMEM((nb, SUBLANES, LANES), F32),
        ],
        compiler_params=pltpu.CompilerParams(
            dimension_semantics=("arbitrary",), vmem_limit_bytes=VMEM_LIMIT),
        name="rglru_bwd" if rev else "rglru_fwd",
    )(*args)


def _natten_kernel(k_col, v_col, q_ref, kv_hbm, bias_ref, o_ref, k_buf, v_buf, sem):
    batch = pl.program_id(0)
    step = pl.program_id(1)
    n_steps = pl.num_programs(1)
    rows = n_steps * NAT_ROWS
    gw = GRID_W
    d = q_ref.shape[1]
    win_r = min(WIN_R, rows)
    held = k_buf.shape[1] // gw

    def first_held_row(s):
        return jnp.clip(s * NAT_ROWS - win_r // 2, 0, rows - held)

    def window_copies(b, s, slot):
        tok = pl.multiple_of(first_held_row(s) * gw, gw)
        return [pltpu.make_async_copy(kv_hbm.at[b, pl.ds(tok, held * gw), pl.ds(col * d, d)],
                                      buf.at[slot], sem.at[n, slot])
                for n, (col, buf) in enumerate(((k_col, k_buf), (v_col, v_buf)))]

    t = batch * n_steps + step
    slot = t % 2

    @pl.when(t == 0)
    def _():
        for c in window_copies(0, 0, 0):
            c.start()

    @pl.when(t + 1 < pl.num_programs(0) * n_steps)
    def _():
        wrap = step + 1 == n_steps
        for c in window_copies(jnp.where(wrap, batch + 1, batch), jnp.where(wrap, 0, step + 1),
                               1 - slot):
            c.start()

    for c in window_copies(batch, step, slot):
        c.wait()

    lane = lax.broadcasted_iota(jnp.int32, (gw, LANES), 1)
    lo = lane < HEAD_DIM
    zero = jnp.zeros((gw, LANES), BF16)
    n_pairs = d // LANES
    w0 = first_held_row(step)
    wins = []
    pats = []
    scores = []
    for rr in range(NAT_ROWS):
        r = step * NAT_ROWS + rr
        r0 = jnp.clip(r - win_r // 2, 0, rows - win_r)
        k0 = pl.multiple_of((r0 - w0) * gw, gw)
        wins.append(k0)
        pats.append(r0 - r + (WIN_R - 1))
        for p in range(n_pairs):
            sl = slice(p * LANES, (p + 1) * LANES)
            qp = q_ref[rr * gw:(rr + 1) * gw, sl]
            qs = jnp.concatenate([jnp.where(lo, qp, zero), jnp.where(lo, zero, qp)], axis=0)
            scores.append(lax.dot_general(qs, k_buf[slot, pl.ds(k0, win_r * gw), sl],
                                          (((1,), (1,)), ((), ())), preferred_element_type=F32))
    probs = []
    for rr in range(NAT_ROWS):
        for p in range(n_pairs):
            bias = jnp.concatenate([bias_ref[pats[rr] + 2 * t, p] for t in range(win_r // 2)],
                                   axis=1)
            s = scores[rr * n_pairs + p] + bias
            m = jnp.max(s, axis=-1, keepdims=True)
            probs.append(jnp.exp2(s - m).astype(BF16))
    ones = jnp.ones((win_r * gw, LANES), BF16)
    for rr in range(NAT_ROWS):
        for p in range(n_pairs):
            sl = slice(p * LANES, (p + 1) * LANES)
            v_ext = jnp.concatenate([v_buf[slot, pl.ds(wins[rr], win_r * gw), sl], ones], axis=1)
            pv = jnp.dot(probs[rr * n_pairs + p], v_ext, preferred_element_type=F32)
            o = pv[:, :LANES] / pv[:, LANES:]
            o_ref[rr * gw:(rr + 1) * gw, sl] = jnp.where(lo, o[:gw], o[gw:]).astype(BF16)


def _natten(proj3, bias_tab, q_col, k_col, v_col):
    nb, seq, _ = proj3.shape
    d = N_HEADS * HEAD_DIM
    rows = seq // GRID_W
    held = min(rows, NAT_ROWS + min(WIN_R, rows))
    return pl.pallas_call(
        functools.partial(_natten_kernel, k_col, v_col),
        grid=(nb, rows // NAT_ROWS),
        in_specs=[
            pl.BlockSpec((None, NAT_ROWS * GRID_W, d), lambda b, i: (b, i, q_col)),
            pl.BlockSpec(memory_space=pl.ANY),
            pl.BlockSpec(bias_tab.shape, lambda b, i: (0, 0, 0, 0), pipeline_mode=pl.Buffered(1)),
        ],
        out_specs=pl.BlockSpec((None, NAT_ROWS * GRID_W, d), lambda b, i: (b, i, 0)),
        out_shape=jax.ShapeDtypeStruct((nb, seq, d), BF16),
        scratch_shapes=[pltpu.VMEM((2, held * GRID_W, d), BF16),
                        pltpu.VMEM((2, held * GRID_W, d), BF16),
                        pltpu.SemaphoreType.DMA((2, 2))],
        compiler_params=pltpu.CompilerParams(
            dimension_semantics=("arbitrary", "arbitrary"), vmem_limit_bytes=VMEM_LIMIT),
        name="natten",
    )(proj3, proj3, bias_tab)


def _bias_kernel(rp_ref, o_ref):
    per_step, n_pairs = o_ref.shape[:2]
    gw = GRID_W
    q_idx = lax.broadcasted_iota(jnp.int32, (gw, LANES), 0)
    lane = lax.broadcasted_iota(jnp.int32, (gw, LANES), 1)
    col = jnp.where(lane < gw, lane, lane - gw)
    c0 = jnp.clip(q_idx - WIN_C // 2, 0, gw - WIN_C)
    valid = (col >= c0) & (col < c0 + WIN_C)
    first = lane < gw
    for k in range(per_step):
        e = pl.program_id(0) * per_step + k
        for pair in range(n_pairs):
            for hh in range(2):
                h = 2 * pair + hh
                va = jnp.broadcast_to(rp_ref[h, pl.ds(e, 1), :], (gw, LANES))
                vb = jnp.broadcast_to(rp_ref[h, pl.ds(e + 1, 1), :], (gw, LANES))
                ta = pltpu.roll(va, 0, 1, stride=1, stride_axis=0)
                tb = pltpu.roll(vb, gw, 1, stride=1, stride_axis=0)
                tile = jnp.where(valid, LOG2_E * jnp.where(first, ta, tb), MASK_BIAS)
                o_ref[k, pair, hh * gw:(hh + 1) * gw, :] = tile


def _bias_table(rpb, rows):
    win_r = min(WIN_R, rows)
    assert 2 * GRID_W == LANES and win_r % 2 == 0
    nh, nd, _ = rpb.shape
    rp = jnp.concatenate([rpb[:, :, WIN_C - 1:],
                          jnp.zeros((nh, nd, LANES - (2 * WIN_C - 1)), rpb.dtype),
                          rpb[:, :, :WIN_C - 1]], axis=-1).astype(F32)
    n_off = WIN_R + win_r - 2
    assert n_off % BIAS_STEPS == 0
    return pl.pallas_call(
        _bias_kernel,
        grid=(BIAS_STEPS,),
        in_specs=[pl.BlockSpec(rp.shape, lambda e: (0, 0, 0))],
        out_specs=pl.BlockSpec((n_off // BIAS_STEPS, nh // 2, 2 * GRID_W, LANES),
                               lambda e: (e, 0, 0, 0)),
        out_shape=jax.ShapeDtypeStruct((n_off, nh // 2, 2 * GRID_W, LANES), F32),
        compiler_params=pltpu.CompilerParams(
            dimension_semantics=("parallel",), vmem_limit_bytes=VMEM_LIMIT),
        name="bias_table",
    )(rp)


def _tail_kernel(final, x_ref, mod_ref, hf_ref, hb_ref, za_ref, zb_ref, ga_ref, gb_ref, att_ref,
                 wpa_ref, wpb_ref, wo_ref, gf_ref, o_ref, w_scr):
    d = x_ref.shape[1]

    @pl.when(pl.program_id(0) == 0)
    def _():
        for n, w_ref in enumerate((wpa_ref, wpb_ref, wo_ref)):
            w_scr[n] = w_ref[...].astype(BF16)

    for s in range(x_ref.shape[0] // TAIL_SUB):
        rows = slice(s * TAIL_SUB, (s + 1) * TAIL_SUB)
        h = hf_ref[rows, :].astype(F32) + hb_ref[rows, :].astype(F32)
        ya = (h * za_ref[rows, :].astype(F32)).astype(BF16)
        ya = jnp.dot(ya, w_scr[0], preferred_element_type=F32)
        yb = (att_ref[rows, :].astype(F32) * zb_ref[rows, :].astype(F32)).astype(BF16)
        yb = jnp.dot(yb, w_scr[1], preferred_element_type=F32)
        merged = ga_ref[rows, :].astype(F32) * ya + gb_ref[rows, :].astype(F32) * yb
        res = jnp.dot(merged.astype(BF16), w_scr[2], preferred_element_type=F32)
        xo = x_ref[rows, :] + mod_ref[:, 2 * d:3 * d] * res
        if final:
            ms = jnp.mean(xo * xo, axis=-1, keepdims=True)
            xo = xo * lax.rsqrt(ms + EPS) * gf_ref[...]
        o_ref[rows, :] = xo


def _tail(x2, mod3, hf, hb, proj, att2, w_pa, w_pb, w_o, g_final, seq, cols, final):
    n, d = x2.shape
    za_col, zb_col, ga_col, gb_col = cols
    tok = lambda i: (i, 0)
    wspec = pl.BlockSpec((d, d), lambda i: (0, 0), pipeline_mode=pl.Buffered(1))
    return pl.pallas_call(
        functools.partial(_tail_kernel, final),
        grid=(n // TAIL_TM,),
        in_specs=[
            pl.BlockSpec((TAIL_TM, d), tok),
            pl.BlockSpec((None, 1, 3 * d), lambda i: ((i * TAIL_TM) // seq, 0, 0)),
            pl.BlockSpec((TAIL_TM, d), tok),
            pl.BlockSpec((TAIL_TM, d), tok),
            pl.BlockSpec((TAIL_TM, d), lambda i: (i, za_col)),
            pl.BlockSpec((TAIL_TM, d), lambda i: (i, zb_col)),
            pl.BlockSpec((TAIL_TM, d), lambda i: (i, ga_col)),
            pl.BlockSpec((TAIL_TM, d), lambda i: (i, gb_col)),
            pl.BlockSpec((TAIL_TM, d), tok),
            wspec, wspec, wspec,
            pl.BlockSpec((1, d), lambda i: (0, 0)),
        ],
        out_specs=pl.BlockSpec((TAIL_TM, d), tok),
        out_shape=jax.ShapeDtypeStruct((n, d), F32),
        scratch_shapes=[pltpu.VMEM((3, d, d), BF16)],
        compiler_params=pltpu.CompilerParams(
            dimension_semantics=("arbitrary",), vmem_limit_bytes=VMEM_LIMIT),
        name="tail",
    )(x2, mod3, hf, hb, proj, proj, proj, proj, att2, w_pa, w_pb, w_o, g_final)


def kernel(x, c, g_pre, w_c, b_c, w_in, conv_w, conv_b, w_r_f, b_r_f, w_i_f, b_i_f, lam_f,
           w_r_b, b_r_b, w_i_b, b_i_b, lam_b, rpb, w_pa, w_pb, w_o, g_final):
    nb, seq, d = x.shape
    depth = g_pre.shape[0]
    n = nb * seq
    rows = seq // GRID_W
    groups = w_in.shape[2] // d
    xa_col, za_col, q_col, k_col, v_col, zb_col, ga_col, gb_col = range(groups)
    assert xa_col == 0
    acts = ["id"] * groups
    acts[za_col] = acts[zb_col] = "silu"
    acts[ga_col] = acts[gb_col] = "sigmoid"
    acts[q_col] = "qscale"

    x2 = x.reshape(n, d)
    for l in range(depth):
        mod3 = _modulation(c, w_c[l], b_c[l].reshape(1, 3 * d))

        proj = _projection(x2, mod3, g_pre[l].reshape(1, d), w_in[l], seq, tuple(acts))
        proj3 = proj.reshape(nb, seq, groups * d)

        cw, cb = conv_w[l], conv_b[l].reshape(1, d)
        hf, xh = _rglru(proj3, cw, cb, w_r_f[l], w_i_f[l], b_r_f[l].reshape(1, d),
                        b_i_f[l].reshape(1, d), lam_f[l].reshape(1, d), False)
        hb = _rglru(proj3, cw, cb, w_r_b[l], w_i_b[l], b_r_b[l].reshape(1, d),
                    b_i_b[l].reshape(1, d), lam_b[l].reshape(1, d), True, xh=xh)

        att = _natten(proj3, _bias_table(rpb[l], rows), q_col, k_col, v_col)

        x2 = _tail(x2, mod3, hf.reshape(n, d), hb.reshape(n, d), proj, att.reshape(n, d),
                   w_pa[l], w_pb[l], w_o[l],
                   g_final.reshape(1, d), seq, (za_col, zb_col, ga_col, gb_col),
                   final=(l == depth - 1))
    return x2.reshape(nb, seq, d)
```

```python
import functools

import jax
import jax.numpy as jnp
import numpy as np
from jax import lax
from jax.experimental import pallas as pl
from jax.experimental.pallas import tpu as pltpu

F32 = jnp.float32
BF16 = jnp.bfloat16

SUBLANES = 8
LANES = 128
MXU_TILE = 256

GRID_W = 64
A_BLOCKS = 16
CONV_W = 4
RG_C = 8.0
N_HEADS = 16
HEAD_DIM = 64
WIN_R = 8
WIN_C = 16
EPS = 1e-6
MASK_BIAS = -1e30
LOG2_E = 1.4426950408889634
RSQRT_FLOOR = 1e-30

PROJ_TM = 512
NORM_RB = 64
CAST_RB = 128
SCAN_TC = 128
TAIL_TM = 512
TAIL_SUB = 256
TAIL_SLOTS = 3
NAT_ROWS = 8
BIAS_STEPS = 2
MOD_ROWS = 16

VMEM_LIMIT = 56 * 1024 * 1024


def _sigmoid(x):
    return 0.5 * jnp.tanh(0.5 * x) + 0.5


def _silu(x):
    return x * _sigmoid(x)


_ACTIVATIONS = {
    "id": lambda v: v,
    "silu": _silu,
    "sigmoid": _sigmoid,
    "qscale": lambda v: v * (HEAD_DIM ** -0.5 * LOG2_E),
}


def _mod_kernel(c_ref, w_ref, b_ref, o_ref, pad_ref):
    nb = c_ref.shape[0]
    pad_ref[...] = jnp.zeros_like(pad_ref)
    pad_ref[0:nb, :] = c_ref[...]
    sc = _silu(pad_ref[...]).astype(BF16)
    w = w_ref[...].astype(BF16)
    mod = jnp.dot(sc, w, preferred_element_type=F32) + b_ref[...]
    for b in range(nb):
        o_ref[b] = mod[b:b + 1, :]


def _modulation(c, w_c, b_c):
    nb, d = c.shape
    assert nb <= MOD_ROWS
    return pl.pallas_call(
        _mod_kernel,
        grid=(3,),
        in_specs=[
            pl.BlockSpec((nb, d), lambda j: (0, 0)),
            pl.BlockSpec((d, d), lambda j: (0, j)),
            pl.BlockSpec((1, d), lambda j: (0, j)),
        ],
        out_specs=pl.BlockSpec((nb, 1, d), lambda j: (0, 0, j)),
        out_shape=jax.ShapeDtypeStruct((nb, 1, 3 * d), F32),
        scratch_shapes=[pltpu.VMEM((MOD_ROWS, d), F32)],
        compiler_params=pltpu.CompilerParams(
            dimension_semantics=("parallel",), vmem_limit_bytes=VMEM_LIMIT),
        name="mod",
    )(c, w_c, b_c)


def _norm_rows(x_ref, mod_ref, g, h_ref, slot, r0):
    d = x_ref.shape[1]
    x = x_ref[pl.ds(r0, NORM_RB), :]
    ms = jnp.mean(x * x, axis=-1, keepdims=True)
    y = x * lax.rsqrt(ms + EPS) * g
    h = y * (1.0 + mod_ref[:, d:2 * d]) + mod_ref[:, 0:d]
    h_ref[slot, pl.ds(r0, NORM_RB), :] = h.astype(BF16)


def _proj_kernel(acts, x0_ref, mod0_ref, xn_ref, modn_ref, g_ref, w_hbm, o_ref,
                 h_ref, w_ref, stage_ref, sem):
    i = pl.program_id(0)
    tm, d = xn_ref.shape
    g = g_ref[...]
    cur = i % 2

    def weight_copy(j):
        return pltpu.make_async_copy(w_hbm.at[:, pl.ds(j * d, d)], stage_ref.at[j % 2],
                                     sem.at[j % 2])

    def norm_next_tile():
        for k in range(tm // NORM_RB):
            _norm_rows(xn_ref, modn_ref, g, h_ref, 1 - cur, k * NORM_RB)

    def group(j):
        cols = slice(j * d, (j + 1) * d)
        acc = jnp.dot(h_ref[cur], w_ref[:, cols], preferred_element_type=F32)
        o_ref[:, cols] = _ACTIVATIONS[acts[j]](acc).astype(BF16)

    @pl.when(i == 0)
    def _():
        for j in range(min(2, len(acts))):
            weight_copy(j).start()

        def body(k, carry):
            _norm_rows(x0_ref, mod0_ref, g, h_ref, 0, pl.multiple_of(k * NORM_RB, NORM_RB))
            return carry
        lax.fori_loop(0, tm // NORM_RB, body, 0)

        for j in range(len(acts)):
            weight_copy(j).wait()

            def cast(k, carry, j=j):
                r0 = pl.multiple_of(k * CAST_RB, CAST_RB)
                w_ref[pl.ds(r0, CAST_RB), j * d:(j + 1) * d] = (
                    stage_ref[j % 2, pl.ds(r0, CAST_RB), :].astype(BF16))
                return carry
            lax.fori_loop(0, d // CAST_RB, cast, 0)
            if j + 2 < len(acts):
                weight_copy(j + 2).start()
            group(j)
        norm_next_tile()

    @pl.when(i > 0)
    def _():
        norm_next_tile()
        for j in range(len(acts)):
            group(j)


def _projection(x2, mod3, g_pre, w_in, seq, acts):
    n, d = x2.shape
    n_tiles = n // PROJ_TM
    nxt = lambda i: jnp.minimum(i + 1, n_tiles - 1)
    return pl.pallas_call(
        functools.partial(_proj_kernel, acts),
        grid=(n_tiles,),
        in_specs=[
            pl.BlockSpec((PROJ_TM, d), lambda i: (0, 0), pipeline_mode=pl.Buffered(1)),
            pl.BlockSpec((None, 1, 3 * d), lambda i: (0, 0, 0)),
            pl.BlockSpec((PROJ_TM, d), lambda i: (nxt(i), 0)),
            pl.BlockSpec((None, 1, 3 * d), lambda i: ((nxt(i) * PROJ_TM) // seq, 0, 0)),
            pl.BlockSpec((1, d), lambda i: (0, 0)),
            pl.BlockSpec(memory_space=pl.ANY),
        ],
        out_specs=pl.BlockSpec((PROJ_TM, w_in.shape[1]), lambda i: (i, 0)),
        out_shape=jax.ShapeDtypeStruct((n, w_in.shape[1]), BF16),
        scratch_shapes=[
            pltpu.VMEM((2, PROJ_TM, d), BF16),
            pltpu.VMEM(w_in.shape, BF16),
            pltpu.VMEM((2, d, d), w_in.dtype),
            pltpu.SemaphoreType.DMA((2,)),
        ],
        compiler_params=pltpu.CompilerParams(
            dimension_semantics=("arbitrary",), vmem_limit_bytes=VMEM_LIMIT),
        name="proj",
    )(x2, mod3, x2, mod3, g_pre, w_in)


def _rglru_kernel(rev, reuse, *refs):
    if reuse:
        xh_ref, wr_ref, wi_ref, br_ref, bi_ref, lam_ref, o_ref = refs[:7]
        wg_ref, sa_ref, su_ref, sh_ref, carry_ref = refs[7:]
        nb, tc, d = xh_ref.shape
    else:
        (xa_ref, prev_ref, next_ref, shift_ref, cw_ref, cb_ref, wr_ref, wi_ref, br_ref, bi_ref,
         lam_ref, o_ref, xh_out_ref) = refs[:13]
        wg_ref, sa_ref, su_ref, sh_ref, carry_ref = refs[13:]
        nb, tc, d = xa_ref.shape
    i = pl.program_id(0)
    n_chunks = pl.num_programs(0) - 1
    groups = tc // SUBLANES
    n_tiles = d // MXU_TILE
    n_blk, bw, _ = wr_ref.shape
    per_tile = MXU_TILE // bw
    ig = jnp.minimum(i, n_chunks - 1)
    ci = n_chunks - 1 - ig if rev else ig
    cur = i % 2
    old = 1 - cur

    @pl.when(i == 0)
    def _():
        carry_ref[...] = jnp.zeros_like(carry_ref)
        sa_ref[1] = jnp.zeros(sa_ref.shape[1:], F32)
        su_ref[1] = jnp.zeros(su_ref.shape[1:], F32)
        wg_ref[...] = jnp.zeros_like(wg_ref)
        for gate, w_ref in enumerate((wr_ref, wi_ref)):
            for blk in range(n_blk):
                off = (blk % per_tile) * bw
                wg_ref[gate, blk // per_tile, off:off + bw, off:off + bw] = (
                    w_ref[blk].astype(BF16))

    has_prev = ci > 0
    has_next = ci < n_chunks - 1
    lam = lam_ref[...]
    softplus_neg_lam = jnp.maximum(-lam, 0.0) + jnp.log1p(jnp.exp(-jnp.abs(lam)))
    c8l = (-0.5 * RG_C * LOG2_E) * softplus_neg_lam
    brh = 0.5 * br_ref[...]
    bih = 0.5 * bi_ref[...]
    if not reuse:
        cwh = 0.5 * cw_ref[...]
        cbh = 0.5 * cb_ref[...]
        shift_m = shift_ref[...]
        taps = [k for k in range(CONV_W) if k != CONV_W // 2]

    def scan_row(b):
        h = carry_ref[b]
        for gg in range(groups):
            g = groups - 1 - gg if rev else gg
            for jj in range(SUBLANES):
                j = SUBLANES - 1 - jj if rev else jj
                tok = pl.ds(j, SUBLANES, stride=SUBLANES)
                h = sa_ref[old, b, g, tok, :] * h + su_ref[old, b, g, tok, :]
                sh_ref[b, g, tok, :] = h
            if gg % 2 == 1:
                gp = g // 2
                r0 = gp * 2 * SUBLANES
                for cblk in range(d // LANES):
                    blk = slice(cblk * SUBLANES, (cblk + 1) * SUBLANES)
                    rows16 = jnp.concatenate(
                        [sh_ref[b, 2 * gp, blk, :], sh_ref[b, 2 * gp + 1, blk, :]], axis=0)
                    o_ref[b, pl.ds(r0, 2 * SUBLANES), cblk * LANES:(cblk + 1) * LANES] = (
                        rows16.astype(o_ref.dtype))
        carry_ref[b] = h

    def conv_half(b):
        xb = xa_ref[b]
        halo = jnp.concatenate([
            jnp.where(has_prev, prev_ref[b], jnp.zeros_like(prev_ref[b])),
            jnp.where(has_next, next_ref[b], jnp.zeros_like(next_ref[b]))], axis=0)
        sh = jnp.dot(shift_m, jnp.concatenate([xb, halo], axis=0), preferred_element_type=F32)
        xh = cbh + cwh[CONV_W // 2:CONV_W // 2 + 1] * xb.astype(F32)
        for n, k in enumerate(taps):
            xh = xh + cwh[k:k + 1] * sh[n * tc:(n + 1) * tc]
        return xh

    def gates(b):
        if reuse:
            xhb = xh_ref[b]
            xh = xhb.astype(F32)
        else:
            xh = conv_half(b)
            xhb = xh.astype(BF16)
            xh_out_ref[b] = xhb
        for k in range(n_tiles):
            sl = slice(k * MXU_TILE, (k + 1) * MXU_TILE)
            xk = xhb[:, sl]
            tr = jnp.tanh(jnp.dot(xk, wg_ref[0, k], preferred_element_type=F32) + brh[:, sl])
            ti = jnp.tanh(jnp.dot(xk, wg_ref[1, k], preferred_element_type=F32) + bih[:, sl])
            a = jnp.exp2(c8l[:, sl] * (tr + 1.0))
            z = 1.0 - a * a
            u = (z * lax.rsqrt(jnp.maximum(z, RSQRT_FLOOR))) * ((ti + 1.0) * xh[:, sl])
            for c2 in range(MXU_TILE // LANES):
                cblk = k * (MXU_TILE // LANES) + c2
                rows = slice(cblk * SUBLANES, (cblk + 1) * SUBLANES)
                lanes = slice(c2 * LANES, (c2 + 1) * LANES)
                sa_ref[cur, b, :, rows, :] = a[:, lanes].reshape(groups, SUBLANES, LANES)
                su_ref[cur, b, :, rows, :] = u[:, lanes].reshape(groups, SUBLANES, LANES)

    def body(b, carry):
        scan_row(b)
        gates(b)
        return carry

    lax.fori_loop(0, nb, body, 0)


def _shift_matrix(tc):
    taps = [k for k in range(CONV_W) if k != CONV_W // 2]
    m = np.zeros((len(taps) * tc, tc + 2 * SUBLANES), np.float32)
    for n, k in enumerate(taps):
        for t in range(tc):
            s = t - CONV_W // 2 + k
            if s < 0:
                col = tc + SUBLANES + s
            elif s >= tc:
                col = tc + SUBLANES + (s - tc)
            else:
                col = s
            m[n * tc + t, col] = 1.0
    return m


def _rglru(proj3, conv_w, conv_b, w_r, w_i, b_r, b_i, lam, rev, xh=None):
    nb, seq, _ = proj3.shape
    reuse = xh is not None
    d = w_r.shape[0] * w_r.shape[1]
    n_chunks = seq // SCAN_TC
    n_groups = seq // SUBLANES
    gpc = SCAN_TC // SUBLANES

    def chunk(i):
        return n_chunks - 1 - i if rev else i

    def gated(i):
        return chunk(jnp.minimum(i, n_chunks - 1))

    def scanned(i):
        return chunk(jnp.maximum(i - 1, 0))

    n_tiles = d // MXU_TILE
    gate_args = (w_r, w_i, b_r, b_i, lam)
    whole = lambda a: pl.BlockSpec(a.shape, lambda i: (0,) * a.ndim)
    gated_spec = pl.BlockSpec((nb, SCAN_TC, d), lambda i: (0, gated(i), 0))
    h_spec = pl.BlockSpec((nb, SCAN_TC, d), lambda i: (0, scanned(i), 0))
    h_shape = jax.ShapeDtypeStruct((nb, seq, d), BF16)
    if reuse:
        in_specs = [gated_spec] + [whole(a) for a in gate_args]
        args = (xh,) + gate_args
        out_specs, out_shape = h_spec, h_shape
    else:
        proj4 = proj3.reshape(nb, n_groups, SUBLANES, proj3.shape[2])
        shift_m = jnp.asarray(_shift_matrix(SCAN_TC), BF16)
        in_specs = [
            gated_spec,
            pl.BlockSpec((nb, None, SUBLANES, d),
                         lambda i: (0, jnp.maximum(gated(i) * gpc - 1, 0), 0, 0)),
            pl.BlockSpec((nb, None, SUBLANES, d),
                         lambda i: (0, jnp.minimum((gated(i) + 1) * gpc, n_groups - 1), 0, 0)),
            whole(shift_m), whole(conv_w), whole(conv_b),
        ] + [whole(a) for a in gate_args]
        args = (proj3, proj4, proj4, shift_m, conv_w, conv_b) + gate_args
        out_specs, out_shape = [h_spec, gated_spec], [h_shape, h_shape]
    return pl.pallas_call(
        functools.partial(_rglru_kernel, rev, reuse),
        grid=(n_chunks + 1,),
        in_specs=in_specs,
        out_specs=out_specs,
        out_shape=out_shape,
        scratch_shapes=[
            pltpu.VMEM((2, n_tiles, MXU_TILE, MXU_TILE), BF16),
            pltpu.VMEM((2, nb, gpc, d // LANES * SUBLANES, LANES), F32),
            pltpu.VMEM((2, nb, gpc, d // LANES * SUBLANES, LANES), F32),
            pltpu.VMEM((nb, gpc, d // LANES * SUBLANES, LANES), F32),
            pltpu.VMEM((nb, SUBLANES, LANES), F32),
        ],
        compiler_params=pltpu.CompilerParams(
            dimension_semantics=("arbitrary",), vmem_limit_bytes=VMEM_LIMIT),
        name="rglru_bwd" if rev else "rglru_fwd",
    )(*args)


def _natten_kernel(k_col, v_col, q_ref, kv_hbm, bias_ref, o_ref, k_buf, v_buf, sem):
    batch = pl.program_id(0)
    step = pl.program_id(1)
    n_steps = pl.num_programs(1)
    rows = n_steps * NAT_ROWS
    gw = GRID_W
    d = q_ref.shape[1]
    win_r = min(WIN_R, rows)
    held = k_buf.shape[1] // gw

    def first_held_row(s):
        return jnp.clip(s * NAT_ROWS - win_r // 2, 0, rows - held)

    def window_copies(b, s, slot):
        tok = pl.multiple_of(first_held_row(s) * gw, gw)
        return [pltpu.make_async_copy(kv_hbm.at[b, pl.ds(tok, held * gw), pl.ds(col * d, d)],
                                      buf.at[slot], sem.at[n, slot])
                for n, (col, buf) in enumerate(((k_col, k_buf), (v_col, v_buf)))]

    t = batch * n_steps + step
    slot = t % 2

    @pl.when(t == 0)
    def _():
        for c in window_copies(0, 0, 0):
            c.start()

    @pl.when(t + 1 < pl.num_programs(0) * n_steps)
    def _():
        wrap = step + 1 == n_steps
        for c in window_copies(jnp.where(wrap, batch + 1, batch), jnp.where(wrap, 0, step + 1),
                               1 - slot):
            c.start()

    for c in window_copies(batch, step, slot):
        c.wait()

    lane = lax.broadcasted_iota(jnp.int32, (gw, LANES), 1)
    lo = lane < HEAD_DIM
    zero = jnp.zeros((gw, LANES), BF16)
    n_pairs = d // LANES
    w0 = first_held_row(step)
    wins = []
    pats = []
    scores = []
    for rr in range(NAT_ROWS):
        r = step * NAT_ROWS + rr
        r0 = jnp.clip(r - win_r // 2, 0, rows - win_r)
        k0 = pl.multiple_of((r0 - w0) * gw, gw)
        wins.append(k0)
        pats.append(r0 - r + (WIN_R - 1))
        for p in range(n_pairs):
            sl = slice(p * LANES, (p + 1) * LANES)
            qp = q_ref[rr * gw:(rr + 1) * gw, sl]
            qs = jnp.concatenate([jnp.where(lo, qp, zero), jnp.where(lo, zero, qp)], axis=0)
            scores.append(lax.dot_general(qs, k_buf[slot, pl.ds(k0, win_r * gw), sl],
                                          (((1,), (1,)), ((), ())), preferred_element_type=F32))
    probs = []
    for rr in range(NAT_ROWS):
        for p in range(n_pairs):
            bias = jnp.concatenate([bias_ref[pats[rr] + 2 * t, p] for t in range(win_r // 2)],
                                   axis=1)
            s = scores[rr * n_pairs + p] + bias
            m = jnp.max(s, axis=-1, keepdims=True)
            probs.append(jnp.exp2(s - m).astype(BF16))
    ones = jnp.ones((win_r * gw, LANES), BF16)
    for rr in range(NAT_ROWS):
        for p in range(n_pairs):
            sl = slice(p * LANES, (p + 1) * LANES)
            v_ext = jnp.concatenate([v_buf[slot, pl.ds(wins[rr], win_r * gw), sl], ones], axis=1)
            pv = jnp.dot(probs[rr * n_pairs + p], v_ext, preferred_element_type=F32)
            o = pv[:, :LANES] / pv[:, LANES:]
            o_ref[rr * gw:(rr + 1) * gw, sl] = jnp.where(lo, o[:gw], o[gw:]).astype(BF16)


def _natten(proj3, bias_tab, q_col, k_col, v_col):
    nb, seq, _ = proj3.shape
    d = N_HEADS * HEAD_DIM
    rows = seq // GRID_W
    held = min(rows, NAT_ROWS + min(WIN_R, rows))
    return pl.pallas_call(
        functools.partial(_natten_kernel, k_col, v_col),
        grid=(nb, rows // NAT_ROWS),
        in_specs=[
            pl.BlockSpec((None, NAT_ROWS * GRID_W, d), lambda b, i: (b, i, q_col)),
            pl.BlockSpec(memory_space=pl.ANY),
            pl.BlockSpec(bias_tab.shape, lambda b, i: (0, 0, 0, 0), pipeline_mode=pl.Buffered(1)),
        ],
        out_specs=pl.BlockSpec((None, NAT_ROWS * GRID_W, d), lambda b, i: (b, i, 0)),
        out_shape=jax.ShapeDtypeStruct((nb, seq, d), BF16),
        scratch_shapes=[pltpu.VMEM((2, held * GRID_W, d), BF16),
                        pltpu.VMEM((2, held * GRID_W, d), BF16),
                        pltpu.SemaphoreType.DMA((2, 2))],
        compiler_params=pltpu.CompilerParams(
            dimension_semantics=("arbitrary", "arbitrary"), vmem_limit_bytes=VMEM_LIMIT),
        name="natten",
    )(proj3, proj3, bias_tab)


def _bias_kernel(rp_ref, o_ref):
    per_step, n_pairs = o_ref.shape[:2]
    gw = GRID_W
    q_idx = lax.broadcasted_iota(jnp.int32, (gw, LANES), 0)
    lane = lax.broadcasted_iota(jnp.int32, (gw, LANES), 1)
    col = jnp.where(lane < gw, lane, lane - gw)
    c0 = jnp.clip(q_idx - WIN_C // 2, 0, gw - WIN_C)
    valid = (col >= c0) & (col < c0 + WIN_C)
    first = lane < gw
    for k in range(per_step):
        e = pl.program_id(0) * per_step + k
        for pair in range(n_pairs):
            for hh in range(2):
                h = 2 * pair + hh
                va = jnp.broadcast_to(rp_ref[h, pl.ds(e, 1), :], (gw, LANES))
                vb = jnp.broadcast_to(rp_ref[h, pl.ds(e + 1, 1), :], (gw, LANES))
                ta = pltpu.roll(va, 0, 1, stride=1, stride_axis=0)
                tb = pltpu.roll(vb, gw, 1, stride=1, stride_axis=0)
                tile = jnp.where(valid, LOG2_E * jnp.where(first, ta, tb), MASK_BIAS)
                o_ref[k, pair, hh * gw:(hh + 1) * gw, :] = tile


def _bias_table(rpb, rows):
    win_r = min(WIN_R, rows)
    assert 2 * GRID_W == LANES and win_r % 2 == 0
    nh, nd, _ = rpb.shape
    rp = jnp.concatenate([rpb[:, :, WIN_C - 1:],
                          jnp.zeros((nh, nd, LANES - (2 * WIN_C - 1)), rpb.dtype),
                          rpb[:, :, :WIN_C - 1]], axis=-1).astype(F32)
    n_off = WIN_R + win_r - 2
    assert n_off % BIAS_STEPS == 0
    return pl.pallas_call(
        _bias_kernel,
        grid=(BIAS_STEPS,),
        in_specs=[pl.BlockSpec(rp.shape, lambda e: (0, 0, 0))],
        out_specs=pl.BlockSpec((n_off // BIAS_STEPS, nh // 2, 2 * GRID_W, LANES),
                               lambda e: (e, 0, 0, 0)),
        out_shape=jax.ShapeDtypeStruct((n_off, nh // 2, 2 * GRID_W, LANES), F32),
        compiler_params=pltpu.CompilerParams(
            dimension_semantics=("parallel",), vmem_limit_bytes=VMEM_LIMIT),
        name="bias_table",
    )(rp)


def _tail_kernel(final, cols, mod_ref, x_hbm, hf_hbm, hb_hbm, att_hbm, proj_hbm,
                 wpa_ref, wpb_ref, wo_ref, gf_ref, o_ref, w_scr, x_buf, b_buf, sem):
    i = pl.program_id(0)
    n_steps = pl.num_programs(0)
    _, tm, d = x_buf.shape
    ahead = TAIL_SLOTS - 1

    def tile_copies(s, slot):
        r = pl.ds(pl.multiple_of(s * tm, tm), tm)
        srcs = [x_hbm.at[r, :], hf_hbm.at[r, :], hb_hbm.at[r, :], att_hbm.at[r, :]]
        srcs += [proj_hbm.at[r, pl.ds(c * d, d)] for c in cols]
        dsts = [x_buf.at[slot]] + [b_buf.at[slot, n] for n in range(len(srcs) - 1)]
        return [pltpu.make_async_copy(src, dst, sem.at[n, slot])
                for n, (src, dst) in enumerate(zip(srcs, dsts))]

    @pl.when(i == 0)
    def _():
        for s in range(ahead):
            @pl.when(s < n_steps)
            def _():
                for c in tile_copies(s, s):
                    c.start()
        for n, w_ref in enumerate((wpa_ref, wpb_ref, wo_ref)):
            w_scr[n] = w_ref[...].astype(BF16)

    @pl.when(i + ahead < n_steps)
    def _():
        for c in tile_copies(i + ahead, (i + ahead) % TAIL_SLOTS):
            c.start()

    slot = i % TAIL_SLOTS
    for c in tile_copies(i, slot):
        c.wait()
    hf_ref, hb_ref, att_ref, za_ref, zb_ref, ga_ref, gb_ref = (
        b_buf.at[slot, n] for n in range(b_buf.shape[1]))
    x_ref = x_buf.at[slot]

    for s in range(tm // TAIL_SUB):
        rows = slice(s * TAIL_SUB, (s + 1) * TAIL_SUB)
        h = hf_ref[rows, :].astype(F32) + hb_ref[rows, :].astype(F32)
        ya = (h * za_ref[rows, :].astype(F32)).astype(BF16)
        ya = jnp.dot(ya, w_scr[0], preferred_element_type=F32)
        yb = (att_ref[rows, :].astype(F32) * zb_ref[rows, :].astype(F32)).astype(BF16)
        yb = jnp.dot(yb, w_scr[1], preferred_element_type=F32)
        merged = ga_ref[rows, :].astype(F32) * ya + gb_ref[rows, :].astype(F32) * yb
        res = jnp.dot(merged.astype(BF16), w_scr[2], preferred_element_type=F32)
        xo = x_ref[rows, :] + mod_ref[:, 2 * d:3 * d] * res
        if final:
            ms = jnp.mean(xo * xo, axis=-1, keepdims=True)
            xo = xo * lax.rsqrt(ms + EPS) * gf_ref[...]
        o_ref[rows, :] = xo


def _tail(x2, mod3, hf, hb, proj, att2, w_pa, w_pb, w_o, g_final, seq, cols, final):
    n, d = x2.shape
    tok = lambda i: (i, 0)
    hbm = pl.BlockSpec(memory_space=pl.ANY)
    wspec = pl.BlockSpec((d, d), lambda i: (0, 0), pipeline_mode=pl.Buffered(1))
    return pl.pallas_call(
        functools.partial(_tail_kernel, final, cols),
        grid=(n // TAIL_TM,),
        in_specs=[
            pl.BlockSpec((None, 1, 3 * d), lambda i: ((i * TAIL_TM) // seq, 0, 0)),
            hbm, hbm, hbm, hbm, hbm,
            wspec, wspec, wspec,
            pl.BlockSpec((1, d), lambda i: (0, 0)),
        ],
        out_specs=pl.BlockSpec((TAIL_TM, d), tok),
        out_shape=jax.ShapeDtypeStruct((n, d), F32),
        scratch_shapes=[pltpu.VMEM((3, d, d), BF16),
                        pltpu.VMEM((TAIL_SLOTS, TAIL_TM, d), F32),
                        pltpu.VMEM((TAIL_SLOTS, 3 + len(cols), TAIL_TM, d), BF16),
                        pltpu.SemaphoreType.DMA((4 + len(cols), TAIL_SLOTS))],
        compiler_params=pltpu.CompilerParams(
            dimension_semantics=("arbitrary",), vmem_limit_bytes=VMEM_LIMIT),
        name="tail",
    )(mod3, x2, hf, hb, att2, proj, w_pa, w_pb, w_o, g_final)


def kernel(x, c, g_pre, w_c, b_c, w_in, conv_w, conv_b, w_r_f, b_r_f, w_i_f, b_i_f, lam_f,
           w_r_b, b_r_b, w_i_b, b_i_b, lam_b, rpb, w_pa, w_pb, w_o, g_final):
    nb, seq, d = x.shape
    depth = g_pre.shape[0]
    n = nb * seq
    rows = seq // GRID_W
    groups = w_in.shape[2] // d
    xa_col, za_col, q_col, k_col, v_col, zb_col, ga_col, gb_col = range(groups)
    assert xa_col == 0
    acts = ["id"] * groups
    acts[za_col] = acts[zb_col] = "silu"
    acts[ga_col] = acts[gb_col] = "sigmoid"
    acts[q_col] = "qscale"

    x2 = x.reshape(n, d)
    for l in range(depth):
        mod3 = _modulation(c, w_c[l], b_c[l].reshape(1, 3 * d))

        proj = _projection(x2, mod3, g_pre[l].reshape(1, d), w_in[l], seq, tuple(acts))
        proj3 = proj.reshape(nb, seq, groups * d)

        cw, cb = conv_w[l], conv_b[l].reshape(1, d)
        hf, xh = _rglru(proj3, cw, cb, w_r_f[l], w_i_f[l], b_r_f[l].reshape(1, d),
                        b_i_f[l].reshape(1, d), lam_f[l].reshape(1, d), False)
        hb = _rglru(proj3, cw, cb, w_r_b[l], w_i_b[l], b_r_b[l].reshape(1, d),
                    b_i_b[l].reshape(1, d), lam_b[l].reshape(1, d), True, xh=xh)

        att = _natten(proj3, _bias_table(rpb[l], rows), q_col, k_col, v_col)

        x2 = _tail(x2, mod3, hf.reshape(n, d), hb.reshape(n, d), proj, att.reshape(n, d),
                   w_pa[l], w_pb[l], w_o[l],
                   g_final.reshape(1, d), seq, (za_col, zb_col, ga_col, gb_col),
                   final=(l == depth - 1))
    return x2.reshape(nb, seq, d)
```
